```python
import math
import jax, jax.numpy as jnp
from jax import lax
import numpy as np

D_MODEL = 2048
BATCH = 4
SEQ = 2048
DEPTH = 2
DEC_BATCH = 128
DEC_SEQ = 8
PAST_LEN = 2048
PAGE_SIZE = 128

N_EVEN = (DEPTH + 1) // 2
N_ODD = DEPTH // 2
C_CONV = D_MODEL // 2
CONV_W = 31
DA_HEADS = 4
DA_HEAD_DIM = 128
DA_WIDTH = DA_HEADS * 2 * DA_HEAD_DIM
IN_EVEN = 2 * C_CONV + 3 * DA_WIDTH
OUT_EVEN = C_CONV + DA_WIDTH
Q_BLOCK = 128
REL_BUCKETS = 32
REL_MAX_EXACT = REL_BUCKETS // 2
REL_MAX_DIST = 128
RW_HEAD = 64
RW_HEADS = D_MODEL // RW_HEAD
RW_DECAY_LORA = 96
RW_A_LORA = 96
RW_GATE_LORA = 256
RW_GN_EPS = 64e-5
N_MEM = 256
X_HEADS = 4
X_HEAD_DIM = 128
X_WIDTH = X_HEADS * X_HEAD_DIM
D_FF = 4 * D_MODEL
NORM_EPS = 1e-6
LN_EPS = 1e-5

kernel_name = 'hybrid_conv_diffattn_rwkv7_decoder_step'


def rmsnorm(x, g):
    xf = x.astype(jnp.float32)
    y = xf * lax.rsqrt(jnp.mean(xf * xf, axis=-1, keepdims=True) + NORM_EPS)
    return (y * g.astype(jnp.float32)).astype(x.dtype)


def layernorm(x, g, b, eps):
    xf = x.astype(jnp.float32)
    mu = jnp.mean(xf, axis=-1, keepdims=True)
    var = jnp.mean(jnp.square(xf - mu), axis=-1, keepdims=True)
    return ((xf - mu) * lax.rsqrt(var + eps) * g.astype(jnp.float32) + b.astype(jnp.float32)).astype(x.dtype)


def rel_bias(q_pos, k_pos, table):
    n = jnp.maximum(q_pos[:, None] - k_pos[None, :], 0)
    nf = jnp.maximum(n, 1).astype(jnp.float32)
    large = REL_MAX_EXACT + (jnp.log(nf / REL_MAX_EXACT) / math.log(REL_MAX_DIST / REL_MAX_EXACT)
                             * (REL_BUCKETS - REL_MAX_EXACT)).astype(jnp.int32)
    large = jnp.minimum(large, REL_BUCKETS - 1)
    bucket = jnp.where(n < REL_MAX_EXACT, n, large)
    return jnp.transpose(table[bucket].astype(jnp.float32), (2, 0, 1))


def even_in(u, w_in):
    b, t = u.shape[:2]
    z = u @ w_in
    ga, gb, q, k, v = jnp.split(z, [C_CONV, 2 * C_CONV, 2 * C_CONV + DA_WIDTH, 2 * C_CONV + 2 * DA_WIDTH], axis=-1)
    glu = ga * jax.nn.sigmoid(gb)
    q = q.reshape(b, t, DA_HEADS, 2, DA_HEAD_DIM)
    k = k.reshape(b, t, DA_HEADS, 2, DA_HEAD_DIM)
    v = v.reshape(b, t, DA_HEADS, 2 * DA_HEAD_DIM)
    return glu, q, k, v


def conv_branch(glu, conv_prev, conv_w, conv_b, ln_g, ln_b):
    xp = jnp.concatenate([conv_prev.astype(glu.dtype), glu], axis=1)
    y = lax.conv_general_dilated(xp, conv_w[:, None, :], (1,), 'VALID',
                                 dimension_numbers=('NWC', 'WIO', 'NWC'),
                                 feature_group_count=C_CONV) + conv_b
    y = layernorm(y, ln_g, ln_b, LN_EPS)
    return jax.nn.silu(y), xp[:, -(CONV_W - 1):]


def diff_lambda(lq1, lk1, lq2, lk2, lam_init):
    f32 = jnp.float32
    return (jnp.exp(jnp.sum(lq1.astype(f32) * lk1.astype(f32)))
            - jnp.exp(jnp.sum(lq2.astype(f32) * lk2.astype(f32))) + lam_init)


def diff_scores(q, k, q_pos, k_pos, rel_table):
    s = jnp.einsum('bqhcd,bkhcd->bhcqk', q, k, preferred_element_type=jnp.float32) * (DA_HEAD_DIM ** -0.5)
    s = s + rel_bias(q_pos, k_pos, rel_table)[None, :, None]
    return jnp.where(k_pos[None, :] <= q_pos[:, None], s, -jnp.inf)


def diff_weights(s, lam):
    p = jax.nn.softmax(s, axis=-1)
    return p[:, :, 0] - lam * p[:, :, 1]


def diff_attn_prompt(q, k, v, rel_table, lam):
    b, t = q.shape[:2]
    nb = t // Q_BLOCK
    qb = jnp.moveaxis(q.reshape(b, nb, Q_BLOCK, DA_HEADS, 2, DA_HEAD_DIM), 1, 0)
    k_pos = jnp.arange(t, dtype=jnp.int32)

    def one_block(args):
        q_blk, i = args
        q_pos = i * Q_BLOCK + jnp.arange(Q_BLOCK, dtype=jnp.int32)
        a = diff_weights(diff_scores(q_blk, k, q_pos, k_pos, rel_table), lam)
        return jnp.einsum('bhqk,bkhe->bqhe', a.astype(v.dtype), v)

    o = lax.map(one_block, (qb, jnp.arange(nb, dtype=jnp.int32)))
    return jnp.moveaxis(o, 0, 1).reshape(b, t, DA_HEADS, 2 * DA_HEAD_DIM)


def diff_attn_sample(q, k_new, v_new, k_pool, v_pool, page_table, ei, rel_table, lam):
    bd, t = q.shape[:2]
    n_pages = page_table.shape[1]
    past = n_pages * PAGE_SIZE
    k_past = k_pool[ei, page_table].reshape(bd, past, DA_HEADS, 2, DA_HEAD_DIM)
    v_past = v_pool[ei, page_table].reshape(bd, past, DA_HEADS, 2 * DA_HEAD_DIM)
    q_pos = past + jnp.arange(t, dtype=jnp.int32)
    s = jnp.concatenate([diff_scores(q, k_past, q_pos, jnp.arange(past, dtype=jnp.int32), rel_table),
                         diff_scores(q, k_new, q_pos, q_pos, rel_table)], axis=-1)
    a = diff_weights(s, lam).astype(v_new.dtype)
    return (jnp.einsum('bhqk,bkhe->bqhe', a[..., :past], v_past)
            + jnp.einsum('bhqk,bkhe->bqhe', a[..., past:], v_new))


def _wkv_step(s, inp):
    r_t, w_t, k_t, v_t, a_t, b_t = inp
    sa = jnp.einsum('bhij,bhj->bhi', s, a_t)
    s = s * w_t[:, :, None, :] + sa[..., None] * b_t[:, :, None, :] + v_t[..., None] * k_t[:, :, None, :]
    return s, jnp.einsum('bhij,bhj->bhi', s, r_t)


def rwkv_mixer(u, shift_prev, s0, p, oi):
    b, t, d = u.shape
    f32 = jnp.float32
    prev = jnp.concatenate([shift_prev[:, None].astype(u.dtype), u[:, :-1]], axis=1)
    xx = prev - u
    mix = p['rw_mix'][oi]
    xr, xw, xk, xv, xa, xg = [u + xx * mix[i] for i in range(6)]
    r = xr @ p['rw_wr'][oi]
    wlog = -jax.nn.softplus(-(p['rw_w0'][oi] + jnp.tanh(xw @ p['rw_w1'][oi]) @ p['rw_w2'][oi])) - 0.5
    k = xk @ p['rw_wk'][oi]
    v = xv @ p['rw_wv'][oi]
    a = jax.nn.sigmoid(p['rw_a0'][oi] + (xa @ p['rw_a1'][oi]) @ p['rw_a2'][oi])
    gate = jax.nn.sigmoid(xg @ p['rw_g1'][oi]) @ p['rw_g2'][oi]

    def heads(z):
        return z.astype(f32).reshape(b, t, RW_HEADS, RW_HEAD)

    kk = heads(k * p['rw_kk'][oi])
    kk = kk / jnp.maximum(jnp.sqrt(jnp.sum(kk * kk, axis=-1, keepdims=True)), 1e-12)
    k = k * (1.0 + (a - 1.0) * p['rw_ka'][oi])
    rh, kh, vh, ah = heads(r), heads(k), heads(v), heads(a)
    decay = jnp.exp(-jnp.exp(heads(wlog)))
    xs = tuple(jnp.moveaxis(z, 1, 0) for z in (rh, decay, kh, vh, -kk, kk * ah))
    s_final, y = lax.scan(_wkv_step, s0.astype(f32), xs)
    y = jnp.moveaxis(y, 0, 1)
    mu = jnp.mean(y, axis=-1, keepdims=True)
    var = jnp.mean(jnp.square(y - mu), axis=-1, keepdims=True)
    yn = ((y - mu) * lax.rsqrt(var + RW_GN_EPS)).reshape(b, t, d) * p['rw_lnx_g'][oi].astype(f32) \
        + p['rw_lnx_b'][oi].astype(f32)
    bonus = (jnp.sum(rh * kh * p['rw_rk'][oi].astype(f32), axis=-1, keepdims=True) * vh).reshape(b, t, d)
    out = ((yn + bonus) * gate.astype(f32)).astype(u.dtype) @ p['rw_wo'][oi]
    return out, u[:, -1], s_final


def cross_attn(u, mk, mv, wq, wo):
    b, t = u.shape[:2]
    q = (u @ wq).reshape(b, t, X_HEADS, X_HEAD_DIM)
    s = jnp.einsum('bqhd,bmhd->bhqm', q, mk, preferred_element_type=jnp.float32) * (X_HEAD_DIM ** -0.5)
    pr = jax.nn.softmax(s, axis=-1).astype(mv.dtype)
    o = jnp.einsum('bhqm,bmhd->bqhd', pr, mv)
    return o.reshape(b, t, X_WIDTH) @ wo


def squared_relu_mlp(u, w1, w2):
    return jnp.square(jax.nn.relu(u @ w1)) @ w2


def trunk(x, mem_k, mem_v, conv_prev, shift_prev, wkv_prev, paged, p):
    b, t = x.shape[:2]
    h = x
    ks, vs, convs, shifts, wkvs = [], [], [], [], []
    for li in range(DEPTH):
        g = p['norm_g'][li]
        u = rmsnorm(h, g[0])
        if li % 2 == 0:
            ei = li // 2
            lam_init = 0.8 - 0.6 * math.exp(-0.3 * li)
            glu, q, k, v = even_in(u, p['w_in_even'][ei])
            conv_out, conv_state = conv_branch(glu, conv_prev[ei], p['conv_w'][ei], p['conv_b'][ei],
                                               p['conv_ln_g'][ei], p['conv_ln_b'][ei])
            lam = diff_lambda(p['lambda_q1'][ei], p['lambda_k1'][ei], p['lambda_q2'][ei], p['lambda_k2'][ei], lam_init)
            if paged is None:
                o = diff_attn_prompt(q, k, v, p['rel_table'], lam)
            else:
                o = diff_attn_sample(q, k, v, paged[0], paged[1], paged[2], ei, p['rel_table'], lam)
            o = rmsnorm(o, p['subln_g'][ei]) * (1.0 - lam_init)
            mix = jnp.concatenate([conv_out, o.reshape(b, t, DA_WIDTH)], axis=-1) @ p['w_out_even'][ei]
            ks.append(k.reshape(b, t, DA_HEADS, 2 * DA_HEAD_DIM))
            vs.append(v)
            convs.append(conv_state)
        else:
            oi = li // 2
            mix, sh, s = rwkv_mixer(u, shift_prev[oi], wkv_prev[oi], p, oi)
            shifts.append(sh)
            wkvs.append(s)
        h = h + rmsnorm(mix, g[1])
        h = h + rmsnorm(cross_attn(rmsnorm(h, g[2]), mem_k[li], mem_v[li], p['x_wq'][li], p['x_wo'][li]), g[3])
        h = h + rmsnorm(squared_relu_mlp(rmsnorm(h, g[4]), p['ff_w1'][li], p['ff_w2'][li]), g[5])
    return h, jnp.stack(ks), jnp.stack(vs), jnp.stack(convs), jnp.stack(shifts), jnp.stack(wkvs)


def setup_inputs(seed: int = 0) -> dict:
    key = jax.random.key(seed)
    ks = iter(jax.random.split(key, 64))
    f32 = jnp.float32
    D = D_MODEL

    def nrm(shape, scale):
        return jax.random.normal(next(ks), shape, f32) * scale

    def gain(shape):
        return 1.0 + nrm(shape, 0.02)

    n_pages = PAST_LEN // PAGE_SIZE
    n_pool = (5 * DEC_BATCH * n_pages + 3) // 4
    out = {}
    out['x_prompt'] = nrm((BATCH, SEQ, D), 1.0)
    out['x_sample'] = nrm((DEC_BATCH, DEC_SEQ, D), 1.0)
    out['cache_attn_k'] = nrm((N_EVEN, n_pool, PAGE_SIZE, DA_HEADS, 2 * DA_HEAD_DIM), 1.0)
    out['cache_attn_v'] = nrm((N_EVEN, n_pool, PAGE_SIZE, DA_HEADS, 2 * DA_HEAD_DIM), 1.0)
    out['cache_mem_k'] = nrm((DEPTH, DEC_BATCH, N_MEM, X_HEADS, X_HEAD_DIM), 1.0)
    out['cache_mem_v'] = nrm((DEPTH, DEC_BATCH, N_MEM, X_HEADS, X_HEAD_DIM), 1.0)
    out['state_conv'] = nrm((N_EVEN, DEC_BATCH, CONV_W - 1, C_CONV), 0.5)
    out['state_shift'] = nrm((N_ODD, DEC_BATCH, D), 1.0)
    out['state_wkv'] = nrm((N_ODD, DEC_BATCH, RW_HEADS, RW_HEAD, RW_HEAD), 0.3)
    out['page_table'] = jax.random.permutation(next(ks), n_pool)[:DEC_BATCH * n_pages].reshape(
        DEC_BATCH, n_pages).astype(jnp.int32)
    out['mem_prompt'] = nrm((BATCH, N_MEM, D), 1.0)
    out['norm_g'] = gain((DEPTH, 6, D))
    out['mem_norm_g'] = gain((DEPTH, D))
    out['rel_table'] = nrm((REL_BUCKETS, DA_HEADS), 0.5)
    out['w_in_even'] = nrm((N_EVEN, D, IN_EVEN), D ** -0.5)
    out['conv_w'] = nrm((N_EVEN, CONV_W, C_CONV), CONV_W ** -0.5)
    out['conv_b'] = nrm((N_EVEN, C_CONV), 0.02)
    out['conv_ln_g'] = gain((N_EVEN, C_CONV))
    out['conv_ln_b'] = nrm((N_EVEN, C_CONV), 0.02)
    out['lambda_q1'] = nrm((N_EVEN, DA_HEAD_DIM), 0.1)
    out['lambda_k1'] = nrm((N_EVEN, DA_HEAD_DIM), 0.1)
    out['lambda_q2'] = nrm((N_EVEN, DA_HEAD_DIM), 0.1)
    out['lambda_k2'] = nrm((N_EVEN, DA_HEAD_DIM), 0.1)
    out['subln_g'] = gain((N_EVEN, 2 * DA_HEAD_DIM))
    out['w_out_even'] = nrm((N_EVEN, OUT_EVEN, D), OUT_EVEN ** -0.5)
    out['rw_mix'] = jax.random.uniform(next(ks), (N_ODD, 6, D), f32)
    out['rw_w0'] = -1.0 + nrm((N_ODD, D), 0.5)
    out['rw_w1'] = nrm((N_ODD, D, RW_DECAY_LORA), D ** -0.5)
    out['rw_w2'] = nrm((N_ODD, RW_DECAY_LORA, D), 0.5 * RW_DECAY_LORA ** -0.5)
    out['rw_a0'] = nrm((N_ODD, D), 0.1)
    out['rw_a1'] = nrm((N_ODD, D, RW_A_LORA), D ** -0.5)
    out['rw_a2'] = nrm((N_ODD, RW_A_LORA, D), 0.5 * RW_A_LORA ** -0.5)
    out['rw_g1'] = nrm((N_ODD, D, RW_GATE_LORA), D ** -0.5)
    out['rw_g2'] = nrm((N_ODD, RW_GATE_LORA, D), RW_GATE_LORA ** -0.5)
    out['rw_kk'] = 0.85 + nrm((N_ODD, D), 0.05)
    out['rw_ka'] = 1.0 + nrm((N_ODD, D), 0.05)
    out['rw_rk'] = nrm((N_ODD, RW_HEADS, RW_HEAD), 0.1)
    out['rw_lnx_g'] = gain((N_ODD, D))
    out['rw_lnx_b'] = nrm((N_ODD, D), 0.02)
    out['rw_wr'] = nrm((N_ODD, D, D), D ** -0.5)
    out['rw_wk'] = nrm((N_ODD, D, D), D ** -0.5)
    out['rw_wv'] = nrm((N_ODD, D, D), D ** -0.5)
    out['rw_wo'] = nrm((N_ODD, D, D), D ** -0.5)
    out['x_wq'] = nrm((DEPTH, D, X_WIDTH), D ** -0.5)
    out['x_wk'] = nrm((DEPTH, D, X_WIDTH), D ** -0.5)
    out['x_wv'] = nrm((DEPTH, D, X_WIDTH), D ** -0.5)
    out['x_wo'] = nrm((DEPTH, X_WIDTH, D), X_WIDTH ** -0.5)
    out['ff_w1'] = nrm((DEPTH, D, D_FF), D ** -0.5)
    out['ff_w2'] = nrm((DEPTH, D_FF, D), D_FF ** -0.5)
    return out


def reference(x_prompt, x_sample, cache_attn_k, cache_attn_v, cache_mem_k, cache_mem_v, state_conv, state_shift,
              state_wkv, page_table, mem_prompt, norm_g, mem_norm_g, rel_table, w_in_even, conv_w, conv_b,
              conv_ln_g, conv_ln_b, lambda_q1, lambda_k1, lambda_q2, lambda_k2, subln_g, w_out_even, rw_mix,
              rw_w0, rw_w1, rw_w2, rw_a0, rw_a1, rw_a2, rw_g1, rw_g2, rw_kk, rw_ka, rw_rk, rw_lnx_g, rw_lnx_b,
              rw_wr, rw_wk, rw_wv, rw_wo, x_wq, x_wk, x_wv, x_wo, ff_w1, ff_w2):
    p = dict(norm_g=norm_g, rel_table=rel_table, w_in_even=w_in_even, conv_w=conv_w, conv_b=conv_b,
             conv_ln_g=conv_ln_g, conv_ln_b=conv_ln_b, lambda_q1=lambda_q1, lambda_k1=lambda_k1,
             lambda_q2=lambda_q2, lambda_k2=lambda_k2, subln_g=subln_g, w_out_even=w_out_even,
             rw_mix=rw_mix, rw_w0=rw_w0, rw_w1=rw_w1, rw_w2=rw_w2, rw_a0=rw_a0, rw_a1=rw_a1, rw_a2=rw_a2,
             rw_g1=rw_g1, rw_g2=rw_g2, rw_kk=rw_kk, rw_ka=rw_ka, rw_rk=rw_rk, rw_lnx_g=rw_lnx_g,
             rw_lnx_b=rw_lnx_b, rw_wr=rw_wr, rw_wk=rw_wk, rw_wv=rw_wv, rw_wo=rw_wo, x_wq=x_wq, x_wo=x_wo,
             ff_w1=ff_w1, ff_w2=ff_w2)
    bp = x_prompt.shape[0]
    mem_n = [rmsnorm(mem_prompt, mem_norm_g[l]) for l in range(DEPTH)]
    mem_k_prompt = jnp.stack([(mem_n[l] @ x_wk[l]).reshape(bp, -1, X_HEADS, X_HEAD_DIM) for l in range(DEPTH)])
    mem_v_prompt = jnp.stack([(mem_n[l] @ x_wv[l]).reshape(bp, -1, X_HEADS, X_HEAD_DIM) for l in range(DEPTH)])
    zc = jnp.zeros((N_EVEN, bp, CONV_W - 1, C_CONV), x_prompt.dtype)
    zs = jnp.zeros((N_ODD, bp, D_MODEL), x_prompt.dtype)
    zw = jnp.zeros((N_ODD, bp, RW_HEADS, RW_HEAD, RW_HEAD), jnp.float32)
    y_prompt, ak_p, av_p, cv_p, sh_p, wk_p = trunk(x_prompt, mem_k_prompt, mem_v_prompt, zc, zs, zw, None, p)
    y_sample, ak_s, av_s, cv_s, sh_s, wk_s = trunk(x_sample, cache_mem_k, cache_mem_v, state_conv, state_shift,
                                                   state_wkv, (cache_attn_k, cache_attn_v, page_table), p)
    return (y_prompt, y_sample, ak_p, av_p, ak_s, av_s, cv_p, cv_s, sh_p, sh_s, wk_p, wk_s, mem_k_prompt, mem_v_prompt)
```

```python
import functools
import math

import jax
import jax.numpy as jnp
from jax import lax
from jax.experimental import pallas as pl
from jax.experimental.pallas import tpu as pltpu

F32 = jnp.float32
BF16 = jnp.bfloat16

D_MODEL = 2048
PAGE_SIZE = 128
C_CONV = D_MODEL // 2
CONV_W = 31
DA_HEADS = 4
DA_HEAD_DIM = 128
DA_WIDTH = DA_HEADS * 2 * DA_HEAD_DIM
REL_BUCKETS = 32
REL_MAX_EXACT = REL_BUCKETS // 2
REL_MAX_DIST = 128
RW_HEAD = 64
RW_HEADS = D_MODEL // RW_HEAD
RW_GN_EPS = 64e-5
N_MEM = 256
X_HEADS = 4
X_HEAD_DIM = 128
X_WIDTH = X_HEADS * X_HEAD_DIM
D_FF = 4 * D_MODEL
NORM_EPS = 1e-6
LN_EPS = 1e-5

V7X_VMEM_BYTES = 64 * 1024 * 1024
V7X_VMEM_RESERVE = 6 * 1024 * 1024
LANES = 128
BF16_SUBLANES = 16
MASK_VALUE = -1e30


def _rel_bucket(n):
    if n < REL_MAX_EXACT:
        return n
    return min(REL_MAX_EXACT + int(math.log(n / REL_MAX_EXACT) / math.log(REL_MAX_DIST / REL_MAX_EXACT)
                                   * (REL_BUCKETS - REL_MAX_EXACT)), REL_BUCKETS - 1)


REL_THRESH = tuple(min(n for n in range(4 * REL_MAX_DIST) if _rel_bucket(n) >= m) for m in range(REL_BUCKETS))


def _cparams(n_grid, vmem_estimate):
    limit = min(max(int(vmem_estimate * 1.2), 32 * 1024 * 1024), V7X_VMEM_BYTES - V7X_VMEM_RESERVE)
    return pltpu.CompilerParams(dimension_semantics=("arbitrary",) * n_grid, vmem_limit_bytes=limit)


def _rms(x, g, eps=NORM_EPS):
    return x * lax.rsqrt(jnp.mean(x * x, axis=-1, keepdims=True) + eps) * g


def _mxu(x):
    return x.astype(BF16) if x.shape[0] % BF16_SUBLANES == 0 else x.astype(F32)


def _dot(a, b):
    return jnp.dot(a, b, preferred_element_type=F32)


def _dot_nt(a, b):
    return lax.dot_general(a, b, (((1,), (1,)), ((), ())), preferred_element_type=F32)


def _dot_tn(a, b):
    return lax.dot_general(a, b, (((0,), (0,)), ((), ())), preferred_element_type=F32)


def _even_in_kernel(x_ref, g_ref, w_ref, glu_ref, q_ref, k_ref, v_ref, u_ref, ga_ref):
    j = pl.program_id(1)

    @pl.when(j == 0)
    def _():
        u_ref[...] = _rms(x_ref[...], g_ref[...]).astype(BF16)

    z = _dot(u_ref[...], w_ref[...])

    @pl.when(j == 0)
    def _():
        ga_ref[...] = z

    @pl.when(j == 1)
    def _():
        glu_ref[...] = ga_ref[...] * jax.nn.sigmoid(z)

    @pl.when(j == 2)
    def _():
        q_ref[...] = z.astype(q_ref.dtype)

    @pl.when(j == 3)
    def _():
        k_ref[...] = z

    @pl.when(j == 4)
    def _():
        v_ref[...] = z


def even_in(x, g, w_in, *, tm, q_dtype):
    m, d = x.shape
    c = C_CONV
    assert w_in.shape == (d, 5 * c) and m % tm == 0
    est = 2 * tm * d * 4 + 2 * d * c * 2 + 2 * tm * c * (4 + 4 + 4 + 4) + tm * d * 2 + tm * c * 4 + 3 * tm * c * 4
    row = lambda i, j: (i, 0)
    return pl.pallas_call(
        _even_in_kernel,
        grid=(m // tm, 5),
        in_specs=[pl.BlockSpec((tm, d), row), pl.BlockSpec((1, d), lambda i, j: (0, 0)),
                  pl.BlockSpec((d, c), lambda i, j: (0, j))],
        out_specs=[pl.BlockSpec((tm, c), row)] * 4,
        out_shape=[jax.ShapeDtypeStruct((m, c), F32), jax.ShapeDtypeStruct((m, c), q_dtype),
                   jax.ShapeDtypeStruct((m, c), F32), jax.ShapeDtypeStruct((m, c), F32)],
        scratch_shapes=[pltpu.VMEM((tm, d), BF16), pltpu.VMEM((tm, c), F32)],
        compiler_params=_cparams(2, est),
        name="even_in",
    )(x, g.reshape(1, d), w_in)


def _proj_kernel(x_ref, g_ref, w_ref, o_ref, u_ref, *, norm):
    @pl.when(pl.program_id(1) == 0)
    def _():
        x = x_ref[...].astype(F32)
        if norm:
            x = _rms(x, g_ref[...])
        u_ref[...] = x.astype(u_ref.dtype)

    o_ref[...] = _dot(u_ref[...], w_ref[...]).astype(o_ref.dtype)


def proj(x, g, w, *, tm, tn, out_dtype=F32):
    m, kd = x.shape
    n = w.shape[1]
    assert m % tm == 0 and n % tn == 0
    norm = g is not None
    gg = (g if norm else jnp.ones((kd,), F32)).reshape(1, kd)
    est = 2 * tm * kd * x.dtype.itemsize + 2 * kd * tn * 2 + 2 * tm * tn * 4 + tm * kd * 2 + 2 * tm * tn * 4
    return pl.pallas_call(
        functools.partial(_proj_kernel, norm=norm),
        grid=(m // tm, n // tn),
        in_specs=[pl.BlockSpec((tm, kd), lambda i, j: (i, 0)), pl.BlockSpec((1, kd), lambda i, j: (0, 0)),
                  pl.BlockSpec((kd, tn), lambda i, j: (0, j))],
        out_specs=pl.BlockSpec((tm, tn), lambda i, j: (i, j)),
        out_shape=jax.ShapeDtypeStruct((m, n), out_dtype),
        scratch_shapes=[pltpu.VMEM((tm, kd), BF16)],
        compiler_params=_cparams(2, est),
        name="proj",
    )(x, gg, w)


def _out_resnorm_kernel(*refs, n_in):
    xs = refs[:n_in]
    ws = refs[n_in:2 * n_in]
    h_ref, g_ref, o_ref = refs[2 * n_in:]
    y = _dot(xs[0][...].astype(BF16), ws[0][...])
    for x_ref, w_ref in zip(xs[1:], ws[1:]):
        y = y + _dot(x_ref[...].astype(BF16), w_ref[...])
    o_ref[...] = h_ref[...] + _rms(y, g_ref[...])


def out_resnorm(xs, ws, h, g, *, tm):
    m, d = h.shape
    assert m % tm == 0
    n_in = len(xs)
    est = 4 * tm * d * 4 + tm * d * 4
    for x, w in zip(xs, ws):
        est += 2 * tm * x.shape[1] * x.dtype.itemsize + 2 * w.shape[0] * d * 2
    in_specs = [pl.BlockSpec((tm, x.shape[1]), lambda i: (i, 0)) for x in xs]
    in_specs += [pl.BlockSpec(w.shape, lambda i: (0, 0)) for w in ws]
    in_specs += [pl.BlockSpec((tm, d), lambda i: (i, 0)), pl.BlockSpec((1, d), lambda i: (0, 0))]
    return pl.pallas_call(
        functools.partial(_out_resnorm_kernel, n_in=n_in),
        grid=(m // tm,),
        in_specs=in_specs,
        out_specs=pl.BlockSpec((tm, d), lambda i: (i, 0)),
        out_shape=jax.ShapeDtypeStruct((m, d), F32),
        compiler_params=_cparams(1, est),
        name="out_resnorm",
    )(*xs, *ws, h, g.reshape(1, d))


def _mlp_kernel(h_ref, gpre_ref, w1_ref, w2_ref, gpost_ref, o_ref, u_ref, acc_ref):
    f = pl.program_id(1)

    @pl.when(f == 0)
    def _():
        u_ref[...] = _rms(h_ref[...], gpre_ref[...]).astype(BF16)
        acc_ref[...] = jnp.zeros_like(acc_ref)

    t = jnp.maximum(_dot(u_ref[...], w1_ref[...]), 0.0)
    acc_ref[...] += _dot((t * t).astype(BF16), w2_ref[...])

    @pl.when(f == pl.num_programs(1) - 1)
    def _():
        o_ref[...] = h_ref[...] + _rms(acc_ref[...], gpost_ref[...])


def mlp(h, gpre, w1, w2, gpost, *, tm, tf):
    m, d = h.shape
    ff = w1.shape[1]
    assert m % tm == 0 and ff % tf == 0
    est = 4 * tm * d * 4 + 4 * d * tf * 2 + tm * d * (2 + 4) + 3 * tm * tf * 4 + tm * d * 4
    return pl.pallas_call(
        _mlp_kernel,
        grid=(m // tm, ff // tf),
        in_specs=[pl.BlockSpec((tm, d), lambda i, f: (i, 0)), pl.BlockSpec((1, d), lambda i, f: (0, 0)),
                  pl.BlockSpec((d, tf), lambda i, f: (0, f)), pl.BlockSpec((tf, d), lambda i, f: (f, 0)),
                  pl.BlockSpec((1, d), lambda i, f: (0, 0))],
        out_specs=pl.BlockSpec((tm, d), lambda i, f: (i, 0)),
        out_shape=jax.ShapeDtypeStruct((m, d), F32),
        scratch_shapes=[pltpu.VMEM((tm, d), BF16), pltpu.VMEM((tm, d), F32)],
        compiler_params=_cparams(2, est),
        name="mlp",
    )(h, gpre.reshape(1, d), w1, w2, gpost.reshape(1, d))


CONV_PAD = 32
CONV_ROWS = 32


def _conv_kernel(glu_ref, prev_ref, w_ref, b_ref, lg_ref, lb_ref, out_ref, st_ref, buf_ref, *, tt):
    i = pl.program_id(1)
    lead = CONV_PAD - (CONV_W - 1)
    c = glu_ref.shape[-1]

    @pl.when(i == 0)
    def _():
        buf_ref[0:lead, :] = jnp.zeros((lead, c), F32)
        buf_ref[lead:CONV_PAD, :] = prev_ref[0]

    @pl.when(i > 0)
    def _():
        buf_ref[0:CONV_PAD, :] = buf_ref[tt:tt + CONV_PAD, :]

    buf_ref[CONV_PAD:CONV_PAD + tt, :] = glu_ref[0]

    rows = min(CONV_ROWS, tt)
    for r0 in range(0, tt, rows):
        acc = jnp.zeros((rows, c), F32)
        for tap in range(CONV_W):
            acc = acc + buf_ref[r0 + lead + tap:r0 + lead + tap + rows, :] * w_ref[tap:tap + 1, :]
        y = acc + b_ref[...]
        mu = jnp.mean(y, axis=-1, keepdims=True)
        yc = y - mu
        var = jnp.mean(yc * yc, axis=-1, keepdims=True)
        yn = yc * lax.rsqrt(var + LN_EPS) * lg_ref[...] + lb_ref[...]
        out_ref[0, r0:r0 + rows, :] = (yn * jax.nn.sigmoid(yn)).astype(out_ref.dtype)

    @pl.when(i == pl.num_programs(1) - 1)
    def _():
        st_ref[0] = buf_ref[tt + lead:tt + CONV_PAD, :]


def conv_branch(glu, conv_prev, conv_w, conv_b, ln_g, ln_b, *, tt, out_dtype):
    b, t, c = glu.shape
    assert t % tt == 0 and (tt >= CONV_PAD or t == tt)
    est = 4 * tt * c * 4 + 4 * (CONV_W - 1) * c * 4 + (tt + CONV_PAD) * c * 4 + 2 * CONV_W * c * 4
    vec = pl.BlockSpec((1, c), lambda bi, i: (0, 0))
    return pl.pallas_call(
        functools.partial(_conv_kernel, tt=tt),
        grid=(b, t // tt),
        in_specs=[pl.BlockSpec((1, tt, c), lambda bi, i: (bi, i, 0)),
                  pl.BlockSpec((1, CONV_W - 1, c), lambda bi, i: (bi, 0, 0)),
                  pl.BlockSpec((CONV_W, c), lambda bi, i: (0, 0)), vec, vec, vec],
        out_specs=[pl.BlockSpec((1, tt, c), lambda bi, i: (bi, i, 0)),
                   pl.BlockSpec((1, CONV_W - 1, c), lambda bi, i: (bi, 0, 0))],
        out_shape=[jax.ShapeDtypeStruct((b, t, c), out_dtype), jax.ShapeDtypeStruct((b, CONV_W - 1, c), F32)],
        scratch_shapes=[pltpu.VMEM((tt + CONV_PAD, c), F32)],
        compiler_params=_cparams(2, est),
        name="conv_branch",
    )(glu, conv_prev, conv_w, conv_b.reshape(1, c), ln_g.reshape(1, c), ln_b.reshape(1, c))


def _lambda_kernel(q1_ref, k1_ref, q2_ref, k2_ref, o_ref, *, lam_init):
    s1 = jnp.sum(q1_ref[...] * k1_ref[...], axis=-1, keepdims=True)
    s2 = jnp.sum(q2_ref[...] * k2_ref[...], axis=-1, keepdims=True)
    o_ref[...] = jnp.broadcast_to(jnp.exp(s1) - jnp.exp(s2) + lam_init, o_ref.shape)


def diff_lambda(lq1, lk1, lq2, lk2, lam_init):
    d = lq1.shape[0]
    out = pl.pallas_call(
        functools.partial(_lambda_kernel, lam_init=lam_init),
        out_shape=jax.ShapeDtypeStruct((1, LANES), F32),
        name="diff_lambda",
    )(lq1.reshape(1, d), lk1.reshape(1, d), lq2.reshape(1, d), lk2.reshape(1, d))
    return out[0, :1]


def _bias_kernel(tab_ref, o_ref, *, tq, tk, dstep):
    di = pl.program_id(0)
    h = pl.program_id(1)
    n = (lax.broadcasted_iota(jnp.int32, (tq, tk), 0) - lax.broadcasted_iota(jnp.int32, (tq, tk), 1) + di * dstep)
    bias = jnp.full((tq, tk), tab_ref[(REL_BUCKETS - 1) * DA_HEADS + h], F32)
    for m in range(REL_BUCKETS - 2, -1, -1):
        bias = jnp.where(n < REL_THRESH[m + 1], tab_ref[m * DA_HEADS + h], bias)
    o_ref[0, 0] = jnp.where(n < 0, MASK_VALUE, bias)


def bias_tiles(rel_table, *, tq, tk, nd, dstep):
    return pl.pallas_call(
        functools.partial(_bias_kernel, tq=tq, tk=tk, dstep=dstep),
        grid=(nd, DA_HEADS),
        in_specs=[pl.BlockSpec(memory_space=pltpu.SMEM)],
        out_specs=pl.BlockSpec((1, 1, tq, tk), lambda di, h: (di, h, 0, 0)),
        out_shape=jax.ShapeDtypeStruct((nd, DA_HEADS, tq, tk), F32),
        compiler_params=_cparams(2, 8 * tq * tk * 4),
        name="bias_tiles",
    )(rel_table.reshape(-1))


def _softmax_step(s, v, m_ref, l_ref, acc_ref, idx):
    m_prev = m_ref[idx]
    m_new = jnp.maximum(m_prev, jnp.max(s, axis=-1, keepdims=True))
    alpha = jnp.exp(m_prev - m_new)
    p = jnp.exp(s - m_new)
    l_ref[idx] = alpha * l_ref[idx] + jnp.sum(p, axis=-1, keepdims=True)
    acc_ref[idx] = alpha * acc_ref[idx] + _dot(p.astype(v.dtype), v)
    m_ref[idx] = m_new


def _diff_finish(acc_ref, l_ref, base, lam, sg, lam_init):
    o = acc_ref[base] / l_ref[base] - lam * (acc_ref[base + 1] / l_ref[base + 1])
    return _rms(o, sg) * (1.0 - lam_init)


def _attn_prompt_kernel(lam_ref, far_ref, q_ref, k_ref, v_ref, bias_ref, sg_ref, o_ref, m_ref, l_ref, acc_ref,
                        *, lam_init):
    h = pl.program_id(1)
    i = pl.program_id(2)
    j = pl.program_id(3)
    dh = DA_HEAD_DIM

    @pl.when(j == 0)
    def _():
        m_ref[...] = jnp.full(m_ref.shape, MASK_VALUE, F32)
        l_ref[...] = jnp.zeros_like(l_ref)
        acc_ref[...] = jnp.zeros_like(acc_ref)

    @pl.when(j <= i)
    def _():
        q = q_ref[0]
        kb = k_ref[0].astype(BF16)
        vb = v_ref[0].astype(BF16)
        bias = jnp.where(j >= i - 1, bias_ref[0, 0], far_ref[h])
        for c in range(2):
            s = _dot_nt(q[:, c * dh:(c + 1) * dh], kb[:, c * dh:(c + 1) * dh]) * (dh ** -0.5) + bias
            _softmax_step(s, vb, m_ref, l_ref, acc_ref, c)

    @pl.when(j == i)
    def _():
        o_ref[0] = _diff_finish(acc_ref, l_ref, 0, lam_ref[0], sg_ref[...], lam_init).astype(o_ref.dtype)


def attn_prompt(q, k, v, bias, far, lam, subln_g, *, tq, lam_init):
    b, t, _ = q.shape
    hw = 2 * DA_HEAD_DIM
    nq = t // tq
    est = 2 * tq * hw * (2 + 4 + 4 + 2) + 2 * tq * tq * 4 + 2 * tq * hw * 4 + 8 * tq * tq * 4
    smem = pl.BlockSpec(memory_space=pltpu.SMEM)
    return pl.pallas_call(
        functools.partial(_attn_prompt_kernel, lam_init=lam_init),
        grid=(b, DA_HEADS, nq, nq),
        in_specs=[smem, smem,
                  pl.BlockSpec((1, tq, hw), lambda bi, h, i, j: (bi, i, h)),
                  pl.BlockSpec((1, tq, hw), lambda bi, h, i, j: (bi, jnp.minimum(i, j), h)),
                  pl.BlockSpec((1, tq, hw), lambda bi, h, i, j: (bi, jnp.minimum(i, j), h)),
                  pl.BlockSpec((1, 1, tq, tq), lambda bi, h, i, j: (jnp.where(j >= i, 0, 1), h, 0, 0)),
                  pl.BlockSpec((1, hw), lambda bi, h, i, j: (0, 0))],
        out_specs=pl.BlockSpec((1, tq, hw), lambda bi, h, i, j: (bi, i, h)),
        out_shape=jax.ShapeDtypeStruct((b, t, DA_WIDTH), BF16),
        scratch_shapes=[pltpu.VMEM((2, tq, 1), F32), pltpu.VMEM((2, tq, 1), F32), pltpu.VMEM((2, tq, hw), F32)],
        compiler_params=_cparams(4, est),
        name="attn_prompt",
    )(lam, far, q, k, v, bias, subln_g.reshape(1, hw))


PAGES_PER_STEP = 4


def _attn_sample_kernel(pt_ref, lam_ref, far_ref, q_ref, kn_ref, vn_ref, bias_ref, sg_ref, *rest, lam_init):
    pages = rest[:2 * PAGES_PER_STEP]
    o_ref, m_ref, l_ref, acc_ref = rest[2 * PAGES_PER_STEP:]
    s_idx = pl.program_id(1)
    last = pl.num_programs(1) - 1
    dh = DA_HEAD_DIM
    hw = 2 * dh
    t = q_ref.shape[1]

    @pl.when(s_idx == 0)
    def _():
        m_ref[...] = jnp.full(m_ref.shape, MASK_VALUE, F32)
        l_ref[...] = jnp.zeros_like(l_ref)
        acc_ref[...] = jnp.zeros_like(acc_ref)

    q = q_ref[0]

    def scores(h, c, keys, bias):
        lo = h * hw + c * dh
        return _dot_nt(q[:, lo:lo + dh], keys[:, lo:lo + dh]) * (dh ** -0.5) + bias

    for pp in range(PAGES_PER_STEP):
        k_pg = pages[2 * pp][0]
        v_pg = pages[2 * pp + 1][0]
        for h in range(DA_HEADS):
            if pp == PAGES_PER_STEP - 1:
                bias = jnp.where(s_idx == last, bias_ref[1, h], far_ref[h])
            else:
                bias = far_ref[h]
            for c in range(2):
                _softmax_step(scores(h, c, k_pg, bias), v_pg[:, h * hw:(h + 1) * hw], m_ref, l_ref, acc_ref, 2 * h + c)

    @pl.when(s_idx == last)
    def _():
        kn = kn_ref[0]
        vn = vn_ref[0]
        for h in range(DA_HEADS):
            bias = bias_ref[0, h][:, :t]
            for c in range(2):
                _softmax_step(scores(h, c, kn, bias), vn[:, h * hw:(h + 1) * hw], m_ref, l_ref, acc_ref, 2 * h + c)
            o_ref[0, :, h * hw:(h + 1) * hw] = _diff_finish(
                acc_ref, l_ref, 2 * h, lam_ref[0], sg_ref[...], lam_init).astype(o_ref.dtype)


def attn_sample(q, k_new, v_new, k_pool, v_pool, page_table, bias, far, lam, subln_g, *, lam_init):
    b, t, w = q.shape
    n_pages = page_table.shape[1]
    assert n_pages % PAGES_PER_STEP == 0 and k_pool.shape[1:] == (PAGE_SIZE, w)
    hw = 2 * DA_HEAD_DIM
    smem = pl.BlockSpec(memory_space=pltpu.SMEM)
    row = pl.BlockSpec((1, t, w), lambda bi, s, pt: (bi, 0, 0))

    def page_spec(pp):
        return pl.BlockSpec((1, PAGE_SIZE, w), lambda bi, s, pt: (pt[bi, s * PAGES_PER_STEP + pp], 0, 0))

    in_specs = [smem, smem, row, row, row,
                pl.BlockSpec(bias.shape, lambda bi, s, pt: (0, 0, 0, 0)),
                pl.BlockSpec((1, hw), lambda bi, s, pt: (0, 0))]
    pools = []
    for pp in range(PAGES_PER_STEP):
        in_specs += [page_spec(pp), page_spec(pp)]
        pools += [k_pool, v_pool]
    est = 4 * PAGES_PER_STEP * PAGE_SIZE * w * 4 + 16 * t * w * 4 + 4 * bias.size * 4
    return pl.pallas_call(
        functools.partial(_attn_sample_kernel, lam_init=lam_init),
        grid_spec=pltpu.PrefetchScalarGridSpec(
            num_scalar_prefetch=1,
            grid=(b, n_pages // PAGES_PER_STEP),
            in_specs=in_specs,
            out_specs=pl.BlockSpec((1, t, w), lambda bi, s, pt: (bi, 0, 0)),
            scratch_shapes=[pltpu.VMEM((2 * DA_HEADS, t, 1), F32), pltpu.VMEM((2 * DA_HEADS, t, 1), F32),
                            pltpu.VMEM((2 * DA_HEADS, t, hw), F32)]),
        out_shape=jax.ShapeDtypeStruct((b, t, w), F32),
        compiler_params=_cparams(2, est),
        name="attn_sample",
    )(page_table, lam, far, q, k_new, v_new, bias, subln_g.reshape(1, hw), *pools)


def _xattn_kernel(q_ref, mk_ref, mv_ref, o_ref):
    q = _mxu(q_ref[0])
    mk = mk_ref[0].astype(q.dtype)
    mv = mv_ref[0].astype(q.dtype)
    dh = X_HEAD_DIM
    for h in range(X_HEADS):
        sl = slice(h * dh, (h + 1) * dh)
        s = _dot_nt(q[:, sl], mk[:, sl]) * (dh ** -0.5)
        p = jnp.exp(s - jnp.max(s, axis=-1, keepdims=True))
        p = p / jnp.sum(p, axis=-1, keepdims=True)
        o_ref[0, :, sl] = _dot(p.astype(q.dtype), mv[:, sl]).astype(o_ref.dtype)


def xattn_core(q, mk, mv, *, tt):
    b, t, w = q.shape
    n_mem = mk.shape[1]
    est = 4 * tt * w * 4 + 4 * n_mem * w * 4 + 6 * tt * n_mem * 4
    return pl.pallas_call(
        _xattn_kernel,
        grid=(b, t // tt),
        in_specs=[pl.BlockSpec((1, tt, w), lambda bi, i: (bi, i, 0)),
                  pl.BlockSpec((1, n_mem, w), lambda bi, i: (bi, 0, 0)),
                  pl.BlockSpec((1, n_mem, w), lambda bi, i: (bi, 0, 0))],
        out_specs=pl.BlockSpec((1, tt, w), lambda bi, i: (bi, i, 0)),
        out_shape=jax.ShapeDtypeStruct((b, t, w), F32),
        compiler_params=_cparams(2, est),
        name="xattn_core",
    )(q, mk, mv)


def _rwkv_mix_kernel(h_ref, g_ref, sp_ref, mix_ref, xs_ref, sh_ref, carry_ref):
    i = pl.program_id(1)

    @pl.when(i == 0)
    def _():
        carry_ref[...] = sp_ref[0]

    u = _rms(h_ref[0], g_ref[...])
    tt = u.shape[0]
    row = lax.broadcasted_iota(jnp.int32, u.shape, 0)
    prev = jnp.where(row == 0, carry_ref[...], pltpu.roll(u, 1, axis=0))
    xx = prev - u
    for n in range(6):
        xs_ref[n, 0] = (u + xx * mix_ref[n:n + 1, :]).astype(xs_ref.dtype)
    carry_ref[...] = u[tt - 1:tt, :]
    sh_ref[0] = u[tt - 1:tt, :]


def rwkv_mix(h, g, shift_prev, mix, *, tt, out_dtype):
    b, t, d = h.shape
    est = 2 * tt * d * 4 + 12 * tt * d * 4 + 6 * tt * d * 4
    return pl.pallas_call(
        _rwkv_mix_kernel,
        grid=(b, t // tt),
        in_specs=[pl.BlockSpec((1, tt, d), lambda bi, i: (bi, i, 0)), pl.BlockSpec((1, d), lambda bi, i: (0, 0)),
                  pl.BlockSpec((1, 1, d), lambda bi, i: (bi, 0, 0)), pl.BlockSpec((6, d), lambda bi, i: (0, 0))],
        out_specs=[pl.BlockSpec((6, 1, tt, d), lambda bi, i: (0, bi, i, 0)),
                   pl.BlockSpec((1, 1, d), lambda bi, i: (bi, 0, 0))],
        out_shape=[jax.ShapeDtypeStruct((6, b, t, d), out_dtype), jax.ShapeDtypeStruct((b, 1, d), F32)],
        scratch_shapes=[pltpu.VMEM((1, d), F32)],
        compiler_params=_cparams(2, est),
        name="rwkv_mix",
    )(h, g.reshape(1, d), shift_prev.reshape(b, 1, d), mix)


def _lora_kernel(x_ref, w1_ref, w2_ref, b_ref, o_ref, *, act):
    t = _dot(x_ref[...].astype(BF16), w1_ref[...])
    if act == "tanh":
        t = jnp.tanh(t)
    elif act == "sigmoid":
        t = jax.nn.sigmoid(t)
    o_ref[...] = _dot(t.astype(BF16), w2_ref[...]) + b_ref[...]


def lora(x, w1, w2, bias, *, act, tm):
    m, d = x.shape
    r = w1.shape[1]
    n = w2.shape[1]
    est = 2 * tm * d * x.dtype.itemsize + 4 * d * r * 2 + 4 * r * n * 2 + 4 * tm * n * 4
    return pl.pallas_call(
        functools.partial(_lora_kernel, act=act),
        grid=(m // tm,),
        in_specs=[pl.BlockSpec((tm, d), lambda i: (i, 0)), pl.BlockSpec((d, r), lambda i: (0, 0)),
                  pl.BlockSpec((r, n), lambda i: (0, 0)), pl.BlockSpec((1, n), lambda i: (0, 0))],
        out_specs=pl.BlockSpec((tm, n), lambda i: (i, 0)),
        out_shape=jax.ShapeDtypeStruct((m, n), F32),
        compiler_params=_cparams(1, est),
        name="lora",
    )(x, w1, w2, bias.reshape(1, n))


WKV_GROUP = 4


def _cumsum_rows(x):
    n = x.shape[0]
    row = lax.broadcasted_iota(jnp.int32, x.shape, 0)
    step = 1
    while step < n:
        x = x + jnp.where(row >= step, pltpu.roll(x, step, axis=0), 0.0)
        step *= 2
    return x


def _wkv_chunk(r, kr, v, wp, ap, gate, s0_list, kk_p, ka_p, rk_p, lg, lb):
    ln = r.shape[0]
    n = RW_HEAD
    a = jax.nn.sigmoid(ap)
    lw = -jnp.exp(-jax.nn.softplus(-wp) - 0.5)
    cum = _cumsum_rows(lw)
    cl = cum[ln - 1:ln, :]
    w_in = jnp.exp(cum)
    w_ex = jnp.exp(cum - lw)
    w_inv = jnp.exp(-cum)
    w_rem = jnp.exp(cl - cum)
    w_end = jnp.exp(cl)
    kkv = kr * kk_p
    k = kr * (1.0 + (a - 1.0) * ka_p)

    ri = lax.broadcasted_iota(jnp.int32, (ln, 2 * ln), 0)
    ci = lax.broadcasted_iota(jnp.int32, (ln, 2 * ln), 1)
    ci = jnp.where(ci < ln, ci, ci - ln)
    mask_a = ci < ri
    mask_r = ci <= ri
    nlev = max(1, (ln - 1).bit_length())

    ys = []
    s_new = []
    for h in range(WKV_GROUP):
        sl = slice(h * n, (h + 1) * n)
        s0 = s0_list[h]
        kkh = kkv[:, sl]
        kkn = kkh / jnp.maximum(jnp.sqrt(jnp.sum(kkh * kkh, axis=-1, keepdims=True)), 1e-12)
        bt = kkn * a[:, sl]
        kh = k[:, sl]
        vh = v[:, sl]
        rh = r[:, sl]
        a_til = -kkn * w_ex[:, sl]
        r_til = rh * w_in[:, sl]
        b_til = bt * w_inv[:, sl]
        k_til = kh * w_inv[:, sl]
        amat = _dot_nt(jnp.concatenate([a_til, r_til], axis=0), jnp.concatenate([b_til, k_til], axis=0))
        a_rows = jnp.where(mask_a, amat[:ln], 0.0)
        r_rows = amat[ln:]
        x = _dot_nt(a_til, s0) + _dot(a_rows, jnp.concatenate([jnp.zeros_like(vh), vh], axis=0))
        pw = a_rows[:, :ln]
        for lvl in range(nlev):
            x = x + _dot(pw, x)
            if lvl < nlev - 1:
                pw = _dot(pw, pw)
        uv = jnp.concatenate([x, vh], axis=0)
        y = _dot_nt(r_til, s0) + _dot(jnp.where(mask_r, r_rows, 0.0), uv)
        s_new.append(s0 * w_end[:, sl]
                     + _dot_tn(uv, jnp.concatenate([bt * w_rem[:, sl], kh * w_rem[:, sl]], axis=0)))
        mu = jnp.mean(y, axis=-1, keepdims=True)
        yc = y - mu
        var = jnp.mean(yc * yc, axis=-1, keepdims=True)
        bonus = jnp.sum(rh * kh * rk_p[:, sl], axis=-1, keepdims=True) * vh
        ys.append(yc * lax.rsqrt(var + RW_GN_EPS) * lg[:, sl] + lb[:, sl] + bonus)
    return jnp.concatenate(ys, axis=-1) * gate, s_new


def _wkv_kernel(r_ref, k_ref, v_ref, w_ref, a_ref, g_ref, s0_ref, kk_ref, ka_ref, rk_ref, lg_ref, lb_ref,
                y_ref, sT_ref, s_scr, *, ln):
    tb = pl.program_id(2)
    bb, tblk, _ = r_ref.shape

    @pl.when(tb == 0)
    def _():
        s_scr[...] = s0_ref[...]

    params = (kk_ref[...], ka_ref[...], rk_ref[...], lg_ref[...], lb_ref[...])

    def one_batch(bi, carry):
        def one_chunk(c, carry2):
            rows = pl.ds(pl.multiple_of(c * ln, ln), ln)
            s0_list = [s_scr[bi, h] for h in range(WKV_GROUP)]
            y, s_new = _wkv_chunk(r_ref[bi, rows, :], k_ref[bi, rows, :], v_ref[bi, rows, :], w_ref[bi, rows, :],
                                  a_ref[bi, rows, :], g_ref[bi, rows, :], s0_list, *params)
            y_ref[bi, rows, :] = y.astype(y_ref.dtype)
            for h in range(WKV_GROUP):
                s_scr[bi, h] = s_new[h]
            return carry2

        return lax.fori_loop(0, tblk // ln, one_chunk, carry)

    lax.fori_loop(0, bb, one_batch, 0)

    @pl.when(tb == pl.num_programs(2) - 1)
    def _():
        sT_ref[...] = s_scr[...]


def wkv(r, k, v, wpre, apre, gate, s0, kk, ka, rk, lnx_g, lnx_b, *, bb, tblk, ln, out_dtype):
    b, t, d = r.shape
    gl = WKV_GROUP * RW_HEAD
    assert b % bb == 0 and t % tblk == 0 and tblk % ln == 0 and d % gl == 0
    seq = pl.BlockSpec((bb, tblk, gl), lambda bi, g, i: (bi, i, g))
    st = pl.BlockSpec((bb, WKV_GROUP, RW_HEAD, RW_HEAD), lambda bi, g, i: (bi, g, 0, 0))
    vec = pl.BlockSpec((1, gl), lambda bi, g, i: (0, g))
    est = 14 * bb * tblk * gl * 4 + 5 * bb * gl * RW_HEAD * 4 + 64 * ln * gl * 4
    return pl.pallas_call(
        functools.partial(_wkv_kernel, ln=ln),
        grid=(b // bb, d // gl, t // tblk),
        in_specs=[seq] * 6 + [st] + [vec] * 5,
        out_specs=[seq, st],
        out_shape=[jax.ShapeDtypeStruct((b, t, d), out_dtype), jax.ShapeDtypeStruct(s0.shape, F32)],
        scratch_shapes=[pltpu.VMEM((bb, WKV_GROUP, RW_HEAD, RW_HEAD), F32)],
        compiler_params=_cparams(3, est),
        name="wkv",
    )(r, k, v, wpre, apre, gate, s0, kk.reshape(1, d), ka.reshape(1, d), rk.reshape(1, d),
      lnx_g.reshape(1, d), lnx_b.reshape(1, d))


def _tile(m, pref):
    return pref if m % pref == 0 else m


def _pad_rank(w1, w2):
    r = w1.shape[1]
    rp = -(-r // LANES) * LANES
    return jnp.pad(w1, ((0, 0), (0, rp - r))), jnp.pad(w2, ((0, rp - r), (0, 0)))


def _trunk(x, mem_k, mem_v, conv_prev, shift_prev, wkv_prev, paged, p, bias, far):
    b, t, d = x.shape
    m = b * t
    tm = _tile(m, 512)
    tseq = _tile(t, 512)
    h = x.reshape(m, d)
    depth = p['norm_g'].shape[0]
    ks, vs, convs, shifts, wkvs = [], [], [], [], []
    for li in range(depth):
        g = p['norm_g'][li]
        if li % 2 == 0:
            ei = li // 2
            lam_init = 0.8 - 0.6 * math.exp(-0.3 * li)
            q_dtype = BF16 if paged is None else F32
            glu, q, k, v = even_in(h, g[0], p['w_in_even'][ei], tm=tm, q_dtype=q_dtype)
            conv_out, conv_state = conv_branch(
                glu.reshape(b, t, C_CONV), conv_prev[ei], p['conv_w'][ei], p['conv_b'][ei], p['conv_ln_g'][ei],
                p['conv_ln_b'][ei], tt=_tile(t, 128), out_dtype=BF16 if t % BF16_SUBLANES == 0 else F32)
            lam = diff_lambda(p['lambda_q1'][ei], p['lambda_k1'][ei], p['lambda_q2'][ei], p['lambda_k2'][ei], lam_init)
            q3, k3, v3 = (z.reshape(b, t, DA_WIDTH) for z in (q, k, v))
            if paged is None:
                o = attn_prompt(q3, k3, v3, bias, far, lam, p['subln_g'][ei], tq=tseq, lam_init=lam_init)
            else:
                k_pool, v_pool, page_table, n_pool = paged
                o = attn_sample(q3, k3, v3, k_pool, v_pool, page_table + ei * n_pool, bias, far, lam,
                                p['subln_g'][ei], lam_init=lam_init)
            w_out = p['w_out_even'][ei]
            h = out_resnorm([conv_out.reshape(m, C_CONV), o.reshape(m, DA_WIDTH)], [w_out[:C_CONV], w_out[C_CONV:]],
                            h, g[1], tm=tm)
            ks.append(k.reshape(b, t, DA_HEADS, 2 * DA_HEAD_DIM))
            vs.append(v.reshape(b, t, DA_HEADS, 2 * DA_HEAD_DIM))
            convs.append(conv_state)
        else:
            oi = li // 2
            prompt = t >= 64
            xs, sh = rwkv_mix(h.reshape(b, t, d), g[0], shift_prev[oi], p['rw_mix'][oi], tt=_tile(t, 256),
                              out_dtype=BF16 if prompt else F32)
            xs = xs.reshape(6, m, d)
            r = proj(xs[0], None, p['rw_wr'][oi], tm=tm, tn=d)
            k = proj(xs[2], None, p['rw_wk'][oi], tm=tm, tn=d)
            v = proj(xs[3], None, p['rw_wv'][oi], tm=tm, tn=d)
            wpre = lora(xs[1], *p['rw_w12'][oi], p['rw_w0'][oi], act="tanh", tm=tm)
            apre = lora(xs[4], *p['rw_a12'][oi], p['rw_a0'][oi], act=None, tm=tm)
            gate = lora(xs[5], *p['rw_g12'][oi], jnp.zeros((d,), F32), act="sigmoid", tm=tm)
            seqs = [z.reshape(b, t, d) for z in (r, k, v, wpre, apre, gate)]
            if prompt:
                yact, s_fin = wkv(*seqs, wkv_prev[oi], p['rw_kk'][oi], p['rw_ka'][oi], p['rw_rk'][oi].reshape(d),
                                  p['rw_lnx_g'][oi], p['rw_lnx_b'][oi], bb=1, tblk=tseq, ln=64, out_dtype=BF16)
            else:
                yact, s_fin = wkv(*seqs, wkv_prev[oi], p['rw_kk'][oi], p['rw_ka'][oi], p['rw_rk'][oi].reshape(d),
                                  p['rw_lnx_g'][oi], p['rw_lnx_b'][oi], bb=8, tblk=t, ln=t, out_dtype=F32)
            h = out_resnorm([yact.reshape(m, d)], [p['rw_wo'][oi]], h, g[1], tm=tm)
            shifts.append(sh.reshape(b, d))
            wkvs.append(s_fin)
        qx = proj(h, g[2], p['x_wq'][li], tm=tm, tn=X_WIDTH)
        ox = xattn_core(qx.reshape(b, t, X_WIDTH), mem_k[li], mem_v[li], tt=tseq)
        h = out_resnorm([ox.reshape(m, X_WIDTH)], [p['x_wo'][li]], h, g[3], tm=tm)
        h = mlp(h, g[4], p['ff_w1'][li], p['ff_w2'][li], g[5], tm=tm, tf=512)
    return (h.reshape(b, t, d), jnp.stack(ks), jnp.stack(vs), jnp.stack(convs), jnp.stack(shifts), jnp.stack(wkvs))


def kernel(x_prompt, x_sample, cache_attn_k, cache_attn_v, cache_mem_k, cache_mem_v, state_conv, state_shift,
           state_wkv, page_table, mem_prompt, norm_g, mem_norm_g, rel_table, w_in_even, conv_w, conv_b,
           conv_ln_g, conv_ln_b, lambda_q1, lambda_k1, lambda_q2, lambda_k2, subln_g, w_out_even, rw_mix,
           rw_w0, rw_w1, rw_w2, rw_a0, rw_a1, rw_a2, rw_g1, rw_g2, rw_kk, rw_ka, rw_rk, rw_lnx_g, rw_lnx_b,
           rw_wr, rw_wk, rw_wv, rw_wo, x_wq, x_wk, x_wv, x_wo, ff_w1, ff_w2):
    bf = lambda w: w.astype(BF16)
    n_odd = rw_w1.shape[0]
    lora_pairs = lambda w1, w2: [tuple(bf(z) for z in _pad_rank(w1[o], w2[o])) for o in range(n_odd)]
    p = dict(norm_g=norm_g, w_in_even=bf(w_in_even), conv_w=conv_w, conv_b=conv_b, conv_ln_g=conv_ln_g,
             conv_ln_b=conv_ln_b, lambda_q1=lambda_q1, lambda_k1=lambda_k1, lambda_q2=lambda_q2,
             lambda_k2=lambda_k2, subln_g=subln_g, w_out_even=bf(w_out_even), rw_mix=rw_mix, rw_w0=rw_w0,
             rw_w12=lora_pairs(rw_w1, rw_w2), rw_a0=rw_a0, rw_a12=lora_pairs(rw_a1, rw_a2),
             rw_g12=lora_pairs(rw_g1, rw_g2), rw_kk=rw_kk, rw_ka=rw_ka, rw_rk=rw_rk, rw_lnx_g=rw_lnx_g,
             rw_lnx_b=rw_lnx_b, rw_wr=bf(rw_wr), rw_wk=bf(rw_wk), rw_wv=bf(rw_wv), rw_wo=bf(rw_wo), x_wq=bf(x_wq),
             x_wo=bf(x_wo), ff_w1=bf(ff_w1), ff_w2=bf(ff_w2))
    depth = norm_g.shape[0]
    bp, n_mem, d = mem_prompt.shape
    tp = x_prompt.shape[1]
    bd, td, _ = x_sample.shape
    n_even = w_in_even.shape[0]

    mem2d = mem_prompt.reshape(bp * n_mem, d)
    tmem = _tile(bp * n_mem, 512)
    mem_k_prompt = jnp.stack([proj(mem2d, mem_norm_g[l], bf(x_wk[l]), tm=tmem, tn=X_WIDTH) for l in range(depth)])
    mem_v_prompt = jnp.stack([proj(mem2d, mem_norm_g[l], bf(x_wv[l]), tm=tmem, tn=X_WIDTH) for l in range(depth)])
    mem_k_prompt = mem_k_prompt.reshape(depth, bp, n_mem, X_WIDTH)
    mem_v_prompt = mem_v_prompt.reshape(depth, bp, n_mem, X_WIDTH)

    far = rel_table[REL_BUCKETS - 1]
    tq = _tile(tp, 512)
    bias_p = bias_tiles(rel_table, tq=tq, tk=tq, nd=2, dstep=tq)
    bias_s = bias_tiles(rel_table, tq=td, tk=PAGE_SIZE, nd=2, dstep=PAGE_SIZE)

    zc = jnp.zeros((n_even, bp, CONV_W - 1, C_CONV), F32)
    zs = jnp.zeros((n_odd, bp, d), F32)
    zw = jnp.zeros((n_odd, bp, RW_HEADS, RW_HEAD, RW_HEAD), F32)
    y_p, ak_p, av_p, cv_p, sh_p, wk_p = _trunk(x_prompt, mem_k_prompt, mem_v_prompt, zc, zs, zw, None, p, bias_p, far)

    n_pool = cache_attn_k.shape[1]
    pools = (cache_attn_k.reshape(n_even * n_pool, PAGE_SIZE, DA_WIDTH),
             cache_attn_v.reshape(n_even * n_pool, PAGE_SIZE, DA_WIDTH), page_table, n_pool)
    y_s, ak_s, av_s, cv_s, sh_s, wk_s = _trunk(
        x_sample, cache_mem_k.reshape(depth, bd, n_mem, X_WIDTH), cache_mem_v.reshape(depth, bd, n_mem, X_WIDTH),
        state_conv, state_shift, state_wkv, pools, p, bias_s, far)

    shape_m = (depth, bp, n_mem, X_HEADS, X_HEAD_DIM)
    return (y_p, y_s, ak_p, av_p, ak_s, av_s, cv_p, cv_s, sh_p, sh_s, wk_p, wk_s,
            mem_k_prompt.reshape(shape_m), mem_v_prompt.reshape(shape_m))
```

```python
import functools
import math

import jax
import jax.numpy as jnp
from jax import lax
from jax.experimental import pallas as pl
from jax.experimental.pallas import tpu as pltpu

F32 = jnp.float32
BF16 = jnp.bfloat16

D_MODEL = 2048
PAGE_SIZE = 128
C_CONV = D_MODEL // 2
CONV_W = 31
DA_HEADS = 4
DA_HEAD_DIM = 128
DA_WIDTH = DA_HEADS * 2 * DA_HEAD_DIM
REL_BUCKETS = 32
REL_MAX_EXACT = REL_BUCKETS // 2
REL_MAX_DIST = 128
RW_HEAD = 64
RW_HEADS = D_MODEL // RW_HEAD
RW_GN_EPS = 64e-5
N_MEM = 256
X_HEADS = 4
X_HEAD_DIM = 128
X_WIDTH = X_HEADS * X_HEAD_DIM
D_FF = 4 * D_MODEL
NORM_EPS = 1e-6
LN_EPS = 1e-5

V7X_VMEM_BYTES = 64 * 1024 * 1024
V7X_VMEM_RESERVE = 6 * 1024 * 1024
LANES = 128
BF16_SUBLANES = 16
MASK_VALUE = -1e30


def _rel_bucket(n):
    if n < REL_MAX_EXACT:
        return n
    return min(REL_MAX_EXACT + int(math.log(n / REL_MAX_EXACT) / math.log(REL_MAX_DIST / REL_MAX_EXACT)
                                   * (REL_BUCKETS - REL_MAX_EXACT)), REL_BUCKETS - 1)


REL_THRESH = tuple(min(n for n in range(4 * REL_MAX_DIST) if _rel_bucket(n) >= m) for m in range(REL_BUCKETS))


def _cparams(n_grid, vmem_estimate):
    limit = min(max(int(vmem_estimate * 1.2), 32 * 1024 * 1024), V7X_VMEM_BYTES - V7X_VMEM_RESERVE)
    return pltpu.CompilerParams(dimension_semantics=("arbitrary",) * n_grid, vmem_limit_bytes=limit)


def _rms(x, g, eps=NORM_EPS):
    return x * lax.rsqrt(jnp.mean(x * x, axis=-1, keepdims=True) + eps) * g


def _mxu(x):
    return x.astype(BF16) if x.shape[0] % BF16_SUBLANES == 0 else x.astype(F32)


def _dot(a, b):
    return jnp.dot(a, b, preferred_element_type=F32)


def _dot_nt(a, b):
    return lax.dot_general(a, b, (((1,), (1,)), ((), ())), preferred_element_type=F32)


def _dot_tn(a, b):
    return lax.dot_general(a, b, (((0,), (0,)), ((), ())), preferred_element_type=F32)


def _even_in_kernel(x_ref, g_ref, w_ref, glu_ref, q_ref, k_ref, v_ref, u_ref, ga_ref):
    j = pl.program_id(1)

    @pl.when(j == 0)
    def _():
        u_ref[...] = _rms(x_ref[...], g_ref[...]).astype(BF16)

    z = _dot(u_ref[...], w_ref[...])

    @pl.when(j == 0)
    def _():
        ga_ref[...] = z

    @pl.when(j == 1)
    def _():
        glu_ref[...] = ga_ref[...] * jax.nn.sigmoid(z)

    @pl.when(j == 2)
    def _():
        q_ref[...] = z.astype(q_ref.dtype)

    @pl.when(j == 3)
    def _():
        k_ref[...] = z

    @pl.when(j == 4)
    def _():
        v_ref[...] = z


def even_in(x, g, w_in, *, tm, q_dtype):
    m, d = x.shape
    c = C_CONV
    assert w_in.shape == (d, 5 * c) and m % tm == 0
    est = 2 * tm * d * 4 + 2 * d * c * 2 + 2 * tm * c * (4 + 4 + 4 + 4) + tm * d * 2 + tm * c * 4 + 3 * tm * c * 4
    row = lambda i, j: (i, 0)
    return pl.pallas_call(
        _even_in_kernel,
        grid=(m // tm, 5),
        in_specs=[pl.BlockSpec((tm, d), row), pl.BlockSpec((1, d), lambda i, j: (0, 0)),
                  pl.BlockSpec((d, c), lambda i, j: (0, j))],
        out_specs=[pl.BlockSpec((tm, c), row)] * 4,
        out_shape=[jax.ShapeDtypeStruct((m, c), F32), jax.ShapeDtypeStruct((m, c), q_dtype),
                   jax.ShapeDtypeStruct((m, c), F32), jax.ShapeDtypeStruct((m, c), F32)],
        scratch_shapes=[pltpu.VMEM((tm, d), BF16), pltpu.VMEM((tm, c), F32)],
        compiler_params=_cparams(2, est),
        name="even_in",
    )(x, g.reshape(1, d), w_in)


def _proj_kernel(x_ref, g_ref, w_ref, o_ref, u_ref, *, norm):
    @pl.when(pl.program_id(1) == 0)
    def _():
        x = x_ref[...].astype(F32)
        if norm:
            x = _rms(x, g_ref[...])
        u_ref[...] = x.astype(u_ref.dtype)

    o_ref[...] = _dot(u_ref[...], w_ref[...]).astype(o_ref.dtype)


def proj(x, g, w, *, tm, tn, out_dtype=F32):
    m, kd = x.shape
    n = w.shape[1]
    assert m % tm == 0 and n % tn == 0
    norm = g is not None
    gg = (g if norm else jnp.ones((kd,), F32)).reshape(1, kd)
    est = 2 * tm * kd * x.dtype.itemsize + 2 * kd * tn * 2 + 2 * tm * tn * 4 + tm * kd * 2 + 2 * tm * tn * 4
    return pl.pallas_call(
        functools.partial(_proj_kernel, norm=norm),
        grid=(m // tm, n // tn),
        in_specs=[pl.BlockSpec((tm, kd), lambda i, j: (i, 0)), pl.BlockSpec((1, kd), lambda i, j: (0, 0)),
                  pl.BlockSpec((kd, tn), lambda i, j: (0, j))],
        out_specs=pl.BlockSpec((tm, tn), lambda i, j: (i, j)),
        out_shape=jax.ShapeDtypeStruct((m, n), out_dtype),
        scratch_shapes=[pltpu.VMEM((tm, kd), BF16)],
        compiler_params=_cparams(2, est),
        name="proj",
    )(x, gg, w)


def _out_resnorm_kernel(*refs, n_in):
    xs = refs[:n_in]
    ws = refs[n_in:2 * n_in]
    h_ref, g_ref, o_ref = refs[2 * n_in:]
    y = _dot(xs[0][...].astype(BF16), ws[0][...])
    for x_ref, w_ref in zip(xs[1:], ws[1:]):
        y = y + _dot(x_ref[...].astype(BF16), w_ref[...])
    o_ref[...] = h_ref[...] + _rms(y, g_ref[...])


def out_resnorm(xs, ws, h, g, *, tm):
    m, d = h.shape
    assert m % tm == 0
    n_in = len(xs)
    est = 4 * tm * d * 4 + tm * d * 4
    for x, w in zip(xs, ws):
        est += 2 * tm * x.shape[1] * x.dtype.itemsize + 2 * w.shape[0] * d * 2
    in_specs = [pl.BlockSpec((tm, x.shape[1]), lambda i: (i, 0)) for x in xs]
    in_specs += [pl.BlockSpec(w.shape, lambda i: (0, 0)) for w in ws]
    in_specs += [pl.BlockSpec((tm, d), lambda i: (i, 0)), pl.BlockSpec((1, d), lambda i: (0, 0))]
    return pl.pallas_call(
        functools.partial(_out_resnorm_kernel, n_in=n_in),
        grid=(m // tm,),
        in_specs=in_specs,
        out_specs=pl.BlockSpec((tm, d), lambda i: (i, 0)),
        out_shape=jax.ShapeDtypeStruct((m, d), F32),
        compiler_params=_cparams(1, est),
        name="out_resnorm",
    )(*xs, *ws, h, g.reshape(1, d))


def _mlp_kernel(h_ref, gpre_ref, w1_ref, w2_ref, gpost_ref, o_ref, u_ref, acc_ref):
    f = pl.program_id(1)

    @pl.when(f == 0)
    def _():
        u_ref[...] = _rms(h_ref[...], gpre_ref[...]).astype(BF16)
        acc_ref[...] = jnp.zeros_like(acc_ref)

    t = jnp.maximum(_dot(u_ref[...], w1_ref[...]), 0.0)
    acc_ref[...] += _dot((t * t).astype(BF16), w2_ref[...])

    @pl.when(f == pl.num_programs(1) - 1)
    def _():
        o_ref[...] = h_ref[...] + _rms(acc_ref[...], gpost_ref[...])


def mlp(h, gpre, w1, w2, gpost, *, tm, tf):
    m, d = h.shape
    ff = w1.shape[1]
    assert m % tm == 0 and ff % tf == 0
    est = 4 * tm * d * 4 + 4 * d * tf * 2 + tm * d * (2 + 4) + 3 * tm * tf * 4 + tm * d * 4
    return pl.pallas_call(
        _mlp_kernel,
        grid=(m // tm, ff // tf),
        in_specs=[pl.BlockSpec((tm, d), lambda i, f: (i, 0)), pl.BlockSpec((1, d), lambda i, f: (0, 0)),
                  pl.BlockSpec((d, tf), lambda i, f: (0, f)), pl.BlockSpec((tf, d), lambda i, f: (f, 0)),
                  pl.BlockSpec((1, d), lambda i, f: (0, 0))],
        out_specs=pl.BlockSpec((tm, d), lambda i, f: (i, 0)),
        out_shape=jax.ShapeDtypeStruct((m, d), F32),
        scratch_shapes=[pltpu.VMEM((tm, d), BF16), pltpu.VMEM((tm, d), F32)],
        compiler_params=_cparams(2, est),
        name="mlp",
    )(h, gpre.reshape(1, d), w1, w2, gpost.reshape(1, d))


CONV_PAD = 32
CONV_ROWS = 32


def _conv_kernel(glu_ref, prev_ref, w_ref, b_ref, lg_ref, lb_ref, out_ref, st_ref, buf_ref, *, tt):
    i = pl.program_id(1)
    lead = CONV_PAD - (CONV_W - 1)
    c = glu_ref.shape[-1]

    @pl.when(i == 0)
    def _():
        buf_ref[0:lead, :] = jnp.zeros((lead, c), F32)
        buf_ref[lead:CONV_PAD, :] = prev_ref[0]

    @pl.when(i > 0)
    def _():
        buf_ref[0:CONV_PAD, :] = buf_ref[tt:tt + CONV_PAD, :]

    buf_ref[CONV_PAD:CONV_PAD + tt, :] = glu_ref[0]

    rows = min(CONV_ROWS, tt)
    for r0 in range(0, tt, rows):
        acc = jnp.zeros((rows, c), F32)
        for tap in range(CONV_W):
            acc = acc + buf_ref[r0 + lead + tap:r0 + lead + tap + rows, :] * w_ref[tap:tap + 1, :]
        y = acc + b_ref[...]
        mu = jnp.mean(y, axis=-1, keepdims=True)
        yc = y - mu
        var = jnp.mean(yc * yc, axis=-1, keepdims=True)
        yn = yc * lax.rsqrt(var + LN_EPS) * lg_ref[...] + lb_ref[...]
        out_ref[0, r0:r0 + rows, :] = (yn * jax.nn.sigmoid(yn)).astype(out_ref.dtype)

    @pl.when(i == pl.num_programs(1) - 1)
    def _():
        st_ref[0] = buf_ref[tt + lead:tt + CONV_PAD, :]


def conv_branch(glu, conv_prev, conv_w, conv_b, ln_g, ln_b, *, tt, out_dtype):
    b, t, c = glu.shape
    assert t % tt == 0 and (tt >= CONV_PAD or t == tt)
    est = 4 * tt * c * 4 + 4 * (CONV_W - 1) * c * 4 + (tt + CONV_PAD) * c * 4 + 2 * CONV_W * c * 4
    vec = pl.BlockSpec((1, c), lambda bi, i: (0, 0))
    return pl.pallas_call(
        functools.partial(_conv_kernel, tt=tt),
        grid=(b, t // tt),
        in_specs=[pl.BlockSpec((1, tt, c), lambda bi, i: (bi, i, 0)),
                  pl.BlockSpec((1, CONV_W - 1, c), lambda bi, i: (bi, 0, 0)),
                  pl.BlockSpec((CONV_W, c), lambda bi, i: (0, 0)), vec, vec, vec],
        out_specs=[pl.BlockSpec((1, tt, c), lambda bi, i: (bi, i, 0)),
                   pl.BlockSpec((1, CONV_W - 1, c), lambda bi, i: (bi, 0, 0))],
        out_shape=[jax.ShapeDtypeStruct((b, t, c), out_dtype), jax.ShapeDtypeStruct((b, CONV_W - 1, c), F32)],
        scratch_shapes=[pltpu.VMEM((tt + CONV_PAD, c), F32)],
        compiler_params=_cparams(2, est),
        name="conv_branch",
    )(glu, conv_prev, conv_w, conv_b.reshape(1, c), ln_g.reshape(1, c), ln_b.reshape(1, c))


def _lambda_kernel(q1_ref, k1_ref, q2_ref, k2_ref, o_ref, *, lam_init):
    s1 = jnp.sum(q1_ref[...] * k1_ref[...], axis=-1, keepdims=True)
    s2 = jnp.sum(q2_ref[...] * k2_ref[...], axis=-1, keepdims=True)
    o_ref[...] = jnp.broadcast_to(jnp.exp(s1) - jnp.exp(s2) + lam_init, o_ref.shape)


def diff_lambda(lq1, lk1, lq2, lk2, lam_init):
    d = lq1.shape[0]
    out = pl.pallas_call(
        functools.partial(_lambda_kernel, lam_init=lam_init),
        out_shape=jax.ShapeDtypeStruct((1, LANES), F32),
        name="diff_lambda",
    )(lq1.reshape(1, d), lk1.reshape(1, d), lq2.reshape(1, d), lk2.reshape(1, d))
    return out[0, :1]


def _bias_kernel(tab_ref, o_ref, *, tq, tk, dstep):
    di = pl.program_id(0)
    h = pl.program_id(1)
    n = (lax.broadcasted_iota(jnp.int32, (tq, tk), 0) - lax.broadcasted_iota(jnp.int32, (tq, tk), 1) + di * dstep)
    bias = jnp.full((tq, tk), tab_ref[(REL_BUCKETS - 1) * DA_HEADS + h], F32)
    for m in range(REL_BUCKETS - 2, -1, -1):
        bias = jnp.where(n < REL_THRESH[m + 1], tab_ref[m * DA_HEADS + h], bias)
    o_ref[0, 0] = jnp.where(n < 0, MASK_VALUE, bias)


def bias_tiles(rel_table, *, tq, tk, nd, dstep):
    return pl.pallas_call(
        functools.partial(_bias_kernel, tq=tq, tk=tk, dstep=dstep),
        grid=(nd, DA_HEADS),
        in_specs=[pl.BlockSpec(memory_space=pltpu.SMEM)],
        out_specs=pl.BlockSpec((1, 1, tq, tk), lambda di, h: (di, h, 0, 0)),
        out_shape=jax.ShapeDtypeStruct((nd, DA_HEADS, tq, tk), F32),
        compiler_params=_cparams(2, 8 * tq * tk * 4),
        name="bias_tiles",
    )(rel_table.reshape(-1))


def _softmax_step(s, v, m_ref, l_ref, acc_ref, idx):
    m_prev = m_ref[idx]
    m_new = jnp.maximum(m_prev, jnp.max(s, axis=-1, keepdims=True))
    alpha = jnp.exp(m_prev - m_new)
    p = jnp.exp(s - m_new)
    l_ref[idx] = alpha * l_ref[idx] + jnp.sum(p, axis=-1, keepdims=True)
    acc_ref[idx] = alpha * acc_ref[idx] + _dot(p.astype(v.dtype), v)
    m_ref[idx] = m_new


def _diff_finish(acc_ref, l_ref, base, lam, sg, lam_init):
    o = acc_ref[base] / l_ref[base] - lam * (acc_ref[base + 1] / l_ref[base + 1])
    return _rms(o, sg) * (1.0 - lam_init)


def _attn_prompt_kernel(lam_ref, far_ref, q_ref, k_ref, v_ref, bias_ref, sg_ref, o_ref, m_ref, l_ref, acc_ref,
                        *, lam_init):
    h = pl.program_id(1)
    i = pl.program_id(2)
    j = pl.program_id(3)
    dh = DA_HEAD_DIM

    @pl.when(j == 0)
    def _():
        m_ref[...] = jnp.full(m_ref.shape, MASK_VALUE, F32)
        l_ref[...] = jnp.zeros_like(l_ref)
        acc_ref[...] = jnp.zeros_like(acc_ref)

    @pl.when(j <= i)
    def _():
        q = q_ref[0]
        kb = k_ref[0].astype(BF16)
        vb = v_ref[0].astype(BF16)
        bias = jnp.where(j >= i - 1, bias_ref[0, 0], far_ref[h])
        for c in range(2):
            s = _dot_nt(q[:, c * dh:(c + 1) * dh], kb[:, c * dh:(c + 1) * dh]) * (dh ** -0.5) + bias
            _softmax_step(s, vb, m_ref, l_ref, acc_ref, c)

    @pl.when(j == i)
    def _():
        o_ref[0] = _diff_finish(acc_ref, l_ref, 0, lam_ref[0], sg_ref[...], lam_init).astype(o_ref.dtype)


def attn_prompt(q, k, v, bias, far, lam, subln_g, *, tq, lam_init):
    b, t, _ = q.shape
    hw = 2 * DA_HEAD_DIM
    nq = t // tq
    est = 2 * tq * hw * (2 + 4 + 4 + 2) + 2 * tq * tq * 4 + 2 * tq * hw * 4 + 8 * tq * tq * 4
    smem = pl.BlockSpec(memory_space=pltpu.SMEM)
    return pl.pallas_call(
        functools.partial(_attn_prompt_kernel, lam_init=lam_init),
        grid=(b, DA_HEADS, nq, nq),
        in_specs=[smem, smem,
                  pl.BlockSpec((1, tq, hw), lambda bi, h, i, j: (bi, i, h)),
                  pl.BlockSpec((1, tq, hw), lambda bi, h, i, j: (bi, jnp.minimum(i, j), h)),
                  pl.BlockSpec((1, tq, hw), lambda bi, h, i, j: (bi, jnp.minimum(i, j), h)),
                  pl.BlockSpec((1, 1, tq, tq), lambda bi, h, i, j: (jnp.where(j >= i, 0, 1), h, 0, 0)),
                  pl.BlockSpec((1, hw), lambda bi, h, i, j: (0, 0))],
        out_specs=pl.BlockSpec((1, tq, hw), lambda bi, h, i, j: (bi, i, h)),
        out_shape=jax.ShapeDtypeStruct((b, t, DA_WIDTH), BF16),
        scratch_shapes=[pltpu.VMEM((2, tq, 1), F32), pltpu.VMEM((2, tq, 1), F32), pltpu.VMEM((2, tq, hw), F32)],
        compiler_params=_cparams(4, est),
        name="attn_prompt",
    )(lam, far, q, k, v, bias, subln_g.reshape(1, hw))


PAGES_PER_STEP = 4


PAGE_ROWS = PAGE_SIZE * 2 * DA_HEADS


def _page_rows(pool):
    pg = pool.shape[0]
    x = pool.reshape(pg, PAGE_SIZE, DA_HEADS, 2, DA_HEAD_DIM)
    return jnp.transpose(x, (0, 1, 3, 2, 4)).reshape(pg, PAGE_ROWS, DA_HEAD_DIM)


def _stack_heads(x, half=None):
    dh = DA_HEAD_DIM
    parts = []
    for c in range(2):
        for h in range(DA_HEADS):
            lo = h * 2 * dh + (c if half is None else half) * dh
            parts.append(x[:, lo:lo + dh])
    return jnp.concatenate(parts, axis=0)


def _attn_sample_kernel(pt_ref, lam_ref, q_ref, kn_ref, vn_ref, ptile_ref, ntile_ref, sg_ref, *rest, lam_init):
    pages = rest[:2 * PAGES_PER_STEP]
    o_ref, m_ref, l_ref, acc_ref = rest[2 * PAGES_PER_STEP:]
    s_idx = pl.program_id(1)
    last = pl.num_programs(1) - 1
    dh = DA_HEAD_DIM
    t = q_ref.shape[1]
    nq = DA_HEADS * t
    scale = dh ** -0.5

    @pl.when(s_idx == 0)
    def _():
        m_ref[...] = jnp.full(m_ref.shape, MASK_VALUE, F32)
        l_ref[...] = jnp.zeros_like(l_ref)
        acc_ref[...] = jnp.zeros_like(acc_ref)

    qm = _stack_heads(q_ref[0])

    def update(s, weighted_values):
        m_prev = m_ref[...]
        m_new = jnp.maximum(m_prev, jnp.max(s, axis=-1, keepdims=True))
        alpha = jnp.exp(m_prev - m_new)
        p = jnp.exp(s - m_new)
        l_ref[...] = alpha * l_ref[...] + jnp.sum(p, axis=-1, keepdims=True)
        lo, hi = weighted_values(p)
        acc_ref[:, :dh] = alpha * acc_ref[:, :dh] + lo
        acc_ref[:, dh:] = alpha * acc_ref[:, dh:] + hi
        m_ref[...] = m_new

    scores = []
    for pp in range(PAGES_PER_STEP):
        tile = ptile_ref[0]
        if pp == PAGES_PER_STEP - 1:
            tile = jnp.where(s_idx == last, ptile_ref[1], tile)
        scores.append(_dot_nt(qm, pages[2 * pp][0]) * scale + tile)

    def page_values(p):
        width = p.shape[1]
        top, bot = p[:nq], p[nq:]
        lhs = jnp.concatenate([top, pltpu.roll(bot, width - DA_HEADS, axis=1),
                               pltpu.roll(top, DA_HEADS, axis=1), bot], axis=0)
        res = _dot(lhs[:, :PAGE_ROWS], pages[1][0])
        for pp in range(1, PAGES_PER_STEP):
            res = res + _dot(lhs[:, pp * PAGE_ROWS:(pp + 1) * PAGE_ROWS], pages[2 * pp + 1][0])
        return res[:2 * nq], res[2 * nq:]

    update(jnp.concatenate(scores, axis=1), page_values)

    @pl.when(s_idx == last)
    def _():
        vn = vn_ref[0]
        update(_dot_nt(qm, _stack_heads(kn_ref[0])) * scale + ntile_ref[...],
               lambda p: (_dot(p, _stack_heads(vn, 0)), _dot(p, _stack_heads(vn, 1))))
        o = acc_ref[...] / l_ref[...]
        o = _rms(o[:nq] - lam_ref[0] * o[nq:], sg_ref[...]) * (1.0 - lam_init)
        for h in range(DA_HEADS):
            o_ref[0, :, h * 2 * dh:(h + 1) * 2 * dh] = o[h * t:(h + 1) * t].astype(o_ref.dtype)


def _sample_tiles(bias, far):
    _, nh, t, tk = bias.shape
    c_row = jnp.arange(2).reshape(2, 1, 1, 1, 1, 1)
    h_row = jnp.arange(nh).reshape(1, nh, 1, 1, 1, 1)
    own_page = (c_row == jnp.arange(2).reshape(1, 1, 1, 1, 2, 1)) & (h_row == jnp.arange(nh).reshape(1, 1, 1, 1, 1, nh))
    base = jnp.stack([jnp.broadcast_to(far.reshape(nh, 1, 1), (nh, t, tk)), bias[1]])
    ptile = jnp.where(own_page[None], base[:, None, :, :, :, None, None], MASK_VALUE).reshape(2, 2 * nh * t, tk * 2 * nh)
    own_new = (c_row == jnp.arange(2).reshape(1, 1, 1, 2, 1, 1)) & (h_row == jnp.arange(nh).reshape(1, 1, 1, 1, nh, 1))
    ntile = jnp.where(own_new, bias[0][None, :, :, None, None, :t], MASK_VALUE).reshape(2 * nh * t, 2 * nh * t)
    return ptile, ntile


def attn_sample(q, k_new, v_new, k_rows, v_rows, page_table, bias, far, lam, subln_g, *, lam_init):
    b, t, w = q.shape
    n_pages = page_table.shape[1]
    assert n_pages % PAGES_PER_STEP == 0 and k_rows.shape[1:] == (PAGE_ROWS, DA_HEAD_DIM)
    hw = 2 * DA_HEAD_DIM
    ptile, ntile = _sample_tiles(bias, far)
    smem = pl.BlockSpec(memory_space=pltpu.SMEM)
    row = pl.BlockSpec((1, t, w), lambda bi, s, pt: (bi, 0, 0))

    def page_spec(pp):
        return pl.BlockSpec((1, PAGE_ROWS, DA_HEAD_DIM), lambda bi, s, pt: (pt[bi, s * PAGES_PER_STEP + pp], 0, 0))

    in_specs = [smem, row, row, row,
                pl.BlockSpec(ptile.shape, lambda bi, s, pt: (0, 0, 0)),
                pl.BlockSpec(ntile.shape, lambda bi, s, pt: (0, 0)),
                pl.BlockSpec((1, hw), lambda bi, s, pt: (0, 0))]
    pools = []
    for pp in range(PAGES_PER_STEP):
        in_specs += [page_spec(pp), page_spec(pp)]
        pools += [k_rows, v_rows]
    nr = 2 * DA_HEADS * t
    est = (4 * PAGES_PER_STEP * PAGE_ROWS * DA_HEAD_DIM * 4 + 16 * t * w * 4 + 4 * ptile.size * 4
           + 12 * nr * PAGES_PER_STEP * PAGE_ROWS * 4)
    return pl.pallas_call(
        functools.partial(_attn_sample_kernel, lam_init=lam_init),
        grid_spec=pltpu.PrefetchScalarGridSpec(
            num_scalar_prefetch=1,
            grid=(b, n_pages // PAGES_PER_STEP),
            in_specs=in_specs,
            out_specs=pl.BlockSpec((1, t, w), lambda bi, s, pt: (bi, 0, 0)),
            scratch_shapes=[pltpu.VMEM((nr, 1), F32), pltpu.VMEM((nr, 1), F32), pltpu.VMEM((nr, hw), F32)]),
        out_shape=jax.ShapeDtypeStruct((b, t, w), F32),
        compiler_params=_cparams(2, est),
        name="attn_sample",
    )(page_table, lam, q, k_new, v_new, ptile, ntile, subln_g.reshape(1, hw), *pools)


def _xattn_kernel(q_ref, mk_ref, mv_ref, o_ref):
    q = _mxu(q_ref[0])
    mk = mk_ref[0].astype(q.dtype)
    mv = mv_ref[0].astype(q.dtype)
    dh = X_HEAD_DIM
    for h in range(X_HEADS):
        sl = slice(h * dh, (h + 1) * dh)
        s = _dot_nt(q[:, sl], mk[:, sl]) * (dh ** -0.5)
        p = jnp.exp(s - jnp.max(s, axis=-1, keepdims=True))
        p = p / jnp.sum(p, axis=-1, keepdims=True)
        o_ref[0, :, sl] = _dot(p.astype(q.dtype), mv[:, sl]).astype(o_ref.dtype)


def xattn_core(q, mk, mv, *, tt):
    b, t, w = q.shape
    n_mem = mk.shape[1]
    est = 4 * tt * w * 4 + 4 * n_mem * w * 4 + 6 * tt * n_mem * 4
    return pl.pallas_call(
        _xattn_kernel,
        grid=(b, t // tt),
        in_specs=[pl.BlockSpec((1, tt, w), lambda bi, i: (bi, i, 0)),
                  pl.BlockSpec((1, n_mem, w), lambda bi, i: (bi, 0, 0)),
                  pl.BlockSpec((1, n_mem, w), lambda bi, i: (bi, 0, 0))],
        out_specs=pl.BlockSpec((1, tt, w), lambda bi, i: (bi, i, 0)),
        out_shape=jax.ShapeDtypeStruct((b, t, w), F32),
        compiler_params=_cparams(2, est),
        name="xattn_core",
    )(q, mk, mv)


def _rwkv_mix_kernel(h_ref, g_ref, sp_ref, mix_ref, xs_ref, sh_ref, carry_ref):
    i = pl.program_id(1)

    @pl.when(i == 0)
    def _():
        carry_ref[...] = sp_ref[0]

    u = _rms(h_ref[0], g_ref[...])
    tt = u.shape[0]
    row = lax.broadcasted_iota(jnp.int32, u.shape, 0)
    prev = jnp.where(row == 0, carry_ref[...], pltpu.roll(u, 1, axis=0))
    xx = prev - u
    for n in range(6):
        xs_ref[n, 0] = (u + xx * mix_ref[n:n + 1, :]).astype(xs_ref.dtype)
    carry_ref[...] = u[tt - 1:tt, :]
    sh_ref[0] = u[tt - 1:tt, :]


def rwkv_mix(h, g, shift_prev, mix, *, tt, out_dtype):
    b, t, d = h.shape
    est = 2 * tt * d * 4 + 12 * tt * d * 4 + 6 * tt * d * 4
    return pl.pallas_call(
        _rwkv_mix_kernel,
        grid=(b, t // tt),
        in_specs=[pl.BlockSpec((1, tt, d), lambda bi, i: (bi, i, 0)), pl.BlockSpec((1, d), lambda bi, i: (0, 0)),
                  pl.BlockSpec((1, 1, d), lambda bi, i: (bi, 0, 0)), pl.BlockSpec((6, d), lambda bi, i: (0, 0))],
        out_specs=[pl.BlockSpec((6, 1, tt, d), lambda bi, i: (0, bi, i, 0)),
                   pl.BlockSpec((1, 1, d), lambda bi, i: (bi, 0, 0))],
        out_shape=[jax.ShapeDtypeStruct((6, b, t, d), out_dtype), jax.ShapeDtypeStruct((b, 1, d), F32)],
        scratch_shapes=[pltpu.VMEM((1, d), F32)],
        compiler_params=_cparams(2, est),
        name="rwkv_mix",
    )(h, g.reshape(1, d), shift_prev.reshape(b, 1, d), mix)


def _lora_kernel(x_ref, w1_ref, w2_ref, b_ref, o_ref, *, act):
    t = _dot(x_ref[...].astype(BF16), w1_ref[...])
    if act == "tanh":
        t = jnp.tanh(t)
    elif act == "sigmoid":
        t = jax.nn.sigmoid(t)
    o_ref[...] = _dot(t.astype(BF16), w2_ref[...]) + b_ref[...]


def lora(x, w1, w2, bias, *, act, tm):
    m, d = x.shape
    r = w1.shape[1]
    n = w2.shape[1]
    est = 2 * tm * d * x.dtype.itemsize + 4 * d * r * 2 + 4 * r * n * 2 + 4 * tm * n * 4
    return pl.pallas_call(
        functools.partial(_lora_kernel, act=act),
        grid=(m // tm,),
        in_specs=[pl.BlockSpec((tm, d), lambda i: (i, 0)), pl.BlockSpec((d, r), lambda i: (0, 0)),
                  pl.BlockSpec((r, n), lambda i: (0, 0)), pl.BlockSpec((1, n), lambda i: (0, 0))],
        out_specs=pl.BlockSpec((tm, n), lambda i: (i, 0)),
        out_shape=jax.ShapeDtypeStruct((m, n), F32),
        compiler_params=_cparams(1, est),
        name="lora",
    )(x, w1, w2, bias.reshape(1, n))


WKV_GROUP = 4


def _cumsum_rows(x):
    n = x.shape[0]
    row = lax.broadcasted_iota(jnp.int32, x.shape, 0)
    step = 1
    while step < n:
        x = x + jnp.where(row >= step, pltpu.roll(x, step, axis=0), 0.0)
        step *= 2
    return x


def _wkv_solve_chunks(chunks, kk_p, ka_p, rk_p):
    ln = chunks[0][0].shape[0]
    n = RW_HEAD
    ri = lax.broadcasted_iota(jnp.int32, (ln, 2 * ln), 0)
    ci = lax.broadcasted_iota(jnp.int32, (ln, 2 * ln), 1)
    ci = jnp.where(ci < ln, ci, ci - ln)
    mask_a = ci < ri
    mask_r = ci <= ri
    nlev = max(1, (ln - 1).bit_length())
    zeros = jnp.zeros((ln, n), F32)

    pairs = []
    w_ends = []
    bonuses = []
    for r, kr, v, wp, ap in chunks:
        a = jax.nn.sigmoid(ap)
        lw = -jnp.exp(-jax.nn.softplus(-wp) - 0.5)
        cum = _cumsum_rows(lw)
        cl = cum[ln - 1:ln, :]
        w_in = jnp.exp(cum)
        w_ex = jnp.exp(cum - lw)
        w_inv = jnp.exp(-cum)
        w_rem = jnp.exp(cl - cum)
        w_ends.append(jnp.exp(cl))
        kkv = kr * kk_p
        k = kr * (1.0 + (a - 1.0) * ka_p)
        bon = []
        for h in range(WKV_GROUP):
            sl = slice(h * n, (h + 1) * n)
            kkh = kkv[:, sl]
            kkn = kkh / jnp.maximum(jnp.sqrt(jnp.sum(kkh * kkh, axis=-1, keepdims=True)), 1e-12)
            bt = kkn * a[:, sl]
            kh = k[:, sl]
            vh = v[:, sl]
            rh = r[:, sl]
            bon.append(jnp.sum(rh * kh * rk_p[:, sl], axis=-1, keepdims=True) * vh)
            pairs.append(dict(a_til=-kkn * w_ex[:, sl], r_til=rh * w_in[:, sl], b_til=bt * w_inv[:, sl],
                              k_til=kh * w_inv[:, sl], bp=bt * w_rem[:, sl], kp=kh * w_rem[:, sl], v=vh))
        bonuses.append(jnp.concatenate(bon, axis=-1))

    for p in pairs:
        p['amat'] = _dot_nt(jnp.concatenate([p['a_til'], p['r_til']], axis=0),
                            jnp.concatenate([p['b_til'], p['k_til']], axis=0))
    for p in pairs:
        p['a_rows'] = jnp.where(mask_a, p['amat'][:ln], 0.0)
        p['r_rows'] = jnp.where(mask_r, p['amat'][ln:], 0.0)
        p['pw'] = p['a_rows'][:, :ln]
        p['akv'] = _dot(p['a_rows'], jnp.concatenate([zeros, p['v']], axis=0))
    for p in pairs:
        p['x'] = jnp.concatenate([p['a_til'], p['akv']], axis=1)
    for lvl in range(nlev):
        for p in pairs:
            p['x'] = p['x'] + _dot(p['pw'], p['x'])
        if lvl < nlev - 1:
            for p in pairs:
                p['pw'] = _dot(p['pw'], p['pw'])
    out = []
    for p in pairs:
        tc = _dot_tn(p['x'], p['bp'])
        vk = _dot_tn(p['v'], p['kp'])
        ry = _dot(p['r_rows'], jnp.concatenate([p['x'], jnp.concatenate([zeros, p['v']], axis=1)], axis=0))
        out.append((tc[:n], tc[n:] + vk, p['r_til'] + ry[:, :n], ry[:, n:]))
    return out, w_ends, bonuses


def _wkv_kernel(r_ref, k_ref, v_ref, w_ref, a_ref, g_ref, s0_ref, kk_ref, ka_ref, rk_ref, lg_ref, lb_ref,
                y_ref, sT_ref, s_scr, tm_scr, cm_scr, rq_scr, yc_scr, we_scr, bon_scr, *, ln, nper):
    tb = pl.program_id(2)
    bb, tblk, _ = r_ref.shape
    nck = tblk // ln
    nslot = bb * nck
    n = RW_HEAD

    @pl.when(tb == 0)
    def _():
        s_scr[...] = s0_ref[...]

    kk_p, ka_p, rk_p = kk_ref[...], ka_ref[...], rk_ref[...]

    def slot_index(slot):
        if nck == 1:
            return slot, 0
        return 0, slot

    def rows_of(c):
        return pl.ds(c * ln if isinstance(c, int) else pl.multiple_of(c * ln, ln), ln)

    def solve(it, carry):
        chunks = []
        for q in range(nper):
            bi, c = slot_index(it * nper + q)
            rows = rows_of(c)
            chunks.append(tuple(ref[bi, rows, :] for ref in (r_ref, k_ref, v_ref, w_ref, a_ref)))
        mats, w_ends, bonuses = _wkv_solve_chunks(chunks, kk_p, ka_p, rk_p)
        for q in range(nper):
            slot = it * nper + q
            we_scr[slot] = w_ends[q]
            bon_scr[slot] = bonuses[q]
            for h in range(WKV_GROUP):
                tmat, cmat, rq, yc = mats[q * WKV_GROUP + h]
                tm_scr[slot, h] = tmat
                cm_scr[slot, h] = cmat
                rq_scr[slot, h] = rq
                yc_scr[slot, h] = yc
        return carry

    lax.fori_loop(0, nslot // nper, solve, 0)

    lg, lb = lg_ref[...], lb_ref[...]

    def advance(c, states):
        new_states = []
        for bi in range(bb):
            slot = bi * nck + c
            w_end = we_scr[slot]
            ys = []
            for h in range(WKV_GROUP):
                s = states[bi * WKV_GROUP + h]
                sl = slice(h * n, (h + 1) * n)
                y = _dot_nt(rq_scr[slot, h], s) + yc_scr[slot, h]
                new_states.append(s * w_end[:, sl] + _dot(s, tm_scr[slot, h]) + cm_scr[slot, h])
                mu = jnp.mean(y, axis=-1, keepdims=True)
                yc = y - mu
                var = jnp.mean(yc * yc, axis=-1, keepdims=True)
                ys.append(yc * lax.rsqrt(var + RW_GN_EPS))
            rows = rows_of(c)
            yn = jnp.concatenate(ys, axis=-1) * lg + lb + bon_scr[slot]
            y_ref[bi, rows, :] = (yn * g_ref[bi, rows, :]).astype(y_ref.dtype)
        return tuple(new_states)

    states = tuple(s_scr[bi, h] for bi in range(bb) for h in range(WKV_GROUP))
    if nck == 1:
        states = advance(0, states)
    else:
        states = lax.fori_loop(0, nck, advance, states)
    for bi in range(bb):
        for h in range(WKV_GROUP):
            s_scr[bi, h] = states[bi * WKV_GROUP + h]

    @pl.when(tb == pl.num_programs(2) - 1)
    def _():
        sT_ref[...] = s_scr[...]


def wkv(r, k, v, wpre, apre, gate, s0, kk, ka, rk, lnx_g, lnx_b, *, bb, tblk, ln, nper, out_dtype):
    b, t, d = r.shape
    gl = WKV_GROUP * RW_HEAD
    nslot = bb * (tblk // ln)
    assert b % bb == 0 and t % tblk == 0 and tblk % ln == 0 and d % gl == 0 and nslot % nper == 0
    assert bb == 1 or tblk == ln
    seq = pl.BlockSpec((bb, tblk, gl), lambda bi, g, i: (bi, i, g))
    st = pl.BlockSpec((bb, WKV_GROUP, RW_HEAD, RW_HEAD), lambda bi, g, i: (bi, g, 0, 0))
    vec = pl.BlockSpec((1, gl), lambda bi, g, i: (0, g))
    sq = pltpu.VMEM((nslot, WKV_GROUP, RW_HEAD, RW_HEAD), F32)
    tall = pltpu.VMEM((nslot, WKV_GROUP, ln, RW_HEAD), F32)
    est = (14 * bb * tblk * gl * 4 + 5 * bb * gl * RW_HEAD * 4 + 2 * nslot * gl * (RW_HEAD + ln) * 4
           + nslot * (ln + 8) * gl * 4 + 96 * nper * ln * gl * 4)
    return pl.pallas_call(
        functools.partial(_wkv_kernel, ln=ln, nper=nper),
        grid=(b // bb, d // gl, t // tblk),
        in_specs=[seq] * 6 + [st] + [vec] * 5,
        out_specs=[seq, st],
        out_shape=[jax.ShapeDtypeStruct((b, t, d), out_dtype), jax.ShapeDtypeStruct(s0.shape, F32)],
        scratch_shapes=[pltpu.VMEM((bb, WKV_GROUP, RW_HEAD, RW_HEAD), F32), sq, sq, tall, tall,
                        pltpu.VMEM((nslot, 1, gl), F32), pltpu.VMEM((nslot, ln, gl), F32)],
        compiler_params=_cparams(3, est),
        name="wkv",
    )(r, k, v, wpre, apre, gate, s0, kk.reshape(1, d), ka.reshape(1, d), rk.reshape(1, d),
      lnx_g.reshape(1, d), lnx_b.reshape(1, d))


def _tile(m, pref):
    return pref if m % pref == 0 else m


def _pad_rank(w1, w2):
    r = w1.shape[1]
    rp = -(-r // LANES) * LANES
    return jnp.pad(w1, ((0, 0), (0, rp - r))), jnp.pad(w2, ((0, rp - r), (0, 0)))


def _trunk(x, mem_k, mem_v, conv_prev, shift_prev, wkv_prev, paged, p, bias, far):
    b, t, d = x.shape
    m = b * t
    tm = _tile(m, 512)
    tseq = _tile(t, 512)
    h = x.reshape(m, d)
    depth = p['norm_g'].shape[0]
    ks, vs, convs, shifts, wkvs = [], [], [], [], []
    for li in range(depth):
        g = p['norm_g'][li]
        if li % 2 == 0:
            ei = li // 2
            lam_init = 0.8 - 0.6 * math.exp(-0.3 * li)
            q_dtype = BF16 if paged is None else F32
            glu, q, k, v = even_in(h, g[0], p['w_in_even'][ei], tm=tm, q_dtype=q_dtype)
            conv_out, conv_state = conv_branch(
                glu.reshape(b, t, C_CONV), conv_prev[ei], p['conv_w'][ei], p['conv_b'][ei], p['conv_ln_g'][ei],
                p['conv_ln_b'][ei], tt=_tile(t, 128), out_dtype=BF16 if t % BF16_SUBLANES == 0 else F32)
            lam = diff_lambda(p['lambda_q1'][ei], p['lambda_k1'][ei], p['lambda_q2'][ei], p['lambda_k2'][ei], lam_init)
            q3, k3, v3 = (z.reshape(b, t, DA_WIDTH) for z in (q, k, v))
            if paged is None:
                o = attn_prompt(q3, k3, v3, bias, far, lam, p['subln_g'][ei], tq=tseq, lam_init=lam_init)
            else:
                k_pool, v_pool, page_table, n_pool = paged
                o = attn_sample(q3, k3, v3, k_pool, v_pool, page_table + ei * n_pool, bias, far, lam,
                                p['subln_g'][ei], lam_init=lam_init)
            w_out = p['w_out_even'][ei]
            h = out_resnorm([conv_out.reshape(m, C_CONV), o.reshape(m, DA_WIDTH)], [w_out[:C_CONV], w_out[C_CONV:]],
                            h, g[1], tm=tm)
            ks.append(k.reshape(b, t, DA_HEADS, 2 * DA_HEAD_DIM))
            vs.append(v.reshape(b, t, DA_HEADS, 2 * DA_HEAD_DIM))
            convs.append(conv_state)
        else:
            oi = li // 2
            prompt = t >= 64
            xs, sh = rwkv_mix(h.reshape(b, t, d), g[0], shift_prev[oi], p['rw_mix'][oi], tt=_tile(t, 256),
                              out_dtype=BF16 if prompt else F32)
            xs = xs.reshape(6, m, d)
            r = proj(xs[0], None, p['rw_wr'][oi], tm=tm, tn=d)
            k = proj(xs[2], None, p['rw_wk'][oi], tm=tm, tn=d)
            v = proj(xs[3], None, p['rw_wv'][oi], tm=tm, tn=d)
            wpre = lora(xs[1], *p['rw_w12'][oi], p['rw_w0'][oi], act="tanh", tm=tm)
            apre = lora(xs[4], *p['rw_a12'][oi], p['rw_a0'][oi], act=None, tm=tm)
            gate = lora(xs[5], *p['rw_g12'][oi], jnp.zeros((d,), F32), act="sigmoid", tm=tm)
            seqs = [z.reshape(b, t, d) for z in (r, k, v, wpre, apre, gate)]
            if prompt:
                yact, s_fin = wkv(*seqs, wkv_prev[oi], p['rw_kk'][oi], p['rw_ka'][oi], p['rw_rk'][oi].reshape(d),
                                  p['rw_lnx_g'][oi], p['rw_lnx_b'][oi], bb=1, tblk=tseq, ln=64, nper=2, out_dtype=BF16)
            else:
                yact, s_fin = wkv(*seqs, wkv_prev[oi], p['rw_kk'][oi], p['rw_ka'][oi], p['rw_rk'][oi].reshape(d),
                                  p['rw_lnx_g'][oi], p['rw_lnx_b'][oi], bb=8, tblk=t, ln=t, nper=4, out_dtype=F32)
            h = out_resnorm([yact.reshape(m, d)], [p['rw_wo'][oi]], h, g[1], tm=tm)
            shifts.append(sh.reshape(b, d))
            wkvs.append(s_fin)
        qx = proj(h, g[2], p['x_wq'][li], tm=tm, tn=X_WIDTH)
        ox = xattn_core(qx.reshape(b, t, X_WIDTH), mem_k[li], mem_v[li], tt=tseq)
        h = out_resnorm([ox.reshape(m, X_WIDTH)], [p['x_wo'][li]], h, g[3], tm=tm)
        h = mlp(h, g[4], p['ff_w1'][li], p['ff_w2'][li], g[5], tm=tm, tf=512)
    return (h.reshape(b, t, d), jnp.stack(ks), jnp.stack(vs), jnp.stack(convs), jnp.stack(shifts), jnp.stack(wkvs))


def kernel(x_prompt, x_sample, cache_attn_k, cache_attn_v, cache_mem_k, cache_mem_v, state_conv, state_shift,
           state_wkv, page_table, mem_prompt, norm_g, mem_norm_g, rel_table, w_in_even, conv_w, conv_b,
           conv_ln_g, conv_ln_b, lambda_q1, lambda_k1, lambda_q2, lambda_k2, subln_g, w_out_even, rw_mix,
           rw_w0, rw_w1, rw_w2, rw_a0, rw_a1, rw_a2, rw_g1, rw_g2, rw_kk, rw_ka, rw_rk, rw_lnx_g, rw_lnx_b,
           rw_wr, rw_wk, rw_wv, rw_wo, x_wq, x_wk, x_wv, x_wo, ff_w1, ff_w2):
    bf = lambda w: w.astype(BF16)
    n_odd = rw_w1.shape[0]
    lora_pairs = lambda w1, w2: [tuple(bf(z) for z in _pad_rank(w1[o], w2[o])) for o in range(n_odd)]
    p = dict(norm_g=norm_g, w_in_even=bf(w_in_even), conv_w=conv_w, conv_b=conv_b, conv_ln_g=conv_ln_g,
             conv_ln_b=conv_ln_b, lambda_q1=lambda_q1, lambda_k1=lambda_k1, lambda_q2=lambda_q2,
             lambda_k2=lambda_k2, subln_g=subln_g, w_out_even=bf(w_out_even), rw_mix=rw_mix, rw_w0=rw_w0,
             rw_w12=lora_pairs(rw_w1, rw_w2), rw_a0=rw_a0, rw_a12=lora_pairs(rw_a1, rw_a2),
             rw_g12=lora_pairs(rw_g1, rw_g2), rw_kk=rw_kk, rw_ka=rw_ka, rw_rk=rw_rk, rw_lnx_g=rw_lnx_g,
             rw_lnx_b=rw_lnx_b, rw_wr=bf(rw_wr), rw_wk=bf(rw_wk), rw_wv=bf(rw_wv), rw_wo=bf(rw_wo), x_wq=bf(x_wq),
             x_wo=bf(x_wo), ff_w1=bf(ff_w1), ff_w2=bf(ff_w2))
    depth = norm_g.shape[0]
    bp, n_mem, d = mem_prompt.shape
    tp = x_prompt.shape[1]
    bd, td, _ = x_sample.shape
    n_even = w_in_even.shape[0]

    mem2d = mem_prompt.reshape(bp * n_mem, d)
    tmem = _tile(bp * n_mem, 512)
    mem_k_prompt = jnp.stack([proj(mem2d, mem_norm_g[l], bf(x_wk[l]), tm=tmem, tn=X_WIDTH) for l in range(depth)])
    mem_v_prompt = jnp.stack([proj(mem2d, mem_norm_g[l], bf(x_wv[l]), tm=tmem, tn=X_WIDTH) for l in range(depth)])
    mem_k_prompt = mem_k_prompt.reshape(depth, bp, n_mem, X_WIDTH)
    mem_v_prompt = mem_v_prompt.reshape(depth, bp, n_mem, X_WIDTH)

    far = rel_table[REL_BUCKETS - 1]
    tq = _tile(tp, 512)
    bias_p = bias_tiles(rel_table, tq=tq, tk=tq, nd=2, dstep=tq)
    bias_s = bias_tiles(rel_table, tq=td, tk=PAGE_SIZE, nd=2, dstep=PAGE_SIZE)

    zc = jnp.zeros((n_even, bp, CONV_W - 1, C_CONV), F32)
    zs = jnp.zeros((n_odd, bp, d), F32)
    zw = jnp.zeros((n_odd, bp, RW_HEADS, RW_HEAD, RW_HEAD), F32)
    y_p, ak_p, av_p, cv_p, sh_p, wk_p = _trunk(x_prompt, mem_k_prompt, mem_v_prompt, zc, zs, zw, None, p, bias_p, far)

    n_pool = cache_attn_k.shape[1]
    pool_shape = (n_even * n_pool, PAGE_SIZE, DA_HEADS, 2 * DA_HEAD_DIM)
    pools = (_page_rows(cache_attn_k.reshape(pool_shape)), _page_rows(cache_attn_v.reshape(pool_shape)),
             page_table, n_pool)
    y_s, ak_s, av_s, cv_s, sh_s, wk_s = _trunk(
        x_sample, cache_mem_k.reshape(depth, bd, n_mem, X_WIDTH), cache_mem_v.reshape(depth, bd, n_mem, X_WIDTH),
        state_conv, state_shift, state_wkv, pools, p, bias_s, far)

    shape_m = (depth, bp, n_mem, X_HEADS, X_HEAD_DIM)
    return (y_p, y_s, ak_p, av_p, ak_s, av_s, cv_p, cv_s, sh_p, sh_s, wk_p, wk_s,
            mem_k_prompt.reshape(shape_m), mem_v_prompt.reshape(shape_m))
```

```python
import functools
import math

import jax
import jax.numpy as jnp
from jax import lax
from jax.experimental import pallas as pl
from jax.experimental.pallas import tpu as pltpu

F32 = jnp.float32
BF16 = jnp.bfloat16

D_MODEL = 2048
PAGE_SIZE = 128
C_CONV = D_MODEL // 2
CONV_W = 31
DA_HEADS = 4
DA_HEAD_DIM = 128
DA_WIDTH = DA_HEADS * 2 * DA_HEAD_DIM
REL_BUCKETS = 32
REL_MAX_EXACT = REL_BUCKETS // 2
REL_MAX_DIST = 128
RW_HEAD = 64
RW_HEADS = D_MODEL // RW_HEAD
RW_GN_EPS = 64e-5
N_MEM = 256
X_HEADS = 4
X_HEAD_DIM = 128
X_WIDTH = X_HEADS * X_HEAD_DIM
D_FF = 4 * D_MODEL
NORM_EPS = 1e-6
LN_EPS = 1e-5

V7X_VMEM_BYTES = 64 * 1024 * 1024
V7X_VMEM_RESERVE = 6 * 1024 * 1024
LANES = 128
BF16_SUBLANES = 16
MASK_VALUE = -1e30


def _rel_bucket(n):
    if n < REL_MAX_EXACT:
        return n
    return min(REL_MAX_EXACT + int(math.log(n / REL_MAX_EXACT) / math.log(REL_MAX_DIST / REL_MAX_EXACT)
                                   * (REL_BUCKETS - REL_MAX_EXACT)), REL_BUCKETS - 1)


REL_THRESH = tuple(min(n for n in range(4 * REL_MAX_DIST) if _rel_bucket(n) >= m) for m in range(REL_BUCKETS))


def _cparams(n_grid, vmem_estimate):
    limit = min(max(int(vmem_estimate * 1.2), 32 * 1024 * 1024), V7X_VMEM_BYTES - V7X_VMEM_RESERVE)
    return pltpu.CompilerParams(dimension_semantics=("arbitrary",) * n_grid, vmem_limit_bytes=limit)


def _rms(x, g, eps=NORM_EPS):
    return x * lax.rsqrt(jnp.mean(x * x, axis=-1, keepdims=True) + eps) * g


def _mxu(x):
    return x.astype(BF16) if x.shape[0] % BF16_SUBLANES == 0 else x.astype(F32)


def _dot(a, b):
    return jnp.dot(a, b, preferred_element_type=F32)


def _dot_nt(a, b):
    return lax.dot_general(a, b, (((1,), (1,)), ((), ())), preferred_element_type=F32)


def _dot_tn(a, b):
    return lax.dot_general(a, b, (((0,), (0,)), ((), ())), preferred_element_type=F32)


def _even_in_kernel(x_ref, g_ref, w_ref, glu_ref, q_ref, k_ref, v_ref, u_ref, ga_ref):
    j = pl.program_id(1)

    @pl.when(j == 0)
    def _():
        u_ref[...] = _rms(x_ref[...], g_ref[...]).astype(BF16)

    z = _dot(u_ref[...], w_ref[...])

    @pl.when(j == 0)
    def _():
        ga_ref[...] = z

    @pl.when(j == 1)
    def _():
        glu_ref[...] = ga_ref[...] * jax.nn.sigmoid(z)

    @pl.when(j == 2)
    def _():
        q_ref[...] = z.astype(q_ref.dtype)

    @pl.when(j == 3)
    def _():
        k_ref[...] = z

    @pl.when(j == 4)
    def _():
        v_ref[...] = z


def _stacked(block, index_map, lead):
    return pl.BlockSpec((None,) + tuple(block), lambda *ids: (lead,) + tuple(index_map(*ids)))


def even_in(x, g, w_in, ei, *, tm, q_dtype):
    m, d = x.shape
    c = C_CONV
    assert w_in.shape[1:] == (d, 5 * c) and m % tm == 0
    est = 2 * tm * d * 4 + 2 * d * c * 2 + 2 * tm * c * (4 + 4 + 4 + 4) + tm * d * 2 + tm * c * 4 + 3 * tm * c * 4
    row = lambda i, j: (i, 0)
    return pl.pallas_call(
        _even_in_kernel,
        grid=(m // tm, 5),
        in_specs=[pl.BlockSpec((tm, d), row), pl.BlockSpec((1, d), lambda i, j: (0, 0)),
                  _stacked((d, c), lambda i, j: (0, j), ei)],
        out_specs=[pl.BlockSpec((tm, c), row)] * 4,
        out_shape=[jax.ShapeDtypeStruct((m, c), F32), jax.ShapeDtypeStruct((m, c), q_dtype),
                   jax.ShapeDtypeStruct((m, c), F32), jax.ShapeDtypeStruct((m, c), F32)],
        scratch_shapes=[pltpu.VMEM((tm, d), BF16), pltpu.VMEM((tm, c), F32)],
        compiler_params=_cparams(2, est),
        name="even_in",
    )(x, g.reshape(1, d), w_in)


def _proj_kernel(x_ref, g_ref, w_ref, o_ref, u_ref, *, norm):
    @pl.when(pl.program_id(1) == 0)
    def _():
        x = x_ref[...].astype(F32)
        if norm:
            x = _rms(x, g_ref[...])
        u_ref[...] = x.astype(u_ref.dtype)

    o_ref[...] = _dot(u_ref[...], w_ref[...]).astype(o_ref.dtype)


def proj(x, g, w, w_lead, *, tm, tn, x_lead=None, out_dtype=F32):
    m, kd = x.shape[-2:]
    n = w.shape[-1]
    assert m % tm == 0 and n % tn == 0 and w.shape[1] == kd
    norm = g is not None
    gg = (g if norm else jnp.ones((kd,), F32)).reshape(1, kd)
    est = 2 * tm * kd * x.dtype.itemsize + 2 * kd * tn * 2 + 2 * tm * tn * 4 + tm * kd * 2 + 2 * tm * tn * 4
    x_map = lambda i, j: (i, 0)
    x_spec = pl.BlockSpec((tm, kd), x_map) if x_lead is None else _stacked((tm, kd), x_map, x_lead)
    return pl.pallas_call(
        functools.partial(_proj_kernel, norm=norm),
        grid=(m // tm, n // tn),
        in_specs=[x_spec, pl.BlockSpec((1, kd), lambda i, j: (0, 0)),
                  _stacked((kd, tn), lambda i, j: (0, j), w_lead)],
        out_specs=pl.BlockSpec((tm, tn), lambda i, j: (i, j)),
        out_shape=jax.ShapeDtypeStruct((m, n), out_dtype),
        scratch_shapes=[pltpu.VMEM((tm, kd), BF16)],
        compiler_params=_cparams(2, est),
        name="proj",
    )(x, gg, w)


def _out_resnorm_kernel(*refs, n_in):
    xs = refs[:n_in]
    ws = refs[n_in:2 * n_in]
    h_ref, g_ref, o_ref = refs[2 * n_in:]
    y = _dot(xs[0][...].astype(BF16), ws[0][...])
    for x_ref, w_ref in zip(xs[1:], ws[1:]):
        y = y + _dot(x_ref[...].astype(BF16), w_ref[...])
    o_ref[...] = h_ref[...] + _rms(y, g_ref[...])


def out_resnorm(xs, w, w_lead, h, g, *, tm):
    m, d = h.shape
    n_in = len(xs)
    kd = xs[0].shape[1]
    assert m % tm == 0 and all(x.shape[1] == kd for x in xs) and w.shape[1:] == (n_in * kd, d)
    est = 4 * tm * d * 4 + tm * d * 4
    for x in xs:
        est += 2 * tm * kd * x.dtype.itemsize + 2 * kd * d * 2
    ws = [w] * n_in
    in_specs = [pl.BlockSpec((tm, kd), lambda i: (i, 0)) for x in xs]
    in_specs += [_stacked((kd, d), functools.partial(lambda blk, i: (blk, 0), blk), w_lead) for blk in range(n_in)]
    in_specs += [pl.BlockSpec((tm, d), lambda i: (i, 0)), pl.BlockSpec((1, d), lambda i: (0, 0))]
    return pl.pallas_call(
        functools.partial(_out_resnorm_kernel, n_in=n_in),
        grid=(m // tm,),
        in_specs=in_specs,
        out_specs=pl.BlockSpec((tm, d), lambda i: (i, 0)),
        out_shape=jax.ShapeDtypeStruct((m, d), F32),
        compiler_params=_cparams(1, est),
        name="out_resnorm",
    )(*xs, *ws, h, g.reshape(1, d))


def _mlp_kernel(h_ref, gpre_ref, w1_ref, w2_ref, gpost_ref, o_ref, u_ref, acc_ref):
    f = pl.program_id(1)

    @pl.when(f == 0)
    def _():
        u_ref[...] = _rms(h_ref[...], gpre_ref[...]).astype(BF16)
        acc_ref[...] = jnp.zeros_like(acc_ref)

    t = jnp.maximum(_dot(u_ref[...], w1_ref[...]), 0.0)
    acc_ref[...] += _dot((t * t).astype(BF16), w2_ref[...])

    @pl.when(f == pl.num_programs(1) - 1)
    def _():
        o_ref[...] = h_ref[...] + _rms(acc_ref[...], gpost_ref[...])


def mlp(h, gpre, w1, w2, li, gpost, *, tm, tf):
    m, d = h.shape
    ff = w1.shape[-1]
    assert m % tm == 0 and ff % tf == 0
    est = 4 * tm * d * 4 + 4 * d * tf * 2 + tm * d * (2 + 4) + 3 * tm * tf * 4 + tm * d * 4
    return pl.pallas_call(
        _mlp_kernel,
        grid=(m // tm, ff // tf),
        in_specs=[pl.BlockSpec((tm, d), lambda i, f: (i, 0)), pl.BlockSpec((1, d), lambda i, f: (0, 0)),
                  _stacked((d, tf), lambda i, f: (0, f), li), _stacked((tf, d), lambda i, f: (f, 0), li),
                  pl.BlockSpec((1, d), lambda i, f: (0, 0))],
        out_specs=pl.BlockSpec((tm, d), lambda i, f: (i, 0)),
        out_shape=jax.ShapeDtypeStruct((m, d), F32),
        scratch_shapes=[pltpu.VMEM((tm, d), BF16), pltpu.VMEM((tm, d), F32)],
        compiler_params=_cparams(2, est),
        name="mlp",
    )(h, gpre.reshape(1, d), w1, w2, gpost.reshape(1, d))


CONV_PAD = 32
CONV_ROWS = 32


def _conv_kernel(glu_ref, prev_ref, w_ref, b_ref, lg_ref, lb_ref, out_ref, st_ref, buf_ref, *, tt):
    i = pl.program_id(1)
    lead = CONV_PAD - (CONV_W - 1)
    c = glu_ref.shape[-1]

    @pl.when(i == 0)
    def _():
        buf_ref[0:lead, :] = jnp.zeros((lead, c), F32)
        buf_ref[lead:CONV_PAD, :] = prev_ref[0]

    @pl.when(i > 0)
    def _():
        buf_ref[0:CONV_PAD, :] = buf_ref[tt:tt + CONV_PAD, :]

    buf_ref[CONV_PAD:CONV_PAD + tt, :] = glu_ref[0]

    rows = min(CONV_ROWS, tt)
    for r0 in range(0, tt, rows):
        acc = jnp.zeros((rows, c), F32)
        for tap in range(CONV_W):
            acc = acc + buf_ref[r0 + lead + tap:r0 + lead + tap + rows, :] * w_ref[tap:tap + 1, :]
        y = acc + b_ref[...]
        mu = jnp.mean(y, axis=-1, keepdims=True)
        yc = y - mu
        var = jnp.mean(yc * yc, axis=-1, keepdims=True)
        yn = yc * lax.rsqrt(var + LN_EPS) * lg_ref[...] + lb_ref[...]
        out_ref[0, r0:r0 + rows, :] = (yn * jax.nn.sigmoid(yn)).astype(out_ref.dtype)

    @pl.when(i == pl.num_programs(1) - 1)
    def _():
        st_ref[0] = buf_ref[tt + lead:tt + CONV_PAD, :]


def conv_branch(glu, conv_prev, conv_w, conv_b, ln_g, ln_b, *, tt, out_dtype):
    b, t, c = glu.shape
    assert t % tt == 0 and (tt >= CONV_PAD or t == tt)
    est = 4 * tt * c * 4 + 4 * (CONV_W - 1) * c * 4 + (tt + CONV_PAD) * c * 4 + 2 * CONV_W * c * 4
    vec = pl.BlockSpec((1, c), lambda bi, i: (0, 0))
    return pl.pallas_call(
        functools.partial(_conv_kernel, tt=tt),
        grid=(b, t // tt),
        in_specs=[pl.BlockSpec((1, tt, c), lambda bi, i: (bi, i, 0)),
                  pl.BlockSpec((1, CONV_W - 1, c), lambda bi, i: (bi, 0, 0)),
                  pl.BlockSpec((CONV_W, c), lambda bi, i: (0, 0)), vec, vec, vec],
        out_specs=[pl.BlockSpec((1, tt, c), lambda bi, i: (bi, i, 0)),
                   pl.BlockSpec((1, CONV_W - 1, c), lambda bi, i: (bi, 0, 0))],
        out_shape=[jax.ShapeDtypeStruct((b, t, c), out_dtype), jax.ShapeDtypeStruct((b, CONV_W - 1, c), F32)],
        scratch_shapes=[pltpu.VMEM((tt + CONV_PAD, c), F32)],
        compiler_params=_cparams(2, est),
        name="conv_branch",
    )(glu, conv_prev, conv_w, conv_b.reshape(1, c), ln_g.reshape(1, c), ln_b.reshape(1, c))


def _lambda_kernel(q1_ref, k1_ref, q2_ref, k2_ref, o_ref, *, lam_init):
    s1 = jnp.sum(q1_ref[...] * k1_ref[...], axis=-1, keepdims=True)
    s2 = jnp.sum(q2_ref[...] * k2_ref[...], axis=-1, keepdims=True)
    o_ref[...] = jnp.broadcast_to(jnp.exp(s1) - jnp.exp(s2) + lam_init, o_ref.shape)


def diff_lambda(lq1, lk1, lq2, lk2, lam_init):
    d = lq1.shape[0]
    out = pl.pallas_call(
        functools.partial(_lambda_kernel, lam_init=lam_init),
        out_shape=jax.ShapeDtypeStruct((1, LANES), F32),
        name="diff_lambda",
    )(lq1.reshape(1, d), lk1.reshape(1, d), lq2.reshape(1, d), lk2.reshape(1, d))
    return out[0, :1]


def _bias_kernel(tab_ref, o_ref, *, tq, tk, dstep):
    di = pl.program_id(0)
    h = pl.program_id(1)
    n = (lax.broadcasted_iota(jnp.int32, (tq, tk), 0) - lax.broadcasted_iota(jnp.int32, (tq, tk), 1) + di * dstep)
    bias = jnp.full((tq, tk), tab_ref[(REL_BUCKETS - 1) * DA_HEADS + h], F32)
    for m in range(REL_BUCKETS - 2, -1, -1):
        bias = jnp.where(n < REL_THRESH[m + 1], tab_ref[m * DA_HEADS + h], bias)
    o_ref[0, 0] = jnp.where(n < 0, MASK_VALUE, bias)


def bias_tiles(rel_table, *, tq, tk, nd, dstep):
    return pl.pallas_call(
        functools.partial(_bias_kernel, tq=tq, tk=tk, dstep=dstep),
        grid=(nd, DA_HEADS),
        in_specs=[pl.BlockSpec(memory_space=pltpu.SMEM)],
        out_specs=pl.BlockSpec((1, 1, tq, tk), lambda di, h: (di, h, 0, 0)),
        out_shape=jax.ShapeDtypeStruct((nd, DA_HEADS, tq, tk), F32),
        compiler_params=_cparams(2, 8 * tq * tk * 4),
        name="bias_tiles",
    )(rel_table.reshape(-1))


def _softmax_step(s, v, m_ref, l_ref, acc_ref, idx):
    m_prev = m_ref[idx]
    m_new = jnp.maximum(m_prev, jnp.max(s, axis=-1, keepdims=True))
    alpha = jnp.exp(m_prev - m_new)
    p = jnp.exp(s - m_new)
    l_ref[idx] = alpha * l_ref[idx] + jnp.sum(p, axis=-1, keepdims=True)
    acc_ref[idx] = alpha * acc_ref[idx] + _dot(p.astype(v.dtype), v)
    m_ref[idx] = m_new


def _diff_finish(acc_ref, l_ref, base, lam, sg, lam_init):
    o = acc_ref[base] / l_ref[base] - lam * (acc_ref[base + 1] / l_ref[base + 1])
    return _rms(o, sg) * (1.0 - lam_init)


def _attn_prompt_kernel(lam_ref, far_ref, q_ref, k_ref, v_ref, bias_ref, sg_ref, o_ref, m_ref, l_ref, acc_ref,
                        *, lam_init):
    h = pl.program_id(1)
    i = pl.program_id(2)
    j = pl.program_id(3)
    dh = DA_HEAD_DIM

    @pl.when(j == 0)
    def _():
        m_ref[...] = jnp.full(m_ref.shape, MASK_VALUE, F32)
        l_ref[...] = jnp.zeros_like(l_ref)
        acc_ref[...] = jnp.zeros_like(acc_ref)

    @pl.when(j <= i)
    def _():
        q = q_ref[0]
        kb = k_ref[0].astype(BF16)
        vb = v_ref[0].astype(BF16)
        bias = jnp.where(j >= i - 1, bias_ref[0, 0], far_ref[h])
        for c in range(2):
            s = _dot_nt(q[:, c * dh:(c + 1) * dh], kb[:, c * dh:(c + 1) * dh]) * (dh ** -0.5) + bias
            _softmax_step(s, vb, m_ref, l_ref, acc_ref, c)

    @pl.when(j == i)
    def _():
        o_ref[0] = _diff_finish(acc_ref, l_ref, 0, lam_ref[0], sg_ref[...], lam_init).astype(o_ref.dtype)


def attn_prompt(q, k, v, bias, far, lam, subln_g, *, tq, lam_init):
    b, t, _ = q.shape
    hw = 2 * DA_HEAD_DIM
    nq = t // tq
    est = 2 * tq * hw * (2 + 4 + 4 + 2) + 2 * tq * tq * 4 + 2 * tq * hw * 4 + 8 * tq * tq * 4
    smem = pl.BlockSpec(memory_space=pltpu.SMEM)
    return pl.pallas_call(
        functools.partial(_attn_prompt_kernel, lam_init=lam_init),
        grid=(b, DA_HEADS, nq, nq),
        in_specs=[smem, smem,
                  pl.BlockSpec((1, tq, hw), lambda bi, h, i, j: (bi, i, h)),
                  pl.BlockSpec((1, tq, hw), lambda bi, h, i, j: (bi, jnp.minimum(i, j), h)),
                  pl.BlockSpec((1, tq, hw), lambda bi, h, i, j: (bi, jnp.minimum(i, j), h)),
                  pl.BlockSpec((1, 1, tq, tq), lambda bi, h, i, j: (jnp.where(j >= i, 0, 1), h, 0, 0)),
                  pl.BlockSpec((1, hw), lambda bi, h, i, j: (0, 0))],
        out_specs=pl.BlockSpec((1, tq, hw), lambda bi, h, i, j: (bi, i, h)),
        out_shape=jax.ShapeDtypeStruct((b, t, DA_WIDTH), BF16),
        scratch_shapes=[pltpu.VMEM((2, tq, 1), F32), pltpu.VMEM((2, tq, 1), F32), pltpu.VMEM((2, tq, hw), F32)],
        compiler_params=_cparams(4, est),
        name="attn_prompt",
    )(lam, far, q, k, v, bias, subln_g.reshape(1, hw))


PAGES_PER_STEP = 8


PAGE_ROWS = PAGE_SIZE * 2 * DA_HEADS


def _page_rows(pool):
    pg = pool.shape[0]
    x = pool.reshape(pg, PAGE_SIZE, DA_HEADS, 2, DA_HEAD_DIM)
    return jnp.transpose(x, (0, 1, 3, 2, 4)).reshape(pg, PAGE_ROWS, DA_HEAD_DIM)


def _stack_heads(x, half=None):
    dh = DA_HEAD_DIM
    parts = []
    for c in range(2):
        for h in range(DA_HEADS):
            lo = h * 2 * dh + (c if half is None else half) * dh
            parts.append(x[:, lo:lo + dh])
    return jnp.concatenate(parts, axis=0)


def _attn_sample_kernel(pt_ref, lam_ref, q_ref, kn_ref, vn_ref, ptile_ref, ntile_ref, sg_ref, *rest, lam_init):
    pages = rest[:2 * PAGES_PER_STEP]
    o_ref, m_ref, l_ref, acc_ref = rest[2 * PAGES_PER_STEP:]
    s_idx = pl.program_id(1)
    last = pl.num_programs(1) - 1
    dh = DA_HEAD_DIM
    t = q_ref.shape[1]
    nq = DA_HEADS * t
    scale = dh ** -0.5

    @pl.when(s_idx == 0)
    def _():
        m_ref[...] = jnp.full(m_ref.shape, MASK_VALUE, F32)
        l_ref[...] = jnp.zeros_like(l_ref)
        acc_ref[...] = jnp.zeros_like(acc_ref)

    qm = _stack_heads(q_ref[0])

    def update(s, weighted_values):
        m_prev = m_ref[...]
        m_new = jnp.maximum(m_prev, jnp.max(s, axis=-1, keepdims=True))
        alpha = jnp.exp(m_prev - m_new)
        p = jnp.exp(s - m_new)
        l_ref[...] = alpha * l_ref[...] + jnp.sum(p, axis=-1, keepdims=True)
        lo, hi = weighted_values(p)
        acc_ref[:, :dh] = alpha * acc_ref[:, :dh] + lo
        acc_ref[:, dh:] = alpha * acc_ref[:, dh:] + hi
        m_ref[...] = m_new

    scores = []
    for pp in range(PAGES_PER_STEP):
        tile = ptile_ref[0]
        if pp == PAGES_PER_STEP - 1:
            tile = jnp.where(s_idx == last, ptile_ref[1], tile)
        scores.append(_dot_nt(qm, pages[2 * pp][0]) * scale + tile)

    def page_values(p):
        width = p.shape[1]
        top, bot = p[:nq], p[nq:]
        lhs = jnp.concatenate([top, pltpu.roll(bot, width - DA_HEADS, axis=1),
                               pltpu.roll(top, DA_HEADS, axis=1), bot], axis=0)
        res = _dot(lhs[:, :PAGE_ROWS], pages[1][0])
        for pp in range(1, PAGES_PER_STEP):
            res = res + _dot(lhs[:, pp * PAGE_ROWS:(pp + 1) * PAGE_ROWS], pages[2 * pp + 1][0])
        return res[:2 * nq], res[2 * nq:]

    update(jnp.concatenate(scores, axis=1), page_values)

    @pl.when(s_idx == last)
    def _():
        vn = vn_ref[0]
        update(_dot_nt(qm, _stack_heads(kn_ref[0])) * scale + ntile_ref[...],
               lambda p: (_dot(p, _stack_heads(vn, 0)), _dot(p, _stack_heads(vn, 1))))
        o = acc_ref[...] / l_ref[...]
        o = _rms(o[:nq] - lam_ref[0] * o[nq:], sg_ref[...]) * (1.0 - lam_init)
        for h in range(DA_HEADS):
            o_ref[0, :, h * 2 * dh:(h + 1) * 2 * dh] = o[h * t:(h + 1) * t].astype(o_ref.dtype)


def _sample_tiles(bias, far):
    _, nh, t, tk = bias.shape
    c_row = jnp.arange(2).reshape(2, 1, 1, 1, 1, 1)
    h_row = jnp.arange(nh).reshape(1, nh, 1, 1, 1, 1)
    own_page = (c_row == jnp.arange(2).reshape(1, 1, 1, 1, 2, 1)) & (h_row == jnp.arange(nh).reshape(1, 1, 1, 1, 1, nh))
    base = jnp.stack([jnp.broadcast_to(far.reshape(nh, 1, 1), (nh, t, tk)), bias[1]])
    ptile = jnp.where(own_page[None], base[:, None, :, :, :, None, None], MASK_VALUE).reshape(2, 2 * nh * t, tk * 2 * nh)
    own_new = (c_row == jnp.arange(2).reshape(1, 1, 1, 2, 1, 1)) & (h_row == jnp.arange(nh).reshape(1, 1, 1, 1, nh, 1))
    ntile = jnp.where(own_new, bias[0][None, :, :, None, None, :t], MASK_VALUE).reshape(2 * nh * t, 2 * nh * t)
    return ptile, ntile


def attn_sample(q, k_new, v_new, k_rows, v_rows, page_table, bias, far, lam, subln_g, *, lam_init):
    b, t, w = q.shape
    n_pages = page_table.shape[1]
    assert n_pages % PAGES_PER_STEP == 0 and k_rows.shape[1:] == (PAGE_ROWS, DA_HEAD_DIM)
    hw = 2 * DA_HEAD_DIM
    ptile, ntile = _sample_tiles(bias, far)
    smem = pl.BlockSpec(memory_space=pltpu.SMEM)
    row = pl.BlockSpec((1, t, w), lambda bi, s, pt: (bi, 0, 0))

    def page_spec(pp):
        return pl.BlockSpec((1, PAGE_ROWS, DA_HEAD_DIM), lambda bi, s, pt: (pt[bi, s * PAGES_PER_STEP + pp], 0, 0))

    in_specs = [smem, row, row, row,
                pl.BlockSpec(ptile.shape, lambda bi, s, pt: (0, 0, 0)),
                pl.BlockSpec(ntile.shape, lambda bi, s, pt: (0, 0)),
                pl.BlockSpec((1, hw), lambda bi, s, pt: (0, 0))]
    pools = []
    for pp in range(PAGES_PER_STEP):
        in_specs += [page_spec(pp), page_spec(pp)]
        pools += [k_rows, v_rows]
    nr = 2 * DA_HEADS * t
    est = (4 * PAGES_PER_STEP * PAGE_ROWS * DA_HEAD_DIM * 4 + 16 * t * w * 4 + 4 * ptile.size * 4
           + 12 * nr * PAGES_PER_STEP * PAGE_ROWS * 4)
    return pl.pallas_call(
        functools.partial(_attn_sample_kernel, lam_init=lam_init),
        grid_spec=pltpu.PrefetchScalarGridSpec(
            num_scalar_prefetch=1,
            grid=(b, n_pages // PAGES_PER_STEP),
            in_specs=in_specs,
            out_specs=pl.BlockSpec((1, t, w), lambda bi, s, pt: (bi, 0, 0)),
            scratch_shapes=[pltpu.VMEM((nr, 1), F32), pltpu.VMEM((nr, 1), F32), pltpu.VMEM((nr, hw), F32)]),
        out_shape=jax.ShapeDtypeStruct((b, t, w), F32),
        compiler_params=_cparams(2, est),
        name="attn_sample",
    )(page_table, lam, q, k_new, v_new, ptile, ntile, subln_g.reshape(1, hw), *pools)


def _xattn_kernel(q_ref, mk_ref, mv_ref, o_ref):
    q = _mxu(q_ref[0])
    mk = mk_ref[0].astype(q.dtype)
    mv = mv_ref[0].astype(q.dtype)
    dh = X_HEAD_DIM
    for h in range(X_HEADS):
        sl = slice(h * dh, (h + 1) * dh)
        s = _dot_nt(q[:, sl], mk[:, sl]) * (dh ** -0.5)
        p = jnp.exp(s - jnp.max(s, axis=-1, keepdims=True))
        p = p / jnp.sum(p, axis=-1, keepdims=True)
        o_ref[0, :, sl] = _dot(p.astype(q.dtype), mv[:, sl]).astype(o_ref.dtype)


def xattn_core(q, mk, mv, *, tt):
    b, t, w = q.shape
    n_mem = mk.shape[1]
    est = 4 * tt * w * 4 + 4 * n_mem * w * 4 + 6 * tt * n_mem * 4
    return pl.pallas_call(
        _xattn_kernel,
        grid=(b, t // tt),
        in_specs=[pl.BlockSpec((1, tt, w), lambda bi, i: (bi, i, 0)),
                  pl.BlockSpec((1, n_mem, w), lambda bi, i: (bi, 0, 0)),
                  pl.BlockSpec((1, n_mem, w), lambda bi, i: (bi, 0, 0))],
        out_specs=pl.BlockSpec((1, tt, w), lambda bi, i: (bi, i, 0)),
        out_shape=jax.ShapeDtypeStruct((b, t, w), F32),
        compiler_params=_cparams(2, est),
        name="xattn_core",
    )(q, mk, mv)


XATTN_SEQS = 4


def _xattn_rows_kernel(q_ref, mk_ref, mv_ref, mask_ref, o_ref):
    dh = X_HEAD_DIM
    t = q_ref.shape[1]
    for bi in range(q_ref.shape[0]):
        q = q_ref[bi]
        qm = jnp.concatenate([q[:, h * dh:(h + 1) * dh] for h in range(X_HEADS)], axis=0)
        s = _dot_nt(qm, mk_ref[bi]) * (dh ** -0.5) + mask_ref[...]
        p = jnp.exp(s - jnp.max(s, axis=-1, keepdims=True))
        p = p / jnp.sum(p, axis=-1, keepdims=True)
        res = _dot(p, mv_ref[bi])
        for h in range(X_HEADS):
            o_ref[bi, :, h * dh:(h + 1) * dh] = res[h * t:(h + 1) * t]


def xattn_rows(q, mk_rows, mv_rows, li):
    b, t, w = q.shape
    nr = mk_rows.shape[1]
    assert b % XATTN_SEQS == 0 and mk_rows.shape[2] == X_HEAD_DIM
    row_head = jnp.arange(X_HEADS * t).reshape(-1, 1) // t
    col_head = jnp.arange(nr).reshape(1, -1) % X_HEADS
    mask = jnp.where(row_head == col_head, 0.0, MASK_VALUE).astype(F32)
    nb = b // XATTN_SEQS
    mem = pl.BlockSpec((XATTN_SEQS, nr, X_HEAD_DIM), lambda i: (li * nb + i, 0, 0))
    est = 4 * XATTN_SEQS * (nr * X_HEAD_DIM + 2 * t * w) * 4 + 2 * mask.size * 4 + 8 * X_HEADS * t * nr * 4
    return pl.pallas_call(
        _xattn_rows_kernel,
        grid=(nb,),
        in_specs=[pl.BlockSpec((XATTN_SEQS, t, w), lambda i: (i, 0, 0)), mem, mem,
                  pl.BlockSpec(mask.shape, lambda i: (0, 0))],
        out_specs=pl.BlockSpec((XATTN_SEQS, t, w), lambda i: (i, 0, 0)),
        out_shape=jax.ShapeDtypeStruct((b, t, w), F32),
        compiler_params=_cparams(1, est),
        name="xattn_rows",
    )(q, mk_rows, mv_rows, mask)


def _rwkv_mix_kernel(h_ref, g_ref, sp_ref, mix_ref, xs_ref, sh_ref, carry_ref):
    i = pl.program_id(1)

    @pl.when(i == 0)
    def _():
        carry_ref[...] = sp_ref[0]

    u = _rms(h_ref[0], g_ref[...])
    tt = u.shape[0]
    row = lax.broadcasted_iota(jnp.int32, u.shape, 0)
    prev = jnp.where(row == 0, carry_ref[...], pltpu.roll(u, 1, axis=0))
    xx = prev - u
    for n in range(6):
        xs_ref[n, 0] = (u + xx * mix_ref[n:n + 1, :]).astype(xs_ref.dtype)
    carry_ref[...] = u[tt - 1:tt, :]
    sh_ref[0] = u[tt - 1:tt, :]


def rwkv_mix(h, g, shift_prev, mix, *, tt, out_dtype):
    b, t, d = h.shape
    est = 2 * tt * d * 4 + 12 * tt * d * 4 + 6 * tt * d * 4
    return pl.pallas_call(
        _rwkv_mix_kernel,
        grid=(b, t // tt),
        in_specs=[pl.BlockSpec((1, tt, d), lambda bi, i: (bi, i, 0)), pl.BlockSpec((1, d), lambda bi, i: (0, 0)),
                  pl.BlockSpec((1, 1, d), lambda bi, i: (bi, 0, 0)), pl.BlockSpec((6, d), lambda bi, i: (0, 0))],
        out_specs=[pl.BlockSpec((6, 1, tt, d), lambda bi, i: (0, bi, i, 0)),
                   pl.BlockSpec((1, 1, d), lambda bi, i: (bi, 0, 0))],
        out_shape=[jax.ShapeDtypeStruct((6, b, t, d), out_dtype), jax.ShapeDtypeStruct((b, 1, d), F32)],
        scratch_shapes=[pltpu.VMEM((1, d), F32)],
        compiler_params=_cparams(2, est),
        name="rwkv_mix",
    )(h, g.reshape(1, d), shift_prev.reshape(b, 1, d), mix)


def _lora_kernel(x_ref, w1_ref, w2_ref, b_ref, o_ref, *, act):
    t = _dot(x_ref[...].astype(BF16), w1_ref[...])
    if act == "tanh":
        t = jnp.tanh(t)
    elif act == "sigmoid":
        t = jax.nn.sigmoid(t)
    o_ref[...] = _dot(t.astype(BF16), w2_ref[...]) + b_ref[...]


def lora(x, x_lead, w1, w2, bias, *, act, tm):
    _, m, d = x.shape
    r = w1.shape[1]
    n = w2.shape[1]
    est = 2 * tm * d * x.dtype.itemsize + 4 * d * r * 2 + 4 * r * n * 2 + 4 * tm * n * 4
    return pl.pallas_call(
        functools.partial(_lora_kernel, act=act),
        grid=(m // tm,),
        in_specs=[_stacked((tm, d), lambda i: (i, 0), x_lead), pl.BlockSpec((d, r), lambda i: (0, 0)),
                  pl.BlockSpec((r, n), lambda i: (0, 0)), pl.BlockSpec((1, n), lambda i: (0, 0))],
        out_specs=pl.BlockSpec((tm, n), lambda i: (i, 0)),
        out_shape=jax.ShapeDtypeStruct((m, n), F32),
        compiler_params=_cparams(1, est),
        name="lora",
    )(x, w1, w2, bias.reshape(1, n))


WKV_GROUP = 4
WKV_PROMPT_CHUNKS = 4
WKV_SAMPLE_SEQS = 8


def _cumsum_rows(x):
    n = x.shape[0]
    row = lax.broadcasted_iota(jnp.int32, x.shape, 0)
    step = 1
    while step < n:
        x = x + jnp.where(row >= step, pltpu.roll(x, step, axis=0), 0.0)
        step *= 2
    return x


def _head_ones(width):
    shift = RW_HEAD.bit_length() - 1
    assert RW_HEAD == 1 << shift
    gi = lax.shift_right_logical(lax.broadcasted_iota(jnp.int32, (width, width), 0), shift)
    gj = lax.shift_right_logical(lax.broadcasted_iota(jnp.int32, (width, width), 1), shift)
    return jnp.where(gi == gj, 1.0, 0.0).astype(F32)


def _head_sums(x, head_ones):
    hi = x.astype(BF16).astype(F32)
    return _dot(hi, head_ones) + _dot(x - hi, head_ones)


def _wkv_solve_chunks(chunks, kk_p, ka_p, rk_p, head_ones):
    ln = chunks[0][0].shape[0]
    n = RW_HEAD
    ri = lax.broadcasted_iota(jnp.int32, (ln, 2 * ln), 0)
    ci = lax.broadcasted_iota(jnp.int32, (ln, 2 * ln), 1)
    ci = jnp.where(ci < ln, ci, ci - ln)
    mask_a = ci < ri
    mask_r = ci <= ri
    nlev = max(1, (ln - 1).bit_length())
    zeros = jnp.zeros((ln, n), F32)

    pairs = []
    w_ends = []
    bonuses = []
    for r, kr, v, wp, ap in chunks:
        a = jax.nn.sigmoid(ap)
        lw = -jnp.exp(-jax.nn.softplus(-wp) - 0.5)
        cum = _cumsum_rows(lw)
        cl = cum[ln - 1:ln, :]
        w_in = jnp.exp(cum)
        w_ex = jnp.exp(cum - lw)
        w_inv = jnp.exp(-cum)
        w_rem = jnp.exp(cl - cum)
        w_ends.append(jnp.exp(cl))
        kkv = kr * kk_p
        k = kr * (1.0 + (a - 1.0) * ka_p)
        kkn = kkv / jnp.maximum(jnp.sqrt(_head_sums(kkv * kkv, head_ones)), 1e-12)
        bt = kkn * a
        bonuses.append(_head_sums(r * k * rk_p, head_ones) * v)
        wide = dict(a_til=-kkn * w_ex, r_til=r * w_in, b_til=bt * w_inv, k_til=k * w_inv, bp=bt * w_rem,
                    kp=k * w_rem, v=v)
        for h in range(WKV_GROUP):
            pairs.append({name: z[:, h * n:(h + 1) * n] for name, z in wide.items()})

    for p in pairs:
        p['amat'] = _dot_nt(jnp.concatenate([p['a_til'], p['r_til']], axis=0),
                            jnp.concatenate([p['b_til'], p['k_til']], axis=0))
    for p in pairs:
        p['a_rows'] = jnp.where(mask_a, p['amat'][:ln], 0.0)
        p['r_rows'] = jnp.where(mask_r, p['amat'][ln:], 0.0)
        p['pw'] = p['a_rows'][:, :ln]
        p['akv'] = _dot(p['a_rows'], jnp.concatenate([zeros, p['v']], axis=0))
    eye = jnp.where(lax.broadcasted_iota(jnp.int32, (ln, ln), 0) == lax.broadcasted_iota(jnp.int32, (ln, ln), 1),
                    1.0, 0.0)
    for p in pairs:
        p['inv'] = eye + p['pw']
    for _ in range(1, nlev):
        for p in pairs:
            p['pw'] = _dot(p['pw'], p['pw'])
        for p in pairs:
            p['inv'] = p['inv'] + _dot(p['pw'], p['inv'])
    for p in pairs:
        p['x'] = _dot(p['inv'], jnp.concatenate([p['a_til'], p['akv']], axis=1))
    out = []
    for p in pairs:
        tc = _dot_tn(p['x'], p['bp'])
        vk = _dot_tn(p['v'], p['kp'])
        ry = _dot(p['r_rows'], jnp.concatenate([p['x'], jnp.concatenate([zeros, p['v']], axis=1)], axis=0))
        out.append((tc[:n], tc[n:] + vk, p['r_til'] + ry[:, :n], ry[:, n:]))
    return out, w_ends, bonuses


def _wkv_kernel(r_ref, k_ref, v_ref, w_ref, a_ref, g_ref, s0_ref, kk_ref, ka_ref, rk_ref, lg_ref, lb_ref,
                y_ref, sT_ref, s_scr, tm_scr, cm_scr, rq_scr, yc_scr, we_scr, bon_scr, y_scr, *, ln, nper):
    tb = pl.program_id(2)
    bb, tblk, _ = r_ref.shape
    nck = tblk // ln
    nslot = bb * nck
    n = RW_HEAD

    @pl.when(tb == 0)
    def _():
        s_scr[...] = s0_ref[...]

    kk_p, ka_p, rk_p = kk_ref[...], ka_ref[...], rk_ref[...]
    head_ones = _head_ones(r_ref.shape[2])

    def slot_index(slot):
        if nck == 1:
            return slot, 0
        return 0, slot

    def rows_of(c):
        return pl.ds(c * ln if isinstance(c, int) else pl.multiple_of(c * ln, ln), ln)

    def solve(it, carry):
        chunks = []
        for q in range(nper):
            bi, c = slot_index(it * nper + q)
            rows = rows_of(c)
            chunks.append(tuple(ref[bi, rows, :] for ref in (r_ref, k_ref, v_ref, w_ref, a_ref)))
        mats, w_ends, bonuses = _wkv_solve_chunks(chunks, kk_p, ka_p, rk_p, head_ones)
        for q in range(nper):
            slot = it * nper + q
            we_scr[slot] = w_ends[q]
            bon_scr[slot] = bonuses[q]
            for h in range(WKV_GROUP):
                tmat, cmat, rq, yc = mats[q * WKV_GROUP + h]
                tm_scr[slot, h] = tmat
                cm_scr[slot, h] = cmat
                rq_scr[slot, h] = rq
                yc_scr[slot, h] = yc
        return carry

    lax.fori_loop(0, nslot // nper, solve, 0)

    lg, lb = lg_ref[...], lb_ref[...]

    def advance(c, states):
        new_states = []
        for bi in range(bb):
            slot = bi * nck + c
            w_end = we_scr[slot]
            ys = []
            for h in range(WKV_GROUP):
                s = states[bi * WKV_GROUP + h]
                sl = slice(h * n, (h + 1) * n)
                ys.append(_dot_nt(rq_scr[slot, h], s) + yc_scr[slot, h])
                new_states.append(s * w_end[:, sl] + _dot(s, tm_scr[slot, h]) + cm_scr[slot, h])
            y_scr[slot] = jnp.concatenate(ys, axis=-1)
        return tuple(new_states)

    states = tuple(s_scr[bi, h] for bi in range(bb) for h in range(WKV_GROUP))
    if nck == 1:
        states = advance(0, states)
    else:
        states = lax.fori_loop(0, nck, advance, states)
    for bi in range(bb):
        for h in range(WKV_GROUP):
            s_scr[bi, h] = states[bi * WKV_GROUP + h]

    def finish(slots):
        ys = [y_scr[slot] for slot in slots]
        ycs = [y - _head_sums(y, head_ones) * (1.0 / n) for y in ys]
        vrs = [_head_sums(yc * yc, head_ones) * (1.0 / n) for yc in ycs]
        for slot, yc, var in zip(slots, ycs, vrs):
            bi, c = slot_index(slot)
            rows = rows_of(c)
            yn = yc * lax.rsqrt(var + RW_GN_EPS) * lg + lb + bon_scr[slot]
            y_ref[bi, rows, :] = (yn * g_ref[bi, rows, :]).astype(y_ref.dtype)

    finish(list(range(nslot)))

    @pl.when(tb == pl.num_programs(2) - 1)
    def _():
        sT_ref[...] = s_scr[...]


def wkv(r, k, v, wpre, apre, gate, s0, kk, ka, rk, lnx_g, lnx_b, *, bb, tblk, ln, nper, out_dtype):
    b, t, d = r.shape
    gl = WKV_GROUP * RW_HEAD
    nslot = bb * (tblk // ln)
    assert b % bb == 0 and t % tblk == 0 and tblk % ln == 0 and d % gl == 0 and nslot % nper == 0
    assert bb == 1 or tblk == ln
    seq = pl.BlockSpec((bb, tblk, gl), lambda bi, g, i: (bi, i, g))
    st = pl.BlockSpec((bb, WKV_GROUP, RW_HEAD, RW_HEAD), lambda bi, g, i: (bi, g, 0, 0))
    vec = pl.BlockSpec((1, gl), lambda bi, g, i: (0, g))
    sq = pltpu.VMEM((nslot, WKV_GROUP, RW_HEAD, RW_HEAD), F32)
    tall = pltpu.VMEM((nslot, WKV_GROUP, ln, RW_HEAD), F32)
    est = (14 * bb * tblk * gl * 4 + 5 * bb * gl * RW_HEAD * 4 + 2 * nslot * gl * (RW_HEAD + ln) * 4
           + nslot * (2 * ln + 8) * gl * 4 + 96 * nper * ln * gl * 4)
    return pl.pallas_call(
        functools.partial(_wkv_kernel, ln=ln, nper=nper),
        grid=(b // bb, d // gl, t // tblk),
        in_specs=[seq] * 6 + [st] + [vec] * 5,
        out_specs=[seq, st],
        out_shape=[jax.ShapeDtypeStruct((b, t, d), out_dtype), jax.ShapeDtypeStruct(s0.shape, F32)],
        scratch_shapes=[pltpu.VMEM((bb, WKV_GROUP, RW_HEAD, RW_HEAD), F32), sq, sq, tall, tall,
                        pltpu.VMEM((nslot, 1, gl), F32), pltpu.VMEM((nslot, ln, gl), F32),
                        pltpu.VMEM((nslot, ln, gl), F32)],
        compiler_params=_cparams(3, est),
        name="wkv",
    )(r, k, v, wpre, apre, gate, s0, kk.reshape(1, d), ka.reshape(1, d), rk.reshape(1, d),
      lnx_g.reshape(1, d), lnx_b.reshape(1, d))


def _tile(m, pref):
    return pref if m % pref == 0 else m


def _pad_rank(w1, w2):
    r = w1.shape[1]
    rp = -(-r // LANES) * LANES
    return jnp.pad(w1, ((0, 0), (0, rp - r))), jnp.pad(w2, ((0, rp - r), (0, 0)))


def _trunk(x, xattn, conv_prev, shift_prev, wkv_prev, paged, p, bias, far):
    b, t, d = x.shape
    m = b * t
    tm = _tile(m, 512)
    tseq = _tile(t, 512)
    h = x.reshape(m, d)
    depth = p['norm_g'].shape[0]
    ks, vs, convs, shifts, wkvs = [], [], [], [], []
    for li in range(depth):
        g = p['norm_g'][li]
        if li % 2 == 0:
            ei = li // 2
            lam_init = 0.8 - 0.6 * math.exp(-0.3 * li)
            q_dtype = BF16 if paged is None else F32
            glu, q, k, v = even_in(h, g[0], p['w_in_even'], ei, tm=tm, q_dtype=q_dtype)
            conv_out, conv_state = conv_branch(
                glu.reshape(b, t, C_CONV), conv_prev[ei], p['conv_w'][ei], p['conv_b'][ei], p['conv_ln_g'][ei],
                p['conv_ln_b'][ei], tt=_tile(t, 128), out_dtype=BF16 if t % BF16_SUBLANES == 0 else F32)
            lam = diff_lambda(p['lambda_q1'][ei], p['lambda_k1'][ei], p['lambda_q2'][ei], p['lambda_k2'][ei], lam_init)
            q3, k3, v3 = (z.reshape(b, t, DA_WIDTH) for z in (q, k, v))
            if paged is None:
                o = attn_prompt(q3, k3, v3, bias, far, lam, p['subln_g'][ei], tq=tseq, lam_init=lam_init)
            else:
                k_pool, v_pool, page_table, n_pool = paged
                o = attn_sample(q3, k3, v3, k_pool, v_pool, page_table + ei * n_pool, bias, far, lam,
                                p['subln_g'][ei], lam_init=lam_init)
            h = out_resnorm([conv_out.reshape(m, C_CONV), o.reshape(m, DA_WIDTH)], p['w_out_even'], ei, h, g[1], tm=tm)
            ks.append(k.reshape(b, t, DA_HEADS, 2 * DA_HEAD_DIM))
            vs.append(v.reshape(b, t, DA_HEADS, 2 * DA_HEAD_DIM))
            convs.append(conv_state)
        else:
            oi = li // 2
            prompt = t >= 64
            xs, sh = rwkv_mix(h.reshape(b, t, d), g[0], shift_prev[oi], p['rw_mix'][oi], tt=_tile(t, 256),
                              out_dtype=BF16 if prompt else F32)
            xs = xs.reshape(6, m, d)
            r = proj(xs, None, p['rw_wr'], oi, x_lead=0, tm=tm, tn=d)
            k = proj(xs, None, p['rw_wk'], oi, x_lead=2, tm=tm, tn=d)
            v = proj(xs, None, p['rw_wv'], oi, x_lead=3, tm=tm, tn=d)
            wpre = lora(xs, 1, *p['rw_w12'][oi], p['rw_w0'][oi], act="tanh", tm=tm)
            apre = lora(xs, 4, *p['rw_a12'][oi], p['rw_a0'][oi], act=None, tm=tm)
            gate = lora(xs, 5, *p['rw_g12'][oi], jnp.zeros((d,), F32), act="sigmoid", tm=tm)
            seqs = [z.reshape(b, t, d) for z in (r, k, v, wpre, apre, gate)]
            wkv_params = (wkv_prev[oi], p['rw_kk'][oi], p['rw_ka'][oi], p['rw_rk'][oi].reshape(d), p['rw_lnx_g'][oi],
                          p['rw_lnx_b'][oi])
            if prompt:
                yact, s_fin = wkv(*seqs, *wkv_params, bb=1, tblk=tseq, ln=64, nper=WKV_PROMPT_CHUNKS, out_dtype=BF16)
            else:
                yact, s_fin = wkv(*seqs, *wkv_params, bb=8, tblk=t, ln=t, nper=WKV_SAMPLE_SEQS, out_dtype=F32)
            h = out_resnorm([yact.reshape(m, d)], p['rw_wo'], oi, h, g[1], tm=tm)
            shifts.append(sh.reshape(b, d))
            wkvs.append(s_fin)
        qx = proj(h, g[2], p['x_wq'], li, tm=tm, tn=X_WIDTH)
        ox = xattn(li, qx.reshape(b, t, X_WIDTH))
        h = out_resnorm([ox.reshape(m, X_WIDTH)], p['x_wo'], li, h, g[3], tm=tm)
        h = mlp(h, g[4], p['ff_w1'], p['ff_w2'], li, g[5], tm=tm, tf=512)
    return (h.reshape(b, t, d), jnp.stack(ks), jnp.stack(vs), jnp.stack(convs), jnp.stack(shifts), jnp.stack(wkvs))


def kernel(x_prompt, x_sample, cache_attn_k, cache_attn_v, cache_mem_k, cache_mem_v, state_conv, state_shift,
           state_wkv, page_table, mem_prompt, norm_g, mem_norm_g, rel_table, w_in_even, conv_w, conv_b,
           conv_ln_g, conv_ln_b, lambda_q1, lambda_k1, lambda_q2, lambda_k2, subln_g, w_out_even, rw_mix,
           rw_w0, rw_w1, rw_w2, rw_a0, rw_a1, rw_a2, rw_g1, rw_g2, rw_kk, rw_ka, rw_rk, rw_lnx_g, rw_lnx_b,
           rw_wr, rw_wk, rw_wv, rw_wo, x_wq, x_wk, x_wv, x_wo, ff_w1, ff_w2):
    bf = lambda w: w.astype(BF16)
    n_odd = rw_w1.shape[0]
    lora_pairs = lambda w1, w2: [tuple(bf(z) for z in _pad_rank(w1[o], w2[o])) for o in range(n_odd)]
    p = dict(norm_g=norm_g, w_in_even=bf(w_in_even), conv_w=conv_w, conv_b=conv_b, conv_ln_g=conv_ln_g,
             conv_ln_b=conv_ln_b, lambda_q1=lambda_q1, lambda_k1=lambda_k1, lambda_q2=lambda_q2,
             lambda_k2=lambda_k2, subln_g=subln_g, w_out_even=bf(w_out_even), rw_mix=rw_mix, rw_w0=rw_w0,
             rw_w12=lora_pairs(rw_w1, rw_w2), rw_a0=rw_a0, rw_a12=lora_pairs(rw_a1, rw_a2),
             rw_g12=lora_pairs(rw_g1, rw_g2), rw_kk=rw_kk, rw_ka=rw_ka, rw_rk=rw_rk, rw_lnx_g=rw_lnx_g,
             rw_lnx_b=rw_lnx_b, rw_wr=bf(rw_wr), rw_wk=bf(rw_wk), rw_wv=bf(rw_wv), rw_wo=bf(rw_wo), x_wq=bf(x_wq),
             x_wo=bf(x_wo), ff_w1=bf(ff_w1), ff_w2=bf(ff_w2))
    depth = norm_g.shape[0]
    bp, n_mem, d = mem_prompt.shape
    tp = x_prompt.shape[1]
    bd, td, _ = x_sample.shape
    n_even = w_in_even.shape[0]

    mem2d = mem_prompt.reshape(bp * n_mem, d)
    tmem = _tile(bp * n_mem, 512)
    wk_bf, wv_bf = bf(x_wk), bf(x_wv)
    mem_k_layers = [proj(mem2d, mem_norm_g[l], wk_bf, l, tm=tmem, tn=X_WIDTH).reshape(bp, n_mem, X_WIDTH)
                    for l in range(depth)]
    mem_v_layers = [proj(mem2d, mem_norm_g[l], wv_bf, l, tm=tmem, tn=X_WIDTH).reshape(bp, n_mem, X_WIDTH)
                    for l in range(depth)]

    def xattn_prompt(li, qx):
        return xattn_core(qx, mem_k_layers[li], mem_v_layers[li], tt=_tile(tp, 512))

    mem_rows = (depth * bd, n_mem * X_HEADS, X_HEAD_DIM)
    mk_rows, mv_rows = cache_mem_k.reshape(mem_rows), cache_mem_v.reshape(mem_rows)

    def xattn_sample(li, qx):
        return xattn_rows(qx, mk_rows, mv_rows, li)

    far = rel_table[REL_BUCKETS - 1]
    tq = _tile(tp, 512)
    bias_p = bias_tiles(rel_table, tq=tq, tk=tq, nd=2, dstep=tq)
    bias_s = bias_tiles(rel_table, tq=td, tk=PAGE_SIZE, nd=2, dstep=PAGE_SIZE)

    zc = jnp.zeros((n_even, bp, CONV_W - 1, C_CONV), F32)
    zs = jnp.zeros((n_odd, bp, d), F32)
    zw = jnp.zeros((n_odd, bp, RW_HEADS, RW_HEAD, RW_HEAD), F32)
    y_p, ak_p, av_p, cv_p, sh_p, wk_p = _trunk(x_prompt, xattn_prompt, zc, zs, zw, None, p, bias_p, far)

    n_pool = cache_attn_k.shape[1]
    pool_shape = (n_even * n_pool, PAGE_SIZE, DA_HEADS, 2 * DA_HEAD_DIM)
    pools = (_page_rows(cache_attn_k.reshape(pool_shape)), _page_rows(cache_attn_v.reshape(pool_shape)),
             page_table, n_pool)
    y_s, ak_s, av_s, cv_s, sh_s, wk_s = _trunk(x_sample, xattn_sample, state_conv, state_shift, state_wkv, pools, p,
                                               bias_s, far)

    shape_m = (depth, bp, n_mem, X_HEADS, X_HEAD_DIM)
    return (y_p, y_s, ak_p, av_p, ak_s, av_s, cv_p, cv_s, sh_p, sh_s, wk_p, wk_s,
            jnp.stack(mem_k_layers).reshape(shape_m), jnp.stack(mem_v_layers).reshape(shape_m))
```

```python
import functools
import math

import jax
import jax.numpy as jnp
from jax import lax
from jax.experimental import pallas as pl
from jax.experimental.pallas import tpu as pltpu

F32 = jnp.float32
BF16 = jnp.bfloat16

D_MODEL = 2048
PAGE_SIZE = 128
C_CONV = D_MODEL // 2
CONV_W = 31
DA_HEADS = 4
DA_HEAD_DIM = 128
DA_WIDTH = DA_HEADS * 2 * DA_HEAD_DIM
REL_BUCKETS = 32
REL_MAX_EXACT = REL_BUCKETS // 2
REL_MAX_DIST = 128
RW_HEAD = 64
RW_HEADS = D_MODEL // RW_HEAD
RW_GN_EPS = 64e-5
N_MEM = 256
X_HEADS = 4
X_HEAD_DIM = 128
X_WIDTH = X_HEADS * X_HEAD_DIM
D_FF = 4 * D_MODEL
NORM_EPS = 1e-6
LN_EPS = 1e-5

V7X_VMEM_BYTES = 64 * 1024 * 1024
V7X_VMEM_RESERVE = 6 * 1024 * 1024
LANES = 128
SUBLANES = 8
BF16_SUBLANES = 16
MASK_VALUE = -1e30


def _rel_bucket(n):
    if n < REL_MAX_EXACT:
        return n
    return min(REL_MAX_EXACT + int(math.log(n / REL_MAX_EXACT) / math.log(REL_MAX_DIST / REL_MAX_EXACT)
                                   * (REL_BUCKETS - REL_MAX_EXACT)), REL_BUCKETS - 1)


REL_THRESH = tuple(min(n for n in range(4 * REL_MAX_DIST) if _rel_bucket(n) >= m) for m in range(REL_BUCKETS))


def _cparams(n_grid, vmem_estimate):
    limit = min(max(int(vmem_estimate * 1.2), 32 * 1024 * 1024), V7X_VMEM_BYTES - V7X_VMEM_RESERVE)
    return pltpu.CompilerParams(dimension_semantics=("arbitrary",) * n_grid, vmem_limit_bytes=limit)


def _rms(x, g, eps=NORM_EPS):
    return x * lax.rsqrt(jnp.mean(x * x, axis=-1, keepdims=True) + eps) * g


def _mxu(x):
    return x.astype(BF16) if x.shape[0] % BF16_SUBLANES == 0 else x.astype(F32)


def _dot(a, b):
    return jnp.dot(a, b, preferred_element_type=F32)


def _dot_nt(a, b):
    return lax.dot_general(a, b, (((1,), (1,)), ((), ())), preferred_element_type=F32)


def _dot_tn(a, b):
    return lax.dot_general(a, b, (((0,), (0,)), ((), ())), preferred_element_type=F32)


def _even_in_kernel(x_ref, g_ref, w_ref, glu_ref, q_ref, k_ref, v_ref, u_ref, ga_ref):
    j = pl.program_id(1)

    @pl.when(j == 0)
    def _():
        u_ref[...] = _rms(x_ref[...], g_ref[...]).astype(BF16)

    z = _dot(u_ref[...], w_ref[...])

    @pl.when(j == 0)
    def _():
        ga_ref[...] = z

    @pl.when(j == 1)
    def _():
        glu_ref[...] = ga_ref[...] * jax.nn.sigmoid(z)

    @pl.when(j == 2)
    def _():
        q_ref[...] = z.astype(q_ref.dtype)

    @pl.when(j == 3)
    def _():
        k_ref[...] = z

    @pl.when(j == 4)
    def _():
        v_ref[...] = z


def _stacked(block, index_map, lead):
    return pl.BlockSpec((None,) + tuple(block), lambda *ids: (lead,) + tuple(index_map(*ids)))


def even_in(x, g, w_in, ei, *, tm, q_dtype):
    m, d = x.shape
    c = C_CONV
    assert w_in.shape[1:] == (d, 5 * c) and m % tm == 0
    est = 2 * tm * d * 4 + 2 * d * c * 2 + 2 * tm * c * (4 + 4 + 4 + 4) + tm * d * 2 + tm * c * 4 + 3 * tm * c * 4
    row = lambda i, j: (i, 0)
    return pl.pallas_call(
        _even_in_kernel,
        grid=(m // tm, 5),
        in_specs=[pl.BlockSpec((tm, d), row), pl.BlockSpec((1, d), lambda i, j: (0, 0)),
                  _stacked((d, c), lambda i, j: (0, j), ei)],
        out_specs=[pl.BlockSpec((tm, c), row)] * 4,
        out_shape=[jax.ShapeDtypeStruct((m, c), F32), jax.ShapeDtypeStruct((m, c), q_dtype),
                   jax.ShapeDtypeStruct((m, c), F32), jax.ShapeDtypeStruct((m, c), F32)],
        scratch_shapes=[pltpu.VMEM((tm, d), BF16), pltpu.VMEM((tm, c), F32)],
        compiler_params=_cparams(2, est),
        name="even_in",
    )(x, g.reshape(1, d), w_in)


def _proj_kernel(x_ref, g_ref, w_ref, o_ref, u_ref, *, norm):
    @pl.when(pl.program_id(1) == 0)
    def _():
        x = x_ref[...].astype(F32)
        if norm:
            x = _rms(x, g_ref[...])
        u_ref[...] = x.astype(u_ref.dtype)

    o_ref[...] = _dot(u_ref[...], w_ref[...]).astype(o_ref.dtype)


def proj(x, g, w, w_lead, *, tm, tn, x_lead=None, out_dtype=F32):
    m, kd = x.shape[-2:]
    n = w.shape[-1]
    assert m % tm == 0 and n % tn == 0 and w.shape[1] == kd
    norm = g is not None
    gg = (g if norm else jnp.ones((kd,), F32)).reshape(1, kd)
    est = 2 * tm * kd * x.dtype.itemsize + 2 * kd * tn * 2 + 2 * tm * tn * 4 + tm * kd * 2 + 2 * tm * tn * 4
    x_map = lambda i, j: (i, 0)
    x_spec = pl.BlockSpec((tm, kd), x_map) if x_lead is None else _stacked((tm, kd), x_map, x_lead)
    return pl.pallas_call(
        functools.partial(_proj_kernel, norm=norm),
        grid=(m // tm, n // tn),
        in_specs=[x_spec, pl.BlockSpec((1, kd), lambda i, j: (0, 0)),
                  _stacked((kd, tn), lambda i, j: (0, j), w_lead)],
        out_specs=pl.BlockSpec((tm, tn), lambda i, j: (i, j)),
        out_shape=jax.ShapeDtypeStruct((m, n), out_dtype),
        scratch_shapes=[pltpu.VMEM((tm, kd), BF16)],
        compiler_params=_cparams(2, est),
        name="proj",
    )(x, gg, w)


def _out_resnorm_kernel(*refs, n_in):
    xs = refs[:n_in]
    ws = refs[n_in:2 * n_in]
    h_ref, g_ref, o_ref = refs[2 * n_in:]
    y = _dot(xs[0][...].astype(BF16), ws[0][...])
    for x_ref, w_ref in zip(xs[1:], ws[1:]):
        y = y + _dot(x_ref[...].astype(BF16), w_ref[...])
    o_ref[...] = h_ref[...] + _rms(y, g_ref[...])


def out_resnorm(xs, w, w_lead, h, g, *, tm):
    m, d = h.shape
    n_in = len(xs)
    kd = xs[0].shape[1]
    assert m % tm == 0 and all(x.shape[1] == kd for x in xs) and w.shape[1:] == (n_in * kd, d)
    est = 4 * tm * d * 4 + tm * d * 4
    for x in xs:
        est += 2 * tm * kd * x.dtype.itemsize + 2 * kd * d * 2
    ws = [w] * n_in
    in_specs = [pl.BlockSpec((tm, kd), lambda i: (i, 0)) for x in xs]
    in_specs += [_stacked((kd, d), functools.partial(lambda blk, i: (blk, 0), blk), w_lead) for blk in range(n_in)]
    in_specs += [pl.BlockSpec((tm, d), lambda i: (i, 0)), pl.BlockSpec((1, d), lambda i: (0, 0))]
    return pl.pallas_call(
        functools.partial(_out_resnorm_kernel, n_in=n_in),
        grid=(m // tm,),
        in_specs=in_specs,
        out_specs=pl.BlockSpec((tm, d), lambda i: (i, 0)),
        out_shape=jax.ShapeDtypeStruct((m, d), F32),
        compiler_params=_cparams(1, est),
        name="out_resnorm",
    )(*xs, *ws, h, g.reshape(1, d))


def _mlp_kernel(h_ref, gpre_ref, w1_ref, w2_ref, gpost_ref, o_ref, u_ref, acc_ref):
    f = pl.program_id(1)

    @pl.when(f == 0)
    def _():
        u_ref[...] = _rms(h_ref[...], gpre_ref[...]).astype(BF16)
        acc_ref[...] = jnp.zeros_like(acc_ref)

    t = jnp.maximum(_dot(u_ref[...], w1_ref[...]), 0.0)
    acc_ref[...] += _dot((t * t).astype(BF16), w2_ref[...])

    @pl.when(f == pl.num_programs(1) - 1)
    def _():
        o_ref[...] = h_ref[...] + _rms(acc_ref[...], gpost_ref[...])


def mlp(h, gpre, w1, w2, li, gpost, *, tm, tf):
    m, d = h.shape
    ff = w1.shape[-1]
    assert m % tm == 0 and ff % tf == 0
    est = 4 * tm * d * 4 + 4 * d * tf * 2 + tm * d * (2 + 4) + 3 * tm * tf * 4 + tm * d * 4
    return pl.pallas_call(
        _mlp_kernel,
        grid=(m // tm, ff // tf),
        in_specs=[pl.BlockSpec((tm, d), lambda i, f: (i, 0)), pl.BlockSpec((1, d), lambda i, f: (0, 0)),
                  _stacked((d, tf), lambda i, f: (0, f), li), _stacked((tf, d), lambda i, f: (f, 0), li),
                  pl.BlockSpec((1, d), lambda i, f: (0, 0))],
        out_specs=pl.BlockSpec((tm, d), lambda i, f: (i, 0)),
        out_shape=jax.ShapeDtypeStruct((m, d), F32),
        scratch_shapes=[pltpu.VMEM((tm, d), BF16), pltpu.VMEM((tm, d), F32)],
        compiler_params=_cparams(2, est),
        name="mlp",
    )(h, gpre.reshape(1, d), w1, w2, gpost.reshape(1, d))


CONV_PAD = 32
CONV_ROWS = 32


def _conv_kernel(glu_ref, prev_ref, w_ref, b_ref, lg_ref, lb_ref, out_ref, st_ref, buf_ref):
    i = pl.program_id(1)
    lead = CONV_PAD - (CONV_W - 1)
    bb, tt, c = glu_ref.shape
    rows = min(CONV_ROWS, tt)

    for bi in range(bb):
        @pl.when(i == 0)
        def _():
            buf_ref[bi, 0:lead, :] = jnp.zeros((lead, c), F32)
            buf_ref[bi, lead:CONV_PAD, :] = prev_ref[bi]

        @pl.when(i > 0)
        def _():
            buf_ref[bi, 0:CONV_PAD, :] = buf_ref[bi, tt:tt + CONV_PAD, :]

        buf_ref[bi, CONV_PAD:CONV_PAD + tt, :] = glu_ref[bi]

        for r0 in range(0, tt, rows):
            win = buf_ref[bi, r0:r0 + rows + CONV_PAD, :]
            acc = jnp.zeros((rows, c), F32)
            for phase in range(SUBLANES):
                shifted = win if phase == 0 else pltpu.roll(win, win.shape[0] - phase, axis=0)
                for tap in range(CONV_W):
                    if (lead + tap) % SUBLANES == phase:
                        lo = lead + tap - phase
                        acc = acc + shifted[lo:lo + rows] * w_ref[tap:tap + 1, :]
            y = acc + b_ref[...]
            mu = jnp.mean(y, axis=-1, keepdims=True)
            yc = y - mu
            var = jnp.mean(yc * yc, axis=-1, keepdims=True)
            yn = yc * lax.rsqrt(var + LN_EPS) * lg_ref[...] + lb_ref[...]
            out_ref[bi, r0:r0 + rows, :] = (yn * jax.nn.sigmoid(yn)).astype(out_ref.dtype)

        @pl.when(i == pl.num_programs(1) - 1)
        def _():
            st_ref[bi] = buf_ref[bi, tt + lead:tt + CONV_PAD, :]


def conv_branch(glu, conv_prev, conv_w, conv_b, ln_g, ln_b, *, bb, tt, out_dtype):
    b, t, c = glu.shape
    assert b % bb == 0 and t % tt == 0 and (tt >= CONV_PAD or t == tt)
    est = bb * (4 * tt * c * 4 + 4 * (CONV_W - 1) * c * 4 + (tt + CONV_PAD) * c * 4) + 2 * CONV_W * c * 4
    vec = pl.BlockSpec((1, c), lambda bi, i: (0, 0))
    return pl.pallas_call(
        _conv_kernel,
        grid=(b // bb, t // tt),
        in_specs=[pl.BlockSpec((bb, tt, c), lambda bi, i: (bi, i, 0)),
                  pl.BlockSpec((bb, CONV_W - 1, c), lambda bi, i: (bi, 0, 0)),
                  pl.BlockSpec((CONV_W, c), lambda bi, i: (0, 0)), vec, vec, vec],
        out_specs=[pl.BlockSpec((bb, tt, c), lambda bi, i: (bi, i, 0)),
                   pl.BlockSpec((bb, CONV_W - 1, c), lambda bi, i: (bi, 0, 0))],
        out_shape=[jax.ShapeDtypeStruct((b, t, c), out_dtype), jax.ShapeDtypeStruct((b, CONV_W - 1, c), F32)],
        scratch_shapes=[pltpu.VMEM((bb, tt + CONV_PAD, c), F32)],
        compiler_params=_cparams(2, est),
        name="conv_branch",
    )(glu, conv_prev, conv_w, conv_b.reshape(1, c), ln_g.reshape(1, c), ln_b.reshape(1, c))


def _lambda_kernel(q1_ref, k1_ref, q2_ref, k2_ref, o_ref, *, lam_init):
    s1 = jnp.sum(q1_ref[...] * k1_ref[...], axis=-1, keepdims=True)
    s2 = jnp.sum(q2_ref[...] * k2_ref[...], axis=-1, keepdims=True)
    o_ref[...] = jnp.broadcast_to(jnp.exp(s1) - jnp.exp(s2) + lam_init, o_ref.shape)


def diff_lambda(lq1, lk1, lq2, lk2, lam_init):
    d = lq1.shape[0]
    out = pl.pallas_call(
        functools.partial(_lambda_kernel, lam_init=lam_init),
        out_shape=jax.ShapeDtypeStruct((1, LANES), F32),
        name="diff_lambda",
    )(lq1.reshape(1, d), lk1.reshape(1, d), lq2.reshape(1, d), lk2.reshape(1, d))
    return out[0, :1]


def _bias_kernel(tab_ref, o_ref, *, dstep, q_axis):
    di = pl.program_id(0)
    h = pl.program_id(1)
    shape = o_ref.shape[2:]
    n = (lax.broadcasted_iota(jnp.int32, shape, q_axis) - lax.broadcasted_iota(jnp.int32, shape, 1 - q_axis)
         + di * dstep)
    bias = jnp.full(shape, tab_ref[(REL_BUCKETS - 1) * DA_HEADS + h], F32)
    for m in range(REL_BUCKETS - 2, -1, -1):
        bias = jnp.where(n < REL_THRESH[m + 1], tab_ref[m * DA_HEADS + h], bias)
    o_ref[0, 0] = jnp.where(n < 0, MASK_VALUE, bias)


def bias_tiles(rel_table, *, tq, tk, nd, dstep, keys_first=False):
    shape = (tk, tq) if keys_first else (tq, tk)
    return pl.pallas_call(
        functools.partial(_bias_kernel, dstep=dstep, q_axis=1 if keys_first else 0),
        grid=(nd, DA_HEADS),
        in_specs=[pl.BlockSpec(memory_space=pltpu.SMEM)],
        out_specs=pl.BlockSpec((1, 1) + shape, lambda di, h: (di, h, 0, 0)),
        out_shape=jax.ShapeDtypeStruct((nd, DA_HEADS) + shape, F32),
        compiler_params=_cparams(2, 8 * tq * tk * 4),
        name="bias_tiles",
    )(rel_table.reshape(-1))


def _attn_prompt_kernel(lam_ref, far_ref, q_ref, k_ref, v_ref, bias_ref, sg_ref, o_ref, kb_ref, vt_ref, m_ref, l_ref,
                        acc_ref, *, lam_init):
    h = pl.program_id(1)
    i = pl.program_id(2)
    dh = DA_HEAD_DIM
    nk, tk, _ = kb_ref.shape

    @pl.when(i == 0)
    def _():
        for j in range(nk):
            kb_ref[j] = k_ref[0, j * tk:(j + 1) * tk, :].astype(BF16)
            vt_ref[j] = v_ref[0, j * tk:(j + 1) * tk, :].T.astype(BF16)

    m_ref[...] = jnp.full(m_ref.shape, MASK_VALUE, F32)
    l_ref[...] = jnp.zeros_like(l_ref)
    acc_ref[...] = jnp.zeros_like(acc_ref)
    q = q_ref[0]

    def tile(j, bias):
        kb = kb_ref[j]
        vt = vt_ref[j]
        for c in range(2):
            s = _dot_nt(kb[:, c * dh:(c + 1) * dh], q[:, c * dh:(c + 1) * dh]) * (dh ** -0.5) + bias
            m_prev = m_ref[c]
            m_new = jnp.maximum(m_prev, jnp.max(s, axis=0, keepdims=True))
            alpha = jnp.exp(m_prev - m_new)
            p = jnp.exp(s - m_new)
            l_ref[c] = alpha * l_ref[c] + jnp.sum(p, axis=0, keepdims=True)
            acc_ref[c] = alpha * acc_ref[c] + _dot(vt, p.astype(BF16))
            m_ref[c] = m_new

    def far_tile(j, carry):
        tile(j, far_ref[h])
        return carry

    lax.fori_loop(0, jnp.maximum(i - 1, 0), far_tile, 0)

    @pl.when(i >= 1)
    def _():
        tile(i - 1, bias_ref[1, 0])

    tile(i, bias_ref[0, 0])
    ot = acc_ref[0] / l_ref[0] - lam_ref[0] * (acc_ref[1] / l_ref[1])
    o_ref[0] = (_rms(ot.T, sg_ref[...]) * (1.0 - lam_init)).astype(o_ref.dtype)


def attn_prompt(q, k, v, bias, far, lam, subln_g, *, tq, lam_init):
    b, t, _ = q.shape
    hw = 2 * DA_HEAD_DIM
    nq = t // tq
    est = (4 * tq * hw * 2 + 4 * t * hw * 4 + 2 * t * hw * 2 + 4 * tq * tq * 4 + 2 * tq * hw * 4 + 10 * tq * tq * 4)
    smem = pl.BlockSpec(memory_space=pltpu.SMEM)
    seq = pl.BlockSpec((1, t, hw), lambda bi, h, i: (bi, 0, h))
    return pl.pallas_call(
        functools.partial(_attn_prompt_kernel, lam_init=lam_init),
        grid=(b, DA_HEADS, nq),
        in_specs=[smem, smem, pl.BlockSpec((1, tq, hw), lambda bi, h, i: (bi, i, h)), seq, seq,
                  pl.BlockSpec((2, 1, tq, tq), lambda bi, h, i: (0, h, 0, 0)),
                  pl.BlockSpec((1, hw), lambda bi, h, i: (0, 0))],
        out_specs=pl.BlockSpec((1, tq, hw), lambda bi, h, i: (bi, i, h)),
        out_shape=jax.ShapeDtypeStruct((b, t, DA_WIDTH), BF16),
        scratch_shapes=[pltpu.VMEM((nq, tq, hw), BF16), pltpu.VMEM((nq, hw, tq), BF16),
                        pltpu.VMEM((2, 1, tq), F32), pltpu.VMEM((2, 1, tq), F32), pltpu.VMEM((2, hw, tq), F32)],
        compiler_params=_cparams(3, est),
        name="attn_prompt",
    )(lam, far, q, k, v, bias, subln_g.reshape(1, hw))


PAGES_PER_STEP = 8


PAGE_ROWS = PAGE_SIZE * 2 * DA_HEADS


def _page_rows(pool):
    pg = pool.shape[0]
    x = pool.reshape(pg, PAGE_SIZE, DA_HEADS, 2, DA_HEAD_DIM)
    return jnp.transpose(x, (0, 1, 3, 2, 4)).reshape(pg, PAGE_ROWS, DA_HEAD_DIM)


def _stack_heads(x, half=None):
    dh = DA_HEAD_DIM
    parts = []
    for c in range(2):
        for h in range(DA_HEADS):
            lo = h * 2 * dh + (c if half is None else half) * dh
            parts.append(x[:, lo:lo + dh])
    return jnp.concatenate(parts, axis=0)


def _attn_sample_kernel(pt_ref, lam_ref, q_ref, kn_ref, vn_ref, ptile_ref, ntile_ref, sg_ref, *rest, lam_init):
    pages = rest[:2 * PAGES_PER_STEP]
    o_ref, m_ref, l_ref, acc_ref = rest[2 * PAGES_PER_STEP:]
    s_idx = pl.program_id(1)
    last = pl.num_programs(1) - 1
    dh = DA_HEAD_DIM
    t = q_ref.shape[1]
    nq = DA_HEADS * t
    scale = dh ** -0.5

    @pl.when(s_idx == 0)
    def _():
        m_ref[...] = jnp.full(m_ref.shape, MASK_VALUE, F32)
        l_ref[...] = jnp.zeros_like(l_ref)
        acc_ref[...] = jnp.zeros_like(acc_ref)

    qm = _stack_heads(q_ref[0])

    def update(s, weighted_values):
        m_prev = m_ref[...]
        m_new = jnp.maximum(m_prev, jnp.max(s, axis=-1, keepdims=True))
        alpha = jnp.exp(m_prev - m_new)
        p = jnp.exp(s - m_new)
        l_ref[...] = alpha * l_ref[...] + jnp.sum(p, axis=-1, keepdims=True)
        lo, hi = weighted_values(p)
        acc_ref[:, :dh] = alpha * acc_ref[:, :dh] + lo
        acc_ref[:, dh:] = alpha * acc_ref[:, dh:] + hi
        m_ref[...] = m_new

    scores = []
    for pp in range(PAGES_PER_STEP):
        tile = ptile_ref[0]
        if pp == PAGES_PER_STEP - 1:
            tile = jnp.where(s_idx == last, ptile_ref[1], tile)
        scores.append(_dot_nt(qm, pages[2 * pp][0]) * scale + tile)

    def page_values(p):
        width = p.shape[1]
        top, bot = p[:nq], p[nq:]
        lhs = jnp.concatenate([top, pltpu.roll(bot, width - DA_HEADS, axis=1),
                               pltpu.roll(top, DA_HEADS, axis=1), bot], axis=0)
        res = _dot(lhs[:, :PAGE_ROWS], pages[1][0])
        for pp in range(1, PAGES_PER_STEP):
            res = res + _dot(lhs[:, pp * PAGE_ROWS:(pp + 1) * PAGE_ROWS], pages[2 * pp + 1][0])
        return res[:2 * nq], res[2 * nq:]

    update(jnp.concatenate(scores, axis=1), page_values)

    @pl.when(s_idx == last)
    def _():
        vn = vn_ref[0]
        update(_dot_nt(qm, _stack_heads(kn_ref[0])) * scale + ntile_ref[...],
               lambda p: (_dot(p, _stack_heads(vn, 0)), _dot(p, _stack_heads(vn, 1))))
        o = acc_ref[...] / l_ref[...]
        o = _rms(o[:nq] - lam_ref[0] * o[nq:], sg_ref[...]) * (1.0 - lam_init)
        for h in range(DA_HEADS):
            o_ref[0, :, h * 2 * dh:(h + 1) * 2 * dh] = o[h * t:(h + 1) * t].astype(o_ref.dtype)


def _sample_tiles(bias, far):
    _, nh, t, tk = bias.shape
    c_row = jnp.arange(2).reshape(2, 1, 1, 1, 1, 1)
    h_row = jnp.arange(nh).reshape(1, nh, 1, 1, 1, 1)
    own_page = (c_row == jnp.arange(2).reshape(1, 1, 1, 1, 2, 1)) & (h_row == jnp.arange(nh).reshape(1, 1, 1, 1, 1, nh))
    base = jnp.stack([jnp.broadcast_to(far.reshape(nh, 1, 1), (nh, t, tk)), bias[1]])
    ptile = jnp.where(own_page[None], base[:, None, :, :, :, None, None], MASK_VALUE).reshape(2, 2 * nh * t, tk * 2 * nh)
    own_new = (c_row == jnp.arange(2).reshape(1, 1, 1, 2, 1, 1)) & (h_row == jnp.arange(nh).reshape(1, 1, 1, 1, nh, 1))
    ntile = jnp.where(own_new, bias[0][None, :, :, None, None, :t], MASK_VALUE).reshape(2 * nh * t, 2 * nh * t)
    return ptile, ntile


def attn_sample(q, k_new, v_new, k_rows, v_rows, page_table, bias, far, lam, subln_g, *, lam_init):
    b, t, w = q.shape
    n_pages = page_table.shape[1]
    assert n_pages % PAGES_PER_STEP == 0 and k_rows.shape[1:] == (PAGE_ROWS, DA_HEAD_DIM)
    hw = 2 * DA_HEAD_DIM
    ptile, ntile = _sample_tiles(bias, far)
    smem = pl.BlockSpec(memory_space=pltpu.SMEM)
    row = pl.BlockSpec((1, t, w), lambda bi, s, pt: (bi, 0, 0))

    def page_spec(pp):
        return pl.BlockSpec((1, PAGE_ROWS, DA_HEAD_DIM), lambda bi, s, pt: (pt[bi, s * PAGES_PER_STEP + pp], 0, 0))

    in_specs = [smem, row, row, row,
                pl.BlockSpec(ptile.shape, lambda bi, s, pt: (0, 0, 0)),
                pl.BlockSpec(ntile.shape, lambda bi, s, pt: (0, 0)),
                pl.BlockSpec((1, hw), lambda bi, s, pt: (0, 0))]
    pools = []
    for pp in range(PAGES_PER_STEP):
        in_specs += [page_spec(pp), page_spec(pp)]
        pools += [k_rows, v_rows]
    nr = 2 * DA_HEADS * t
    est = (4 * PAGES_PER_STEP * PAGE_ROWS * DA_HEAD_DIM * 4 + 16 * t * w * 4 + 4 * ptile.size * 4
           + 12 * nr * PAGES_PER_STEP * PAGE_ROWS * 4)
    return pl.pallas_call(
        functools.partial(_attn_sample_kernel, lam_init=lam_init),
        grid_spec=pltpu.PrefetchScalarGridSpec(
            num_scalar_prefetch=1,
            grid=(b, n_pages // PAGES_PER_STEP),
            in_specs=in_specs,
            out_specs=pl.BlockSpec((1, t, w), lambda bi, s, pt: (bi, 0, 0)),
            scratch_shapes=[pltpu.VMEM((nr, 1), F32), pltpu.VMEM((nr, 1), F32), pltpu.VMEM((nr, hw), F32)]),
        out_shape=jax.ShapeDtypeStruct((b, t, w), F32),
        compiler_params=_cparams(2, est),
        name="attn_sample",
    )(page_table, lam, q, k_new, v_new, ptile, ntile, subln_g.reshape(1, hw), *pools)


def _xattn_kernel(q_ref, mk_ref, mv_ref, o_ref):
    q = _mxu(q_ref[0])
    mk = mk_ref[0].astype(q.dtype)
    mv = mv_ref[0].astype(q.dtype)
    dh = X_HEAD_DIM
    for h in range(X_HEADS):
        sl = slice(h * dh, (h + 1) * dh)
        s = _dot_nt(q[:, sl], mk[:, sl]) * (dh ** -0.5)
        p = jnp.exp(s - jnp.max(s, axis=-1, keepdims=True))
        p = p / jnp.sum(p, axis=-1, keepdims=True)
        o_ref[0, :, sl] = _dot(p.astype(q.dtype), mv[:, sl]).astype(o_ref.dtype)


def xattn_core(q, mk, mv, *, tt):
    b, t, w = q.shape
    n_mem = mk.shape[1]
    est = 4 * tt * w * 4 + 4 * n_mem * w * 4 + 6 * tt * n_mem * 4
    return pl.pallas_call(
        _xattn_kernel,
        grid=(b, t // tt),
        in_specs=[pl.BlockSpec((1, tt, w), lambda bi, i: (bi, i, 0)),
                  pl.BlockSpec((1, n_mem, w), lambda bi, i: (bi, 0, 0)),
                  pl.BlockSpec((1, n_mem, w), lambda bi, i: (bi, 0, 0))],
        out_specs=pl.BlockSpec((1, tt, w), lambda bi, i: (bi, i, 0)),
        out_shape=jax.ShapeDtypeStruct((b, t, w), F32),
        compiler_params=_cparams(2, est),
        name="xattn_core",
    )(q, mk, mv)


XATTN_SEQS = 8


def _xattn_rows_kernel(q_ref, mk_ref, mv_ref, mask_ref, o_ref):
    dh = X_HEAD_DIM
    t = q_ref.shape[1]
    for bi in range(q_ref.shape[0]):
        q = q_ref[bi]
        qm = jnp.concatenate([q[:, h * dh:(h + 1) * dh] for h in range(X_HEADS)], axis=0)
        s = _dot_nt(qm, mk_ref[bi]) * (dh ** -0.5) + mask_ref[...]
        p = jnp.exp(s - jnp.max(s, axis=-1, keepdims=True))
        p = p / jnp.sum(p, axis=-1, keepdims=True)
        res = _dot(p, mv_ref[bi])
        for h in range(X_HEADS):
            o_ref[bi, :, h * dh:(h + 1) * dh] = res[h * t:(h + 1) * t]


def xattn_rows(q, mk_rows, mv_rows, li):
    b, t, w = q.shape
    nr = mk_rows.shape[1]
    assert b % XATTN_SEQS == 0 and mk_rows.shape[2] == X_HEAD_DIM
    row_head = jnp.arange(X_HEADS * t).reshape(-1, 1) // t
    col_head = jnp.arange(nr).reshape(1, -1) % X_HEADS
    mask = jnp.where(row_head == col_head, 0.0, MASK_VALUE).astype(F32)
    nb = b // XATTN_SEQS
    mem = pl.BlockSpec((XATTN_SEQS, nr, X_HEAD_DIM), lambda i: (li * nb + i, 0, 0))
    est = 4 * XATTN_SEQS * (nr * X_HEAD_DIM + 2 * t * w) * 4 + 2 * mask.size * 4 + 8 * X_HEADS * t * nr * 4
    return pl.pallas_call(
        _xattn_rows_kernel,
        grid=(nb,),
        in_specs=[pl.BlockSpec((XATTN_SEQS, t, w), lambda i: (i, 0, 0)), mem, mem,
                  pl.BlockSpec(mask.shape, lambda i: (0, 0))],
        out_specs=pl.BlockSpec((XATTN_SEQS, t, w), lambda i: (i, 0, 0)),
        out_shape=jax.ShapeDtypeStruct((b, t, w), F32),
        compiler_params=_cparams(1, est),
        name="xattn_rows",
    )(q, mk_rows, mv_rows, mask)


def _rwkv_mix_kernel(h_ref, g_ref, sp_ref, mix_ref, xs_ref, sh_ref, carry_ref):
    i = pl.program_id(1)
    bb, tt, _ = h_ref.shape

    @pl.when(i == 0)
    def _():
        carry_ref[...] = sp_ref[...]

    for bi in range(bb):
        u = _rms(h_ref[bi], g_ref[...])
        row = lax.broadcasted_iota(jnp.int32, u.shape, 0)
        prev = jnp.where(row == 0, carry_ref[bi], pltpu.roll(u, 1, axis=0))
        xx = prev - u
        for n in range(6):
            xs_ref[n, bi] = (u + xx * mix_ref[n:n + 1, :]).astype(xs_ref.dtype)
        carry_ref[bi] = u[tt - 1:tt, :]
        sh_ref[bi] = u[tt - 1:tt, :]


def rwkv_mix(h, g, shift_prev, mix, *, bb, tt, out_dtype):
    b, t, d = h.shape
    assert b % bb == 0 and t % tt == 0
    est = bb * (2 * tt * d * 4 + 12 * tt * d * 4 + 6 * tt * d * 4)
    return pl.pallas_call(
        _rwkv_mix_kernel,
        grid=(b // bb, t // tt),
        in_specs=[pl.BlockSpec((bb, tt, d), lambda bi, i: (bi, i, 0)), pl.BlockSpec((1, d), lambda bi, i: (0, 0)),
                  pl.BlockSpec((bb, 1, d), lambda bi, i: (bi, 0, 0)), pl.BlockSpec((6, d), lambda bi, i: (0, 0))],
        out_specs=[pl.BlockSpec((6, bb, tt, d), lambda bi, i: (0, bi, i, 0)),
                   pl.BlockSpec((bb, 1, d), lambda bi, i: (bi, 0, 0))],
        out_shape=[jax.ShapeDtypeStruct((6, b, t, d), out_dtype), jax.ShapeDtypeStruct((b, 1, d), F32)],
        scratch_shapes=[pltpu.VMEM((bb, 1, d), F32)],
        compiler_params=_cparams(2, est),
        name="rwkv_mix",
    )(h, g.reshape(1, d), shift_prev.reshape(b, 1, d), mix)


def _lora_kernel(x_ref, w1_ref, w2_ref, b_ref, o_ref, *, act):
    t = _dot(x_ref[...].astype(BF16), w1_ref[...])
    if act == "tanh":
        t = jnp.tanh(t)
    elif act == "sigmoid":
        t = jax.nn.sigmoid(t)
    o_ref[...] = _dot(t.astype(BF16), w2_ref[...]) + b_ref[...]


def lora(x, x_lead, w1, w2, bias, *, act, tm):
    _, m, d = x.shape
    r = w1.shape[1]
    n = w2.shape[1]
    est = 2 * tm * d * x.dtype.itemsize + 4 * d * r * 2 + 4 * r * n * 2 + 4 * tm * n * 4
    return pl.pallas_call(
        functools.partial(_lora_kernel, act=act),
        grid=(m // tm,),
        in_specs=[_stacked((tm, d), lambda i: (i, 0), x_lead), pl.BlockSpec((d, r), lambda i: (0, 0)),
                  pl.BlockSpec((r, n), lambda i: (0, 0)), pl.BlockSpec((1, n), lambda i: (0, 0))],
        out_specs=pl.BlockSpec((tm, n), lambda i: (i, 0)),
        out_shape=jax.ShapeDtypeStruct((m, n), F32),
        compiler_params=_cparams(1, est),
        name="lora",
    )(x, w1, w2, bias.reshape(1, n))


WKV_GROUP = 4
WKV_PROMPT_CHUNKS = 8
WKV_SAMPLE_SEQS = 8


def _cumsum_rows(x):
    n = x.shape[0]
    row = lax.broadcasted_iota(jnp.int32, x.shape, 0)
    step = 1
    while step < n:
        x = x + jnp.where(row >= step, pltpu.roll(x, step, axis=0), 0.0)
        step *= 2
    return x


def _head_ones(width):
    shift = RW_HEAD.bit_length() - 1
    assert RW_HEAD == 1 << shift
    gi = lax.shift_right_logical(lax.broadcasted_iota(jnp.int32, (width, width), 0), shift)
    gj = lax.shift_right_logical(lax.broadcasted_iota(jnp.int32, (width, width), 1), shift)
    return jnp.where(gi == gj, 1.0, 0.0).astype(F32)


def _head_sums(x, head_ones):
    hi = x.astype(BF16).astype(F32)
    return _dot(hi, head_ones) + _dot(x - hi, head_ones)


def _wkv_solve_chunks(chunks, kk_p, ka_p, rk_p, head_ones):
    ln = chunks[0][0].shape[0]
    n = RW_HEAD
    ri = lax.broadcasted_iota(jnp.int32, (ln, 2 * ln), 0)
    ci = lax.broadcasted_iota(jnp.int32, (ln, 2 * ln), 1)
    ci = jnp.where(ci < ln, ci, ci - ln)
    mask_a = ci < ri
    mask_r = ci <= ri
    nlev = max(1, (ln - 1).bit_length())
    zeros = jnp.zeros((ln, n), F32)

    pairs = []
    w_ends = []
    bonuses = []
    for r, kr, v, wp, ap in chunks:
        a = jax.nn.sigmoid(ap)
        lw = -jnp.exp(-jax.nn.softplus(-wp) - 0.5)
        cum = _cumsum_rows(lw)
        cl = cum[ln - 1:ln, :]
        w_in = jnp.exp(cum)
        w_ex = jnp.exp(cum - lw)
        w_inv = jnp.exp(-cum)
        w_rem = jnp.exp(cl - cum)
        w_ends.append(jnp.exp(cl))
        kkv = kr * kk_p
        k = kr * (1.0 + (a - 1.0) * ka_p)
        kkn = kkv / jnp.maximum(jnp.sqrt(_head_sums(kkv * kkv, head_ones)), 1e-12)
        bt = kkn * a
        bonuses.append(_head_sums(r * k * rk_p, head_ones) * v)
        wide = dict(a_til=-kkn * w_ex, r_til=r * w_in, b_til=bt * w_inv, k_til=k * w_inv, bp=bt * w_rem,
                    kp=k * w_rem, v=v)
        for h in range(WKV_GROUP):
            pairs.append({name: z[:, h * n:(h + 1) * n] for name, z in wide.items()})

    for p in pairs:
        p['amat'] = _dot_nt(jnp.concatenate([p['a_til'], p['r_til']], axis=0),
                            jnp.concatenate([p['b_til'], p['k_til']], axis=0))
    for p in pairs:
        p['a_rows'] = jnp.where(mask_a, p['amat'][:ln], 0.0)
        p['r_rows'] = jnp.where(mask_r, p['amat'][ln:], 0.0)
        p['pw'] = p['a_rows'][:, :ln]
        p['akv'] = _dot(p['a_rows'], jnp.concatenate([zeros, p['v']], axis=0))
    eye = jnp.where(lax.broadcasted_iota(jnp.int32, (ln, ln), 0) == lax.broadcasted_iota(jnp.int32, (ln, ln), 1),
                    1.0, 0.0)
    for p in pairs:
        p['inv'] = eye + p['pw']
    for _ in range(1, nlev):
        for p in pairs:
            p['pw'] = _dot(p['pw'], p['pw'])
        for p in pairs:
            p['inv'] = p['inv'] + _dot(p['pw'], p['inv'])
    for p in pairs:
        p['x'] = _dot(p['inv'], jnp.concatenate([p['a_til'], p['akv']], axis=1))
    out = []
    for p in pairs:
        tc = _dot_tn(p['x'], p['bp'])
        vk = _dot_tn(p['v'], p['kp'])
        ry = _dot(p['r_rows'], jnp.concatenate([p['x'], jnp.concatenate([zeros, p['v']], axis=1)], axis=0))
        out.append((tc[:n], tc[n:] + vk, p['r_til'] + ry[:, :n], ry[:, n:]))
    return out, w_ends, bonuses


def _wkv_kernel(r_ref, k_ref, v_ref, w_ref, a_ref, g_ref, s0_ref, kk_ref, ka_ref, rk_ref, lg_ref, lb_ref,
                y_ref, sT_ref, s_scr, tm_scr, cm_scr, rq_scr, yc_scr, we_scr, bon_scr, y_scr, *, ln, nper):
    tb = pl.program_id(2)
    bb, tblk, _ = r_ref.shape
    nck = tblk // ln
    nslot = bb * nck
    n = RW_HEAD

    @pl.when(tb == 0)
    def _():
        s_scr[...] = s0_ref[...]

    kk_p, ka_p, rk_p = kk_ref[...], ka_ref[...], rk_ref[...]
    head_ones = _head_ones(r_ref.shape[2])

    def slot_index(slot):
        if nck == 1:
            return slot, 0
        return 0, slot

    def rows_of(c):
        return pl.ds(c * ln if isinstance(c, int) else pl.multiple_of(c * ln, ln), ln)

    def solve(it, carry):
        chunks = []
        for q in range(nper):
            bi, c = slot_index(it * nper + q)
            rows = rows_of(c)
            chunks.append(tuple(ref[bi, rows, :] for ref in (r_ref, k_ref, v_ref, w_ref, a_ref)))
        mats, w_ends, bonuses = _wkv_solve_chunks(chunks, kk_p, ka_p, rk_p, head_ones)
        for q in range(nper):
            slot = it * nper + q
            we_scr[slot] = w_ends[q]
            bon_scr[slot] = bonuses[q]
            for h in range(WKV_GROUP):
                tmat, cmat, rq, yc = mats[q * WKV_GROUP + h]
                tm_scr[slot, h] = tmat
                cm_scr[slot, h] = cmat
                rq_scr[slot, h] = rq
                yc_scr[slot, h] = yc
        return carry

    lax.fori_loop(0, nslot // nper, solve, 0)

    lg, lb = lg_ref[...], lb_ref[...]

    def advance(c, states):
        new_states = []
        for bi in range(bb):
            slot = bi * nck + c
            w_end = we_scr[slot]
            ys = []
            for h in range(WKV_GROUP):
                s = states[bi * WKV_GROUP + h]
                sl = slice(h * n, (h + 1) * n)
                ys.append(_dot_nt(rq_scr[slot, h], s) + yc_scr[slot, h])
                new_states.append(s * w_end[:, sl] + _dot(s, tm_scr[slot, h]) + cm_scr[slot, h])
            y_scr[slot] = jnp.concatenate(ys, axis=-1)
        return tuple(new_states)

    states = tuple(s_scr[bi, h] for bi in range(bb) for h in range(WKV_GROUP))
    if nck == 1:
        states = advance(0, states)
    else:
        states = lax.fori_loop(0, nck, advance, states)
    for bi in range(bb):
        for h in range(WKV_GROUP):
            s_scr[bi, h] = states[bi * WKV_GROUP + h]

    def finish(slots):
        ys = [y_scr[slot] for slot in slots]
        ycs = [y - _head_sums(y, head_ones) * (1.0 / n) for y in ys]
        vrs = [_head_sums(yc * yc, head_ones) * (1.0 / n) for yc in ycs]
        for slot, yc, var in zip(slots, ycs, vrs):
            bi, c = slot_index(slot)
            rows = rows_of(c)
            yn = yc * lax.rsqrt(var + RW_GN_EPS) * lg + lb + bon_scr[slot]
            y_ref[bi, rows, :] = (yn * g_ref[bi, rows, :]).astype(y_ref.dtype)

    finish(list(range(nslot)))

    @pl.when(tb == pl.num_programs(2) - 1)
    def _():
        sT_ref[...] = s_scr[...]


def wkv(r, k, v, wpre, apre, gate, s0, kk, ka, rk, lnx_g, lnx_b, *, bb, tblk, ln, nper, out_dtype):
    b, t, d = r.shape
    gl = WKV_GROUP * RW_HEAD
    nslot = bb * (tblk // ln)
    assert b % bb == 0 and t % tblk == 0 and tblk % ln == 0 and d % gl == 0 and nslot % nper == 0
    assert bb == 1 or tblk == ln
    seq = pl.BlockSpec((bb, tblk, gl), lambda bi, g, i: (bi, i, g))
    st = pl.BlockSpec((bb, WKV_GROUP, RW_HEAD, RW_HEAD), lambda bi, g, i: (bi, g, 0, 0))
    vec = pl.BlockSpec((1, gl), lambda bi, g, i: (0, g))
    sq = pltpu.VMEM((nslot, WKV_GROUP, RW_HEAD, RW_HEAD), F32)
    tall = pltpu.VMEM((nslot, WKV_GROUP, ln, RW_HEAD), F32)
    est = (14 * bb * tblk * gl * 4 + 5 * bb * gl * RW_HEAD * 4 + 2 * nslot * gl * (RW_HEAD + ln) * 4
           + nslot * (2 * ln + 8) * gl * 4 + 96 * nper * ln * gl * 4)
    return pl.pallas_call(
        functools.partial(_wkv_kernel, ln=ln, nper=nper),
        grid=(b // bb, d // gl, t // tblk),
        in_specs=[seq] * 6 + [st] + [vec] * 5,
        out_specs=[seq, st],
        out_shape=[jax.ShapeDtypeStruct((b, t, d), out_dtype), jax.ShapeDtypeStruct(s0.shape, F32)],
        scratch_shapes=[pltpu.VMEM((bb, WKV_GROUP, RW_HEAD, RW_HEAD), F32), sq, sq, tall, tall,
                        pltpu.VMEM((nslot, 1, gl), F32), pltpu.VMEM((nslot, ln, gl), F32),
                        pltpu.VMEM((nslot, ln, gl), F32)],
        compiler_params=_cparams(3, est),
        name="wkv",
    )(r, k, v, wpre, apre, gate, s0, kk.reshape(1, d), ka.reshape(1, d), rk.reshape(1, d),
      lnx_g.reshape(1, d), lnx_b.reshape(1, d))


def _tile(m, pref):
    return pref if m % pref == 0 else m


def _pad_rank(w1, w2):
    r = w1.shape[1]
    rp = -(-r // LANES) * LANES
    return jnp.pad(w1, ((0, 0), (0, rp - r))), jnp.pad(w2, ((0, rp - r), (0, 0)))


def _trunk(x, xattn, conv_prev, shift_prev, wkv_prev, paged, p, bias, far):
    b, t, d = x.shape
    m = b * t
    tm = _tile(m, 512)
    tseq = _tile(t, 512)
    h = x.reshape(m, d)
    depth = p['norm_g'].shape[0]
    ks, vs, convs, shifts, wkvs = [], [], [], [], []
    for li in range(depth):
        g = p['norm_g'][li]
        if li % 2 == 0:
            ei = li // 2
            lam_init = 0.8 - 0.6 * math.exp(-0.3 * li)
            q_dtype = BF16 if paged is None else F32
            glu, q, k, v = even_in(h, g[0], p['w_in_even'], ei, tm=tm, q_dtype=q_dtype)
            conv_out, conv_state = conv_branch(
                glu.reshape(b, t, C_CONV), conv_prev[ei], p['conv_w'][ei], p['conv_b'][ei], p['conv_ln_g'][ei],
                p['conv_ln_b'][ei], bb=1 if t >= CONV_PAD else 8, tt=_tile(t, 128),
                out_dtype=BF16 if t % BF16_SUBLANES == 0 else F32)
            lam = diff_lambda(p['lambda_q1'][ei], p['lambda_k1'][ei], p['lambda_q2'][ei], p['lambda_k2'][ei], lam_init)
            q3, k3, v3 = (z.reshape(b, t, DA_WIDTH) for z in (q, k, v))
            if paged is None:
                o = attn_prompt(q3, k3, v3, bias, far, lam, p['subln_g'][ei], tq=tseq, lam_init=lam_init)
            else:
                k_pool, v_pool, page_table, n_pool = paged
                o = attn_sample(q3, k3, v3, k_pool, v_pool, page_table + ei * n_pool, bias, far, lam,
                                p['subln_g'][ei], lam_init=lam_init)
            h = out_resnorm([conv_out.reshape(m, C_CONV), o.reshape(m, DA_WIDTH)], p['w_out_even'], ei, h, g[1], tm=tm)
            ks.append(k.reshape(b, t, DA_HEADS, 2 * DA_HEAD_DIM))
            vs.append(v.reshape(b, t, DA_HEADS, 2 * DA_HEAD_DIM))
            convs.append(conv_state)
        else:
            oi = li // 2
            prompt = t >= 64
            xs, sh = rwkv_mix(h.reshape(b, t, d), g[0], shift_prev[oi], p['rw_mix'][oi], bb=1 if prompt else 8,
                              tt=_tile(t, 256), out_dtype=BF16 if prompt else F32)
            xs = xs.reshape(6, m, d)
            r = proj(xs, None, p['rw_wr'], oi, x_lead=0, tm=tm, tn=d)
            k = proj(xs, None, p['rw_wk'], oi, x_lead=2, tm=tm, tn=d)
            v = proj(xs, None, p['rw_wv'], oi, x_lead=3, tm=tm, tn=d)
            wpre = lora(xs, 1, *p['rw_w12'][oi], p['rw_w0'][oi], act="tanh", tm=tm)
            apre = lora(xs, 4, *p['rw_a12'][oi], p['rw_a0'][oi], act=None, tm=tm)
            gate = lora(xs, 5, *p['rw_g12'][oi], jnp.zeros((d,), F32), act="sigmoid", tm=tm)
            seqs = [z.reshape(b, t, d) for z in (r, k, v, wpre, apre, gate)]
            wkv_params = (wkv_prev[oi], p['rw_kk'][oi], p['rw_ka'][oi], p['rw_rk'][oi].reshape(d), p['rw_lnx_g'][oi],
                          p['rw_lnx_b'][oi])
            if prompt:
                yact, s_fin = wkv(*seqs, *wkv_params, bb=1, tblk=tseq, ln=64, nper=WKV_PROMPT_CHUNKS, out_dtype=BF16)
            else:
                yact, s_fin = wkv(*seqs, *wkv_params, bb=8, tblk=t, ln=t, nper=WKV_SAMPLE_SEQS, out_dtype=F32)
            h = out_resnorm([yact.reshape(m, d)], p['rw_wo'], oi, h, g[1], tm=tm)
            shifts.append(sh.reshape(b, d))
            wkvs.append(s_fin)
        qx = proj(h, g[2], p['x_wq'], li, tm=tm, tn=X_WIDTH)
        ox = xattn(li, qx.reshape(b, t, X_WIDTH))
        h = out_resnorm([ox.reshape(m, X_WIDTH)], p['x_wo'], li, h, g[3], tm=tm)
        h = mlp(h, g[4], p['ff_w1'], p['ff_w2'], li, g[5], tm=tm, tf=512)
    return (h.reshape(b, t, d), jnp.stack(ks), jnp.stack(vs), jnp.stack(convs), jnp.stack(shifts), jnp.stack(wkvs))


def kernel(x_prompt, x_sample, cache_attn_k, cache_attn_v, cache_mem_k, cache_mem_v, state_conv, state_shift,
           state_wkv, page_table, mem_prompt, norm_g, mem_norm_g, rel_table, w_in_even, conv_w, conv_b,
           conv_ln_g, conv_ln_b, lambda_q1, lambda_k1, lambda_q2, lambda_k2, subln_g, w_out_even, rw_mix,
           rw_w0, rw_w1, rw_w2, rw_a0, rw_a1, rw_a2, rw_g1, rw_g2, rw_kk, rw_ka, rw_rk, rw_lnx_g, rw_lnx_b,
           rw_wr, rw_wk, rw_wv, rw_wo, x_wq, x_wk, x_wv, x_wo, ff_w1, ff_w2):
    bf = lambda w: w.astype(BF16)
    n_odd = rw_w1.shape[0]
    lora_pairs = lambda w1, w2: [tuple(bf(z) for z in _pad_rank(w1[o], w2[o])) for o in range(n_odd)]
    p = dict(norm_g=norm_g, w_in_even=bf(w_in_even), conv_w=conv_w, conv_b=conv_b, conv_ln_g=conv_ln_g,
             conv_ln_b=conv_ln_b, lambda_q1=lambda_q1, lambda_k1=lambda_k1, lambda_q2=lambda_q2,
             lambda_k2=lambda_k2, subln_g=subln_g, w_out_even=bf(w_out_even), rw_mix=rw_mix, rw_w0=rw_w0,
             rw_w12=lora_pairs(rw_w1, rw_w2), rw_a0=rw_a0, rw_a12=lora_pairs(rw_a1, rw_a2),
             rw_g12=lora_pairs(rw_g1, rw_g2), rw_kk=rw_kk, rw_ka=rw_ka, rw_rk=rw_rk, rw_lnx_g=rw_lnx_g,
             rw_lnx_b=rw_lnx_b, rw_wr=bf(rw_wr), rw_wk=bf(rw_wk), rw_wv=bf(rw_wv), rw_wo=bf(rw_wo), x_wq=bf(x_wq),
             x_wo=bf(x_wo), ff_w1=bf(ff_w1), ff_w2=bf(ff_w2))
    depth = norm_g.shape[0]
    bp, n_mem, d = mem_prompt.shape
    tp = x_prompt.shape[1]
    bd, td, _ = x_sample.shape
    n_even = w_in_even.shape[0]

    mem2d = mem_prompt.reshape(bp * n_mem, d)
    tmem = _tile(bp * n_mem, 512)
    wk_bf, wv_bf = bf(x_wk), bf(x_wv)
    mem_k_layers = [proj(mem2d, mem_norm_g[l], wk_bf, l, tm=tmem, tn=X_WIDTH).reshape(bp, n_mem, X_WIDTH)
                    for l in range(depth)]
    mem_v_layers = [proj(mem2d, mem_norm_g[l], wv_bf, l, tm=tmem, tn=X_WIDTH).reshape(bp, n_mem, X_WIDTH)
                    for l in range(depth)]

    def xattn_prompt(li, qx):
        return xattn_core(qx, mem_k_layers[li], mem_v_layers[li], tt=_tile(tp, 512))

    mem_rows = (depth * bd, n_mem * X_HEADS, X_HEAD_DIM)
    mk_rows, mv_rows = cache_mem_k.reshape(mem_rows), cache_mem_v.reshape(mem_rows)

    def xattn_sample(li, qx):
        return xattn_rows(qx, mk_rows, mv_rows, li)

    far = rel_table[REL_BUCKETS - 1]
    tq = _tile(tp, 512)
    bias_p = bias_tiles(rel_table, tq=tq, tk=tq, nd=2, dstep=tq, keys_first=True)
    bias_s = bias_tiles(rel_table, tq=td, tk=PAGE_SIZE, nd=2, dstep=PAGE_SIZE)

    zc = jnp.zeros((n_even, bp, CONV_W - 1, C_CONV), F32)
    zs = jnp.zeros((n_odd, bp, d), F32)
    zw = jnp.zeros((n_odd, bp, RW_HEADS, RW_HEAD, RW_HEAD), F32)
    y_p, ak_p, av_p, cv_p, sh_p, wk_p = _trunk(x_prompt, xattn_prompt, zc, zs, zw, None, p, bias_p, far)

    n_pool = cache_attn_k.shape[1]
    pool_shape = (n_even * n_pool, PAGE_SIZE, DA_HEADS, 2 * DA_HEAD_DIM)
    pools = (_page_rows(cache_attn_k.reshape(pool_shape)), _page_rows(cache_attn_v.reshape(pool_shape)),
             page_table, n_pool)
    y_s, ak_s, av_s, cv_s, sh_s, wk_s = _trunk(x_sample, xattn_sample, state_conv, state_shift, state_wkv, pools, p,
                                               bias_s, far)

    shape_m = (depth, bp, n_mem, X_HEADS, X_HEAD_DIM)
    return (y_p, y_s, ak_p, av_p, ak_s, av_s, cv_p, cv_s, sh_p, sh_s, wk_p, wk_s,
            jnp.stack(mem_k_layers).reshape(shape_m), jnp.stack(mem_v_layers).reshape(shape_m))
```

```python
import functools
import math

import jax
import jax.numpy as jnp
from jax import lax
from jax.experimental import pallas as pl
from jax.experimental.pallas import tpu as pltpu

F32 = jnp.float32
BF16 = jnp.bfloat16

D_MODEL = 2048
PAGE_SIZE = 128
C_CONV = D_MODEL // 2
CONV_W = 31
DA_HEADS = 4
DA_HEAD_DIM = 128
DA_WIDTH = DA_HEADS * 2 * DA_HEAD_DIM
REL_BUCKETS = 32
REL_MAX_EXACT = REL_BUCKETS // 2
REL_MAX_DIST = 128
RW_HEAD = 64
RW_HEADS = D_MODEL // RW_HEAD
RW_GN_EPS = 64e-5
N_MEM = 256
X_HEADS = 4
X_HEAD_DIM = 128
X_WIDTH = X_HEADS * X_HEAD_DIM
D_FF = 4 * D_MODEL
NORM_EPS = 1e-6
LN_EPS = 1e-5

V7X_VMEM_BYTES = 64 * 1024 * 1024
V7X_VMEM_RESERVE = 6 * 1024 * 1024
LANES = 128
SUBLANES = 8
BF16_SUBLANES = 16
MASK_VALUE = -1e30


def _rel_bucket(n):
    if n < REL_MAX_EXACT:
        return n
    return min(REL_MAX_EXACT + int(math.log(n / REL_MAX_EXACT) / math.log(REL_MAX_DIST / REL_MAX_EXACT)
                                   * (REL_BUCKETS - REL_MAX_EXACT)), REL_BUCKETS - 1)


REL_THRESH = tuple(min(n for n in range(4 * REL_MAX_DIST) if _rel_bucket(n) >= m) for m in range(REL_BUCKETS))


def _cparams(n_grid, vmem_estimate):
    limit = min(max(int(vmem_estimate * 1.2), 32 * 1024 * 1024), V7X_VMEM_BYTES - V7X_VMEM_RESERVE)
    return pltpu.CompilerParams(dimension_semantics=("arbitrary",) * n_grid, vmem_limit_bytes=limit)


def _rms(x, g, eps=NORM_EPS):
    return x * lax.rsqrt(jnp.mean(x * x, axis=-1, keepdims=True) + eps) * g


def _mxu(x):
    return x.astype(BF16) if x.shape[0] % BF16_SUBLANES == 0 else x.astype(F32)


def _dot(a, b):
    return jnp.dot(a, b, preferred_element_type=F32)


def _dot_nt(a, b):
    return lax.dot_general(a, b, (((1,), (1,)), ((), ())), preferred_element_type=F32)


def _dot_tn(a, b):
    return lax.dot_general(a, b, (((0,), (0,)), ((), ())), preferred_element_type=F32)


def _even_in_kernel(x_ref, g_ref, w_ref, glu_ref, q_ref, k_ref, v_ref, u_ref, ga_ref):
    j = pl.program_id(1)

    def z():
        return _dot(u_ref[...], w_ref[...])

    @pl.when(j == 0)
    def _():
        u_ref[...] = _rms(x_ref[...], g_ref[...]).astype(BF16)
        ga_ref[...] = z()

    @pl.when(j == 1)
    def _():
        glu_ref[...] = ga_ref[...] * jax.nn.sigmoid(z())

    @pl.when(j == 2)
    def _():
        q_ref[...] = z().astype(q_ref.dtype)

    @pl.when(j == 3)
    def _():
        k_ref[...] = z()

    @pl.when(j == 4)
    def _():
        v_ref[...] = z()


def _stacked(block, index_map, lead):
    return pl.BlockSpec((None,) + tuple(block), lambda *ids: (lead,) + tuple(index_map(*ids)))


def even_in(x, g, w_in, ei, *, tm, q_dtype):
    m, d = x.shape
    c = C_CONV
    assert w_in.shape[1:] == (d, 5 * c) and m % tm == 0
    est = 2 * tm * d * 4 + 2 * d * c * 2 + 2 * tm * c * (4 + 4 + 4 + 4) + tm * d * 2 + tm * c * 4 + 3 * tm * c * 4
    row = lambda i, j: (i, 0)
    return pl.pallas_call(
        _even_in_kernel,
        grid=(m // tm, 5),
        in_specs=[pl.BlockSpec((tm, d), row), pl.BlockSpec((1, d), lambda i, j: (0, 0)),
                  _stacked((d, c), lambda i, j: (0, j), ei)],
        out_specs=[pl.BlockSpec((tm, c), row)] * 4,
        out_shape=[jax.ShapeDtypeStruct((m, c), F32), jax.ShapeDtypeStruct((m, c), q_dtype),
                   jax.ShapeDtypeStruct((m, c), F32), jax.ShapeDtypeStruct((m, c), F32)],
        scratch_shapes=[pltpu.VMEM((tm, d), BF16), pltpu.VMEM((tm, c), F32)],
        compiler_params=_cparams(2, est),
        name="even_in",
    )(x, g.reshape(1, d), w_in)


def _proj_kernel(x_ref, g_ref, w_ref, o_ref, u_ref, *, norm):
    @pl.when(pl.program_id(1) == 0)
    def _():
        x = x_ref[...].astype(F32)
        if norm:
            x = _rms(x, g_ref[...])
        u_ref[...] = x.astype(u_ref.dtype)

    o_ref[...] = _dot(u_ref[...], w_ref[...]).astype(o_ref.dtype)


def proj(x, g, w, w_lead, *, tm, tn, x_lead=None, out_dtype=F32):
    m, kd = x.shape[-2:]
    n = w.shape[-1]
    assert m % tm == 0 and n % tn == 0 and w.shape[1] == kd
    norm = g is not None
    gg = (g if norm else jnp.ones((kd,), F32)).reshape(1, kd)
    est = 2 * tm * kd * x.dtype.itemsize + 2 * kd * tn * 2 + 2 * tm * tn * 4 + tm * kd * 2 + 2 * tm * tn * 4
    x_map = lambda i, j: (i, 0)
    x_spec = pl.BlockSpec((tm, kd), x_map) if x_lead is None else _stacked((tm, kd), x_map, x_lead)
    return pl.pallas_call(
        functools.partial(_proj_kernel, norm=norm),
        grid=(m // tm, n // tn),
        in_specs=[x_spec, pl.BlockSpec((1, kd), lambda i, j: (0, 0)),
                  _stacked((kd, tn), lambda i, j: (0, j), w_lead)],
        out_specs=pl.BlockSpec((tm, tn), lambda i, j: (i, j)),
        out_shape=jax.ShapeDtypeStruct((m, n), out_dtype),
        scratch_shapes=[pltpu.VMEM((tm, kd), BF16)],
        compiler_params=_cparams(2, est),
        name="proj",
    )(x, gg, w)


def _out_resnorm_kernel(*refs, n_in):
    xs = refs[:n_in]
    ws = refs[n_in:2 * n_in]
    h_ref, g_ref, o_ref = refs[2 * n_in:]
    y = _dot(xs[0][...].astype(BF16), ws[0][...])
    for x_ref, w_ref in zip(xs[1:], ws[1:]):
        y = y + _dot(x_ref[...].astype(BF16), w_ref[...])
    o_ref[...] = h_ref[...] + _rms(y, g_ref[...])


def out_resnorm(xs, w, w_lead, h, g, *, tm):
    m, d = h.shape
    n_in = len(xs)
    kd = xs[0].shape[1]
    assert m % tm == 0 and all(x.shape[1] == kd for x in xs) and w.shape[1:] == (n_in * kd, d)
    est = 4 * tm * d * 4 + tm * d * 4
    for x in xs:
        est += 2 * tm * kd * x.dtype.itemsize + 2 * kd * d * 2
    ws = [w] * n_in
    in_specs = [pl.BlockSpec((tm, kd), lambda i: (i, 0)) for x in xs]
    in_specs += [_stacked((kd, d), functools.partial(lambda blk, i: (blk, 0), blk), w_lead) for blk in range(n_in)]
    in_specs += [pl.BlockSpec((tm, d), lambda i: (i, 0)), pl.BlockSpec((1, d), lambda i: (0, 0))]
    return pl.pallas_call(
        functools.partial(_out_resnorm_kernel, n_in=n_in),
        grid=(m // tm,),
        in_specs=in_specs,
        out_specs=pl.BlockSpec((tm, d), lambda i: (i, 0)),
        out_shape=jax.ShapeDtypeStruct((m, d), F32),
        compiler_params=_cparams(1, est),
        name="out_resnorm",
    )(*xs, *ws, h, g.reshape(1, d))


def _mlp_kernel(h_ref, gpre_ref, w1_ref, w2_ref, gpost_ref, o_ref, u_ref, acc_ref):
    f = pl.program_id(1)

    @pl.when(f == 0)
    def _():
        u_ref[...] = _rms(h_ref[...], gpre_ref[...]).astype(BF16)
        acc_ref[...] = jnp.zeros_like(acc_ref)

    t = jnp.maximum(_dot(u_ref[...], w1_ref[...]), 0.0)
    acc_ref[...] += _dot((t * t).astype(BF16), w2_ref[...])

    @pl.when(f == pl.num_programs(1) - 1)
    def _():
        o_ref[...] = h_ref[...] + _rms(acc_ref[...], gpost_ref[...])


def mlp(h, gpre, w1, w2, li, gpost, *, tm, tf):
    m, d = h.shape
    ff = w1.shape[-1]
    assert m % tm == 0 and ff % tf == 0
    est = 4 * tm * d * 4 + 4 * d * tf * 2 + tm * d * (2 + 4) + 3 * tm * tf * 4 + tm * d * 4
    return pl.pallas_call(
        _mlp_kernel,
        grid=(m // tm, ff // tf),
        in_specs=[pl.BlockSpec((tm, d), lambda i, f: (i, 0)), pl.BlockSpec((1, d), lambda i, f: (0, 0)),
                  _stacked((d, tf), lambda i, f: (0, f), li), _stacked((tf, d), lambda i, f: (f, 0), li),
                  pl.BlockSpec((1, d), lambda i, f: (0, 0))],
        out_specs=pl.BlockSpec((tm, d), lambda i, f: (i, 0)),
        out_shape=jax.ShapeDtypeStruct((m, d), F32),
        scratch_shapes=[pltpu.VMEM((tm, d), BF16), pltpu.VMEM((tm, d), F32)],
        compiler_params=_cparams(2, est),
        name="mlp",
    )(h, gpre.reshape(1, d), w1, w2, gpost.reshape(1, d))


CONV_PAD = 32
CONV_ROWS = 32


def _conv_kernel(glu_ref, prev_ref, w_ref, b_ref, lg_ref, lb_ref, out_ref, st_ref, buf_ref):
    i = pl.program_id(1)
    lead = CONV_PAD - (CONV_W - 1)
    bb, tt, c = glu_ref.shape
    rows = min(CONV_ROWS, tt)

    for bi in range(bb):
        @pl.when(i == 0)
        def _():
            buf_ref[bi, 0:lead, :] = jnp.zeros((lead, c), F32)
            buf_ref[bi, lead:CONV_PAD, :] = prev_ref[bi]

        @pl.when(i > 0)
        def _():
            buf_ref[bi, 0:CONV_PAD, :] = buf_ref[bi, tt:tt + CONV_PAD, :]

        buf_ref[bi, CONV_PAD:CONV_PAD + tt, :] = glu_ref[bi]

        for r0 in range(0, tt, rows):
            win = buf_ref[bi, r0:r0 + rows + CONV_PAD, :]
            acc = jnp.zeros((rows, c), F32)
            for phase in range(SUBLANES):
                shifted = win if phase == 0 else pltpu.roll(win, win.shape[0] - phase, axis=0)
                for tap in range(CONV_W):
                    if (lead + tap) % SUBLANES == phase:
                        lo = lead + tap - phase
                        acc = acc + shifted[lo:lo + rows] * w_ref[tap:tap + 1, :]
            y = acc + b_ref[...]
            mu = jnp.mean(y, axis=-1, keepdims=True)
            yc = y - mu
            var = jnp.mean(yc * yc, axis=-1, keepdims=True)
            yn = yc * lax.rsqrt(var + LN_EPS) * lg_ref[...] + lb_ref[...]
            out_ref[bi, r0:r0 + rows, :] = (yn * jax.nn.sigmoid(yn)).astype(out_ref.dtype)

        @pl.when(i == pl.num_programs(1) - 1)
        def _():
            st_ref[bi] = buf_ref[bi, tt + lead:tt + CONV_PAD, :]


def conv_branch(glu, conv_prev, conv_w, conv_b, ln_g, ln_b, *, bb, tt, out_dtype):
    b, t, c = glu.shape
    assert b % bb == 0 and t % tt == 0 and (tt >= CONV_PAD or t == tt)
    est = bb * (4 * tt * c * 4 + 4 * (CONV_W - 1) * c * 4 + (tt + CONV_PAD) * c * 4) + 2 * CONV_W * c * 4
    vec = pl.BlockSpec((1, c), lambda bi, i: (0, 0))
    return pl.pallas_call(
        _conv_kernel,
        grid=(b // bb, t // tt),
        in_specs=[pl.BlockSpec((bb, tt, c), lambda bi, i: (bi, i, 0)),
                  pl.BlockSpec((bb, CONV_W - 1, c), lambda bi, i: (bi, 0, 0)),
                  pl.BlockSpec((CONV_W, c), lambda bi, i: (0, 0)), vec, vec, vec],
        out_specs=[pl.BlockSpec((bb, tt, c), lambda bi, i: (bi, i, 0)),
                   pl.BlockSpec((bb, CONV_W - 1, c), lambda bi, i: (bi, 0, 0))],
        out_shape=[jax.ShapeDtypeStruct((b, t, c), out_dtype), jax.ShapeDtypeStruct((b, CONV_W - 1, c), F32)],
        scratch_shapes=[pltpu.VMEM((bb, tt + CONV_PAD, c), F32)],
        compiler_params=_cparams(2, est),
        name="conv_branch",
    )(glu, conv_prev, conv_w, conv_b.reshape(1, c), ln_g.reshape(1, c), ln_b.reshape(1, c))


def _lambda_kernel(q1_ref, k1_ref, q2_ref, k2_ref, o_ref, *, lam_init):
    s1 = jnp.sum(q1_ref[...] * k1_ref[...], axis=-1, keepdims=True)
    s2 = jnp.sum(q2_ref[...] * k2_ref[...], axis=-1, keepdims=True)
    o_ref[...] = jnp.broadcast_to(jnp.exp(s1) - jnp.exp(s2) + lam_init, o_ref.shape)


def diff_lambda(lq1, lk1, lq2, lk2, lam_init):
    d = lq1.shape[0]
    out = pl.pallas_call(
        functools.partial(_lambda_kernel, lam_init=lam_init),
        out_shape=jax.ShapeDtypeStruct((1, LANES), F32),
        name="diff_lambda",
    )(lq1.reshape(1, d), lk1.reshape(1, d), lq2.reshape(1, d), lk2.reshape(1, d))
    return out[0, :1]


def _bias_kernel(tab_ref, o_ref, *, dstep, q_axis):
    di = pl.program_id(0)
    h = pl.program_id(1)
    shape = o_ref.shape[2:]
    n = (lax.broadcasted_iota(jnp.int32, shape, q_axis) - lax.broadcasted_iota(jnp.int32, shape, 1 - q_axis)
         + di * dstep)
    bias = jnp.full(shape, tab_ref[(REL_BUCKETS - 1) * DA_HEADS + h], F32)
    for m in range(REL_BUCKETS - 2, -1, -1):
        bias = jnp.where(n < REL_THRESH[m + 1], tab_ref[m * DA_HEADS + h], bias)
    o_ref[0, 0] = jnp.where(n < 0, MASK_VALUE, bias)


def bias_tiles(rel_table, *, tq, tk, nd, dstep, keys_first=False):
    shape = (tk, tq) if keys_first else (tq, tk)
    return pl.pallas_call(
        functools.partial(_bias_kernel, dstep=dstep, q_axis=1 if keys_first else 0),
        grid=(nd, DA_HEADS),
        in_specs=[pl.BlockSpec(memory_space=pltpu.SMEM)],
        out_specs=pl.BlockSpec((1, 1) + shape, lambda di, h: (di, h, 0, 0)),
        out_shape=jax.ShapeDtypeStruct((nd, DA_HEADS) + shape, F32),
        compiler_params=_cparams(2, 8 * tq * tk * 4),
        name="bias_tiles",
    )(rel_table.reshape(-1))


def _attn_prompt_kernel(lam_ref, far_ref, q_ref, k_ref, v_ref, bias_ref, sg_ref, o_ref, kb_ref, vt_ref, m_ref, l_ref,
                        acc_ref, *, lam_init):
    h = pl.program_id(1)
    i = pl.program_id(2)
    dh = DA_HEAD_DIM
    nk, tk, _ = kb_ref.shape

    @pl.when(i == 0)
    def _():
        for j in range(nk):
            kb_ref[j] = k_ref[0, j * tk:(j + 1) * tk, :].astype(BF16)
            vt_ref[j] = v_ref[0, j * tk:(j + 1) * tk, :].T.astype(BF16)

    m_ref[...] = jnp.full(m_ref.shape, MASK_VALUE, F32)
    l_ref[...] = jnp.zeros_like(l_ref)
    acc_ref[...] = jnp.zeros_like(acc_ref)
    q = q_ref[0]

    def tile(j, bias):
        kb = kb_ref[j]
        vt = vt_ref[j]
        for c in range(2):
            s = _dot_nt(kb[:, c * dh:(c + 1) * dh], q[:, c * dh:(c + 1) * dh]) * (dh ** -0.5) + bias
            m_prev = m_ref[c]
            m_new = jnp.maximum(m_prev, jnp.max(s, axis=0, keepdims=True))
            alpha = jnp.exp(m_prev - m_new)
            p = jnp.exp(s - m_new)
            l_ref[c] = alpha * l_ref[c] + jnp.sum(p, axis=0, keepdims=True)
            acc_ref[c] = alpha * acc_ref[c] + _dot(vt, p.astype(BF16))
            m_ref[c] = m_new

    def far_tile(j, carry):
        tile(j, far_ref[h])
        return carry

    lax.fori_loop(0, jnp.maximum(i - 1, 0), far_tile, 0)

    @pl.when(i >= 1)
    def _():
        tile(i - 1, bias_ref[1, 0])

    tile(i, bias_ref[0, 0])
    ot = acc_ref[0] / l_ref[0] - lam_ref[0] * (acc_ref[1] / l_ref[1])
    o_ref[0] = (_rms(ot.T, sg_ref[...]) * (1.0 - lam_init)).astype(o_ref.dtype)


def attn_prompt(q, k, v, bias, far, lam, subln_g, *, tq, lam_init):
    b, t, _ = q.shape
    hw = 2 * DA_HEAD_DIM
    nq = t // tq
    est = (4 * tq * hw * 2 + 4 * t * hw * 4 + 2 * t * hw * 2 + 4 * tq * tq * 4 + 2 * tq * hw * 4 + 10 * tq * tq * 4)
    smem = pl.BlockSpec(memory_space=pltpu.SMEM)
    seq = pl.BlockSpec((1, t, hw), lambda bi, h, i: (bi, 0, h))
    return pl.pallas_call(
        functools.partial(_attn_prompt_kernel, lam_init=lam_init),
        grid=(b, DA_HEADS, nq),
        in_specs=[smem, smem, pl.BlockSpec((1, tq, hw), lambda bi, h, i: (bi, i, h)), seq, seq,
                  pl.BlockSpec((2, 1, tq, tq), lambda bi, h, i: (0, h, 0, 0)),
                  pl.BlockSpec((1, hw), lambda bi, h, i: (0, 0))],
        out_specs=pl.BlockSpec((1, tq, hw), lambda bi, h, i: (bi, i, h)),
        out_shape=jax.ShapeDtypeStruct((b, t, DA_WIDTH), BF16),
        scratch_shapes=[pltpu.VMEM((nq, tq, hw), BF16), pltpu.VMEM((nq, hw, tq), BF16),
                        pltpu.VMEM((2, 1, tq), F32), pltpu.VMEM((2, 1, tq), F32), pltpu.VMEM((2, hw, tq), F32)],
        compiler_params=_cparams(3, est),
        name="attn_prompt",
    )(lam, far, q, k, v, bias, subln_g.reshape(1, hw))


PAGES_PER_STEP = 8


PAGE_ROWS = PAGE_SIZE * 2 * DA_HEADS


def _page_rows(pool):
    pg = pool.shape[0]
    x = pool.reshape(pg, PAGE_SIZE, DA_HEADS, 2, DA_HEAD_DIM)
    return jnp.transpose(x, (0, 1, 3, 2, 4)).reshape(pg, PAGE_ROWS, DA_HEAD_DIM)


def _stack_heads(x, half=None):
    dh = DA_HEAD_DIM
    parts = []
    for c in range(2):
        for h in range(DA_HEADS):
            lo = h * 2 * dh + (c if half is None else half) * dh
            parts.append(x[:, lo:lo + dh])
    return jnp.concatenate(parts, axis=0)


def _attn_sample_kernel(pt_ref, lam_ref, q_ref, kn_ref, vn_ref, ptile_ref, ntile_ref, sg_ref, *rest, lam_init):
    pages = rest[:2 * PAGES_PER_STEP]
    o_ref, m_ref, l_ref, acc_ref = rest[2 * PAGES_PER_STEP:]
    s_idx = pl.program_id(1)
    last = pl.num_programs(1) - 1
    dh = DA_HEAD_DIM
    t = q_ref.shape[1]
    nq = DA_HEADS * t
    scale = dh ** -0.5

    @pl.when(s_idx == 0)
    def _():
        m_ref[...] = jnp.full(m_ref.shape, MASK_VALUE, F32)
        l_ref[...] = jnp.zeros_like(l_ref)
        acc_ref[...] = jnp.zeros_like(acc_ref)

    qm = _stack_heads(q_ref[0])

    def update(s, weighted_values):
        m_prev = m_ref[...]
        m_new = jnp.maximum(m_prev, jnp.max(s, axis=-1, keepdims=True))
        alpha = jnp.exp(m_prev - m_new)
        p = jnp.exp(s - m_new)
        l_ref[...] = alpha * l_ref[...] + jnp.sum(p, axis=-1, keepdims=True)
        lo, hi = weighted_values(p)
        acc_ref[:, :dh] = alpha * acc_ref[:, :dh] + lo
        acc_ref[:, dh:] = alpha * acc_ref[:, dh:] + hi
        m_ref[...] = m_new

    scores = []
    for pp in range(PAGES_PER_STEP):
        tile = ptile_ref[0]
        if pp == PAGES_PER_STEP - 1:
            tile = jnp.where(s_idx == last, ptile_ref[1], tile)
        scores.append(_dot_nt(qm, pages[2 * pp][0]) * scale + tile)

    def page_values(p):
        width = p.shape[1]
        top, bot = p[:nq], p[nq:]
        lhs = jnp.concatenate([top, pltpu.roll(bot, width - DA_HEADS, axis=1),
                               pltpu.roll(top, DA_HEADS, axis=1), bot], axis=0)
        res = _dot(lhs[:, :PAGE_ROWS], pages[1][0])
        for pp in range(1, PAGES_PER_STEP):
            res = res + _dot(lhs[:, pp * PAGE_ROWS:(pp + 1) * PAGE_ROWS], pages[2 * pp + 1][0])
        return res[:2 * nq], res[2 * nq:]

    update(jnp.concatenate(scores, axis=1), page_values)

    @pl.when(s_idx == last)
    def _():
        vn = vn_ref[0]
        update(_dot_nt(qm, _stack_heads(kn_ref[0])) * scale + ntile_ref[...],
               lambda p: (_dot(p, _stack_heads(vn, 0)), _dot(p, _stack_heads(vn, 1))))
        o = acc_ref[...] / l_ref[...]
        o = _rms(o[:nq] - lam_ref[0] * o[nq:], sg_ref[...]) * (1.0 - lam_init)
        for h in range(DA_HEADS):
            o_ref[0, :, h * 2 * dh:(h + 1) * 2 * dh] = o[h * t:(h + 1) * t].astype(o_ref.dtype)


def _sample_tiles(bias, far):
    _, nh, t, tk = bias.shape
    c_row = jnp.arange(2).reshape(2, 1, 1, 1, 1, 1)
    h_row = jnp.arange(nh).reshape(1, nh, 1, 1, 1, 1)
    own_page = (c_row == jnp.arange(2).reshape(1, 1, 1, 1, 2, 1)) & (h_row == jnp.arange(nh).reshape(1, 1, 1, 1, 1, nh))
    base = jnp.stack([jnp.broadcast_to(far.reshape(nh, 1, 1), (nh, t, tk)), bias[1]])
    ptile = jnp.where(own_page[None], base[:, None, :, :, :, None, None], MASK_VALUE).reshape(2, 2 * nh * t, tk * 2 * nh)
    own_new = (c_row == jnp.arange(2).reshape(1, 1, 1, 2, 1, 1)) & (h_row == jnp.arange(nh).reshape(1, 1, 1, 1, nh, 1))
    ntile = jnp.where(own_new, bias[0][None, :, :, None, None, :t], MASK_VALUE).reshape(2 * nh * t, 2 * nh * t)
    return ptile, ntile


def attn_sample(q, k_new, v_new, k_rows, v_rows, page_table, bias, far, lam, subln_g, *, lam_init):
    b, t, w = q.shape
    n_pages = page_table.shape[1]
    assert n_pages % PAGES_PER_STEP == 0 and k_rows.shape[1:] == (PAGE_ROWS, DA_HEAD_DIM)
    hw = 2 * DA_HEAD_DIM
    ptile, ntile = _sample_tiles(bias, far)
    smem = pl.BlockSpec(memory_space=pltpu.SMEM)
    row = pl.BlockSpec((1, t, w), lambda bi, s, pt: (bi, 0, 0))

    def page_spec(pp):
        return pl.BlockSpec((1, PAGE_ROWS, DA_HEAD_DIM), lambda bi, s, pt: (pt[bi, s * PAGES_PER_STEP + pp], 0, 0))

    in_specs = [smem, row, row, row,
                pl.BlockSpec(ptile.shape, lambda bi, s, pt: (0, 0, 0)),
                pl.BlockSpec(ntile.shape, lambda bi, s, pt: (0, 0)),
                pl.BlockSpec((1, hw), lambda bi, s, pt: (0, 0))]
    pools = []
    for pp in range(PAGES_PER_STEP):
        in_specs += [page_spec(pp), page_spec(pp)]
        pools += [k_rows, v_rows]
    nr = 2 * DA_HEADS * t
    est = (4 * PAGES_PER_STEP * PAGE_ROWS * DA_HEAD_DIM * 4 + 16 * t * w * 4 + 4 * ptile.size * 4
           + 12 * nr * PAGES_PER_STEP * PAGE_ROWS * 4)
    return pl.pallas_call(
        functools.partial(_attn_sample_kernel, lam_init=lam_init),
        grid_spec=pltpu.PrefetchScalarGridSpec(
            num_scalar_prefetch=1,
            grid=(b, n_pages // PAGES_PER_STEP),
            in_specs=in_specs,
            out_specs=pl.BlockSpec((1, t, w), lambda bi, s, pt: (bi, 0, 0)),
            scratch_shapes=[pltpu.VMEM((nr, 1), F32), pltpu.VMEM((nr, 1), F32), pltpu.VMEM((nr, hw), F32)]),
        out_shape=jax.ShapeDtypeStruct((b, t, w), F32),
        compiler_params=_cparams(2, est),
        name="attn_sample",
    )(page_table, lam, q, k_new, v_new, ptile, ntile, subln_g.reshape(1, hw), *pools)


def _xattn_kernel(q_ref, mk_ref, mv_ref, o_ref):
    q = _mxu(q_ref[0])
    mk = mk_ref[0].astype(q.dtype)
    mv = mv_ref[0].astype(q.dtype)
    dh = X_HEAD_DIM
    for h in range(X_HEADS):
        sl = slice(h * dh, (h + 1) * dh)
        s = _dot_nt(q[:, sl], mk[:, sl]) * (dh ** -0.5)
        p = jnp.exp(s - jnp.max(s, axis=-1, keepdims=True))
        p = p / jnp.sum(p, axis=-1, keepdims=True)
        o_ref[0, :, sl] = _dot(p.astype(q.dtype), mv[:, sl]).astype(o_ref.dtype)


def xattn_core(q, mk, mv, *, tt):
    b, t, w = q.shape
    n_mem = mk.shape[1]
    est = 4 * tt * w * 4 + 4 * n_mem * w * 4 + 6 * tt * n_mem * 4
    return pl.pallas_call(
        _xattn_kernel,
        grid=(b, t // tt),
        in_specs=[pl.BlockSpec((1, tt, w), lambda bi, i: (bi, i, 0)),
                  pl.BlockSpec((1, n_mem, w), lambda bi, i: (bi, 0, 0)),
                  pl.BlockSpec((1, n_mem, w), lambda bi, i: (bi, 0, 0))],
        out_specs=pl.BlockSpec((1, tt, w), lambda bi, i: (bi, i, 0)),
        out_shape=jax.ShapeDtypeStruct((b, t, w), F32),
        compiler_params=_cparams(2, est),
        name="xattn_core",
    )(q, mk, mv)


XATTN_SEQS = 8


def _xattn_rows_kernel(q_ref, mk_ref, mv_ref, mask_ref, o_ref):
    dh = X_HEAD_DIM
    t = q_ref.shape[1]
    for bi in range(q_ref.shape[0]):
        q = q_ref[bi]
        qm = jnp.concatenate([q[:, h * dh:(h + 1) * dh] for h in range(X_HEADS)], axis=0)
        s = _dot_nt(qm, mk_ref[bi]) * (dh ** -0.5) + mask_ref[...]
        p = jnp.exp(s - jnp.max(s, axis=-1, keepdims=True))
        p = p / jnp.sum(p, axis=-1, keepdims=True)
        res = _dot(p, mv_ref[bi])
        for h in range(X_HEADS):
            o_ref[bi, :, h * dh:(h + 1) * dh] = res[h * t:(h + 1) * t]


def xattn_rows(q, mk_rows, mv_rows, li):
    b, t, w = q.shape
    nr = mk_rows.shape[1]
    assert b % XATTN_SEQS == 0 and mk_rows.shape[2] == X_HEAD_DIM
    row_head = jnp.arange(X_HEADS * t).reshape(-1, 1) // t
    col_head = jnp.arange(nr).reshape(1, -1) % X_HEADS
    mask = jnp.where(row_head == col_head, 0.0, MASK_VALUE).astype(F32)
    nb = b // XATTN_SEQS
    mem = pl.BlockSpec((XATTN_SEQS, nr, X_HEAD_DIM), lambda i: (li * nb + i, 0, 0))
    est = 4 * XATTN_SEQS * (nr * X_HEAD_DIM + 2 * t * w) * 4 + 2 * mask.size * 4 + 8 * X_HEADS * t * nr * 4
    return pl.pallas_call(
        _xattn_rows_kernel,
        grid=(nb,),
        in_specs=[pl.BlockSpec((XATTN_SEQS, t, w), lambda i: (i, 0, 0)), mem, mem,
                  pl.BlockSpec(mask.shape, lambda i: (0, 0))],
        out_specs=pl.BlockSpec((XATTN_SEQS, t, w), lambda i: (i, 0, 0)),
        out_shape=jax.ShapeDtypeStruct((b, t, w), F32),
        compiler_params=_cparams(1, est),
        name="xattn_rows",
    )(q, mk_rows, mv_rows, mask)


RW_MIX_PROJ = (0, 2, 3)
RW_MIX_LORA = ((1, "tanh"), (4, None), (5, "sigmoid"))


def _rwkv_mix_kernel(h_ref, g_ref, sp_ref, mix_ref, *rest):
    loras = [rest[3 * n:3 * n + 3] for n in range(3)]
    xs_ref, lo_refs, sh_ref, carry_ref = rest[9], rest[10:13], rest[13], rest[14]
    i = pl.program_id(1)
    bb, tt, _ = h_ref.shape
    assert (bb * tt) % BF16_SUBLANES == 0

    @pl.when(i == 0)
    def _():
        carry_ref[...] = sp_ref[...]

    lora_in = [[], [], []]
    for bi in range(bb):
        u = _rms(h_ref[bi], g_ref[...])
        row = lax.broadcasted_iota(jnp.int32, u.shape, 0)
        prev = jnp.where(row == 0, carry_ref[bi], pltpu.roll(u, 1, axis=0))
        xx = prev - u
        for slot, n in enumerate(RW_MIX_PROJ):
            xs_ref[slot, bi] = (u + xx * mix_ref[n:n + 1, :]).astype(xs_ref.dtype)
        for slot, (n, _) in enumerate(RW_MIX_LORA):
            lora_in[slot].append((u + xx * mix_ref[n:n + 1, :]).astype(BF16))
        carry_ref[bi] = u[tt - 1:tt, :]
        sh_ref[bi] = u[tt - 1:tt, :]

    for (w1_ref, w2_ref, b_ref), o_ref, xs, (_, act) in zip(loras, lo_refs, lora_in, RW_MIX_LORA):
        mid = _dot(jnp.concatenate(xs, axis=0), w1_ref[...])
        if act == "tanh":
            mid = jnp.tanh(mid)
        elif act == "sigmoid":
            mid = jax.nn.sigmoid(mid)
        out = _dot(mid.astype(BF16), w2_ref[...]) + b_ref[...]
        for bi in range(bb):
            o_ref[bi] = out[bi * tt:(bi + 1) * tt]


def rwkv_mix(h, g, shift_prev, mix, loras, *, bb, tt, out_dtype):
    b, t, d = h.shape
    assert b % bb == 0 and t % tt == 0
    est = bb * (2 * tt * d * 4 + 6 * tt * d * 4 + 6 * tt * d * 4 + 12 * tt * d * 4)
    const = lambda bi, i: (0, 0)
    seq = pl.BlockSpec((bb, tt, d), lambda bi, i: (bi, i, 0))
    lora_args, lora_specs = [], []
    for w1, w2, bias in loras:
        lora_args += [w1, w2, bias.reshape(1, d)]
        lora_specs += [pl.BlockSpec(w1.shape, const), pl.BlockSpec(w2.shape, const), pl.BlockSpec((1, d), const)]
        est += 4 * (w1.size + w2.size) * 2
    return pl.pallas_call(
        _rwkv_mix_kernel,
        grid=(b // bb, t // tt),
        in_specs=[seq, pl.BlockSpec((1, d), const), pl.BlockSpec((bb, 1, d), lambda bi, i: (bi, 0, 0)),
                  pl.BlockSpec((6, d), const)] + lora_specs,
        out_specs=[pl.BlockSpec((3, bb, tt, d), lambda bi, i: (0, bi, i, 0)), seq, seq, seq,
                   pl.BlockSpec((bb, 1, d), lambda bi, i: (bi, 0, 0))],
        out_shape=[jax.ShapeDtypeStruct((3, b, t, d), out_dtype)] + [jax.ShapeDtypeStruct((b, t, d), F32)] * 3
        + [jax.ShapeDtypeStruct((b, 1, d), F32)],
        scratch_shapes=[pltpu.VMEM((bb, 1, d), F32)],
        compiler_params=_cparams(2, est),
        name="rwkv_mix",
    )(h, g.reshape(1, d), shift_prev.reshape(b, 1, d), mix, *lora_args)


WKV_GROUP = 4
WKV_PROMPT_CHUNKS = 8
WKV_SAMPLE_SEQS = 8


def _cumsum_rows(x):
    n = x.shape[0]
    row = lax.broadcasted_iota(jnp.int32, x.shape, 0)
    step = 1
    while step < n:
        x = x + jnp.where(row >= step, pltpu.roll(x, step, axis=0), 0.0)
        step *= 2
    return x


def _head_ones(width):
    shift = RW_HEAD.bit_length() - 1
    assert RW_HEAD == 1 << shift
    gi = lax.shift_right_logical(lax.broadcasted_iota(jnp.int32, (width, width), 0), shift)
    gj = lax.shift_right_logical(lax.broadcasted_iota(jnp.int32, (width, width), 1), shift)
    return jnp.where(gi == gj, 1.0, 0.0).astype(F32)


def _head_sums(x, head_ones):
    hi = x.astype(BF16).astype(F32)
    return _dot(hi, head_ones) + _dot(x - hi, head_ones)


def _wkv_solve_chunks(chunks, kk_p, ka_p, rk_p, head_ones, states=None):
    ln = chunks[0][0].shape[0]
    n = RW_HEAD
    ri = lax.broadcasted_iota(jnp.int32, (ln, 2 * ln), 0)
    ci = lax.broadcasted_iota(jnp.int32, (ln, 2 * ln), 1)
    ci = jnp.where(ci < ln, ci, ci - ln)
    mask_a = ci < ri
    mask_r = ci <= ri
    nlev = max(1, (ln - 1).bit_length())
    zeros = jnp.zeros((ln, n), F32)

    pairs = []
    w_ends = []
    bonuses = []
    for r, kr, v, wp, ap in chunks:
        a = jax.nn.sigmoid(ap)
        lw = -jnp.exp(-jax.nn.softplus(-wp) - 0.5)
        cum = _cumsum_rows(lw)
        cl = cum[ln - 1:ln, :]
        w_in = jnp.exp(cum)
        w_ex = jnp.exp(cum - lw)
        w_inv = jnp.exp(-cum)
        w_rem = jnp.exp(cl - cum)
        w_ends.append(jnp.exp(cl))
        kkv = kr * kk_p
        k = kr * (1.0 + (a - 1.0) * ka_p)
        kkn = kkv / jnp.maximum(jnp.sqrt(_head_sums(kkv * kkv, head_ones)), 1e-12)
        bt = kkn * a
        bonuses.append(_head_sums(r * k * rk_p, head_ones) * v)
        wide = dict(a_til=-kkn * w_ex, r_til=r * w_in, b_til=bt * w_inv, k_til=k * w_inv, bp=bt * w_rem,
                    kp=k * w_rem, v=v)
        for h in range(WKV_GROUP):
            pairs.append({name: z[:, h * n:(h + 1) * n] for name, z in wide.items()})

    for p in pairs:
        p['amat'] = _dot_nt(jnp.concatenate([p['a_til'], p['r_til']], axis=0),
                            jnp.concatenate([p['b_til'], p['k_til']], axis=0))
    for p in pairs:
        p['a_rows'] = jnp.where(mask_a, p['amat'][:ln], 0.0)
        p['r_rows'] = jnp.where(mask_r, p['amat'][ln:], 0.0)
        p['pw'] = p['a_rows'][:, :ln]
        p['akv'] = _dot(p['a_rows'], jnp.concatenate([zeros, p['v']], axis=0))
    eye = jnp.where(lax.broadcasted_iota(jnp.int32, (ln, ln), 0) == lax.broadcasted_iota(jnp.int32, (ln, ln), 1),
                    1.0, 0.0)
    for p in pairs:
        p['inv'] = eye + p['pw']
    for _ in range(1, nlev):
        for p in pairs:
            p['pw'] = _dot(p['pw'], p['pw'])
        for p in pairs:
            p['inv'] = p['inv'] + _dot(p['pw'], p['inv'])
    out = []
    if states is None:
        for p in pairs:
            p['x'] = _dot(p['inv'], jnp.concatenate([p['a_til'], p['akv']], axis=1))
        for p in pairs:
            tc = _dot_tn(p['x'], p['bp'])
            vk = _dot_tn(p['v'], p['kp'])
            ry = _dot(p['r_rows'], jnp.concatenate([p['x'], jnp.concatenate([zeros, p['v']], axis=1)], axis=0))
            out.append((tc[:n], tc[n:] + vk, p['r_til'] + ry[:, :n], ry[:, n:]))
    else:
        for p, s0 in zip(pairs, states):
            ars = _dot_nt(jnp.concatenate([p['a_til'], p['r_til']], axis=0), s0)
            p['as'], p['rs'] = ars[:ln], ars[ln:]
        for p in pairs:
            p['u'] = _dot(p['inv'], p['as'] + p['akv'])
        for p in pairs:
            p['uv'] = jnp.concatenate([p['u'], p['v']], axis=0)
            p['ry'] = _dot(p['r_rows'], p['uv'])
        for p in pairs:
            p['ds'] = _dot_tn(p['uv'], jnp.concatenate([p['bp'], p['kp']], axis=0))
        for q, (p, s0) in enumerate(zip(pairs, states)):
            sl = slice((q % WKV_GROUP) * n, (q % WKV_GROUP + 1) * n)
            out.append((p['rs'] + p['ry'], s0 * w_ends[q // WKV_GROUP][:, sl] + p['ds']))
    return out, w_ends, bonuses


def _wkv_kernel(r_ref, k_ref, v_ref, w_ref, a_ref, g_ref, s0_ref, kk_ref, ka_ref, rk_ref, lg_ref, lb_ref,
                y_ref, sT_ref, s_scr, tm_scr, cm_scr, rq_scr, yc_scr, we_scr, bon_scr, y_scr, *, ln, nper):
    tb = pl.program_id(2)
    bb, tblk, _ = r_ref.shape
    nck = tblk // ln
    nslot = bb * nck
    n = RW_HEAD

    @pl.when(tb == 0)
    def _():
        s_scr[...] = s0_ref[...]

    kk_p, ka_p, rk_p = kk_ref[...], ka_ref[...], rk_ref[...]
    head_ones = _head_ones(r_ref.shape[2])

    def slot_index(slot):
        if nck == 1:
            return slot, 0
        return 0, slot

    def rows_of(c):
        return pl.ds(c * ln if isinstance(c, int) else pl.multiple_of(c * ln, ln), ln)

    def solve(it, carry):
        chunks = []
        for q in range(nper):
            bi, c = slot_index(it * nper + q)
            rows = rows_of(c)
            chunks.append(tuple(ref[bi, rows, :] for ref in (r_ref, k_ref, v_ref, w_ref, a_ref)))
        mats, w_ends, bonuses = _wkv_solve_chunks(chunks, kk_p, ka_p, rk_p, head_ones)
        for q in range(nper):
            slot = it * nper + q
            we_scr[slot] = w_ends[q]
            bon_scr[slot] = bonuses[q]
            for h in range(WKV_GROUP):
                tmat, cmat, rq, yc = mats[q * WKV_GROUP + h]
                tm_scr[slot, h] = tmat
                cm_scr[slot, h] = cmat
                rq_scr[slot, h] = rq
                yc_scr[slot, h] = yc
        return carry

    def solve_from_state(it, carry):
        seqs = [it * nper + q for q in range(nper)]
        states = [s_scr[bi, h] for bi in seqs for h in range(WKV_GROUP)]
        chunks = [tuple(ref[bi] for ref in (r_ref, k_ref, v_ref, w_ref, a_ref)) for bi in seqs]
        res, _, bonuses = _wkv_solve_chunks(chunks, kk_p, ka_p, rk_p, head_ones, states)
        for q, bi in enumerate(seqs):
            bon_scr[bi] = bonuses[q]
            y_scr[bi] = jnp.concatenate([res[q * WKV_GROUP + h][0] for h in range(WKV_GROUP)], axis=-1)
            for h in range(WKV_GROUP):
                s_scr[bi, h] = res[q * WKV_GROUP + h][1]
        return carry

    lg, lb = lg_ref[...], lb_ref[...]

    def advance(c, states):
        new_states = []
        for bi in range(bb):
            slot = bi * nck + c
            w_end = we_scr[slot]
            ys = []
            for h in range(WKV_GROUP):
                s = states[bi * WKV_GROUP + h]
                sl = slice(h * n, (h + 1) * n)
                ys.append(_dot_nt(rq_scr[slot, h], s) + yc_scr[slot, h])
                new_states.append(s * w_end[:, sl] + _dot(s, tm_scr[slot, h]) + cm_scr[slot, h])
            y_scr[slot] = jnp.concatenate(ys, axis=-1)
        return tuple(new_states)

    if nck == 1:
        lax.fori_loop(0, nslot // nper, solve_from_state, 0)
    else:
        lax.fori_loop(0, nslot // nper, solve, 0)
        states = tuple(s_scr[bi, h] for bi in range(bb) for h in range(WKV_GROUP))
        states = lax.fori_loop(0, nck, advance, states)
        for bi in range(bb):
            for h in range(WKV_GROUP):
                s_scr[bi, h] = states[bi * WKV_GROUP + h]

    def finish(slots):
        ys = [y_scr[slot] for slot in slots]
        ycs = [y - _head_sums(y, head_ones) * (1.0 / n) for y in ys]
        vrs = [_head_sums(yc * yc, head_ones) * (1.0 / n) for yc in ycs]
        for slot, yc, var in zip(slots, ycs, vrs):
            bi, c = slot_index(slot)
            rows = rows_of(c)
            yn = yc * lax.rsqrt(var + RW_GN_EPS) * lg + lb + bon_scr[slot]
            y_ref[bi, rows, :] = (yn * g_ref[bi, rows, :]).astype(y_ref.dtype)

    finish(list(range(nslot)))

    @pl.when(tb == pl.num_programs(2) - 1)
    def _():
        sT_ref[...] = s_scr[...]


def wkv(r, k, v, wpre, apre, gate, s0, kk, ka, rk, lnx_g, lnx_b, *, bb, tblk, ln, nper, out_dtype):
    b, t, d = r.shape
    gl = WKV_GROUP * RW_HEAD
    nslot = bb * (tblk // ln)
    assert b % bb == 0 and t % tblk == 0 and tblk % ln == 0 and d % gl == 0 and nslot % nper == 0
    assert bb == 1 or tblk == ln
    seq = pl.BlockSpec((bb, tblk, gl), lambda bi, g, i: (bi, i, g))
    st = pl.BlockSpec((bb, WKV_GROUP, RW_HEAD, RW_HEAD), lambda bi, g, i: (bi, g, 0, 0))
    vec = pl.BlockSpec((1, gl), lambda bi, g, i: (0, g))
    sq = pltpu.VMEM((nslot, WKV_GROUP, RW_HEAD, RW_HEAD), F32)
    tall = pltpu.VMEM((nslot, WKV_GROUP, ln, RW_HEAD), F32)
    est = (14 * bb * tblk * gl * 4 + 5 * bb * gl * RW_HEAD * 4 + 2 * nslot * gl * (RW_HEAD + ln) * 4
           + nslot * (2 * ln + 8) * gl * 4 + 96 * nper * ln * gl * 4)
    return pl.pallas_call(
        functools.partial(_wkv_kernel, ln=ln, nper=nper),
        grid=(b // bb, d // gl, t // tblk),
        in_specs=[seq] * 6 + [st] + [vec] * 5,
        out_specs=[seq, st],
        out_shape=[jax.ShapeDtypeStruct((b, t, d), out_dtype), jax.ShapeDtypeStruct(s0.shape, F32)],
        scratch_shapes=[pltpu.VMEM((bb, WKV_GROUP, RW_HEAD, RW_HEAD), F32), sq, sq, tall, tall,
                        pltpu.VMEM((nslot, 1, gl), F32), pltpu.VMEM((nslot, ln, gl), F32),
                        pltpu.VMEM((nslot, ln, gl), F32)],
        compiler_params=_cparams(3, est),
        name="wkv",
    )(r, k, v, wpre, apre, gate, s0, kk.reshape(1, d), ka.reshape(1, d), rk.reshape(1, d),
      lnx_g.reshape(1, d), lnx_b.reshape(1, d))


def _tile(m, pref):
    return pref if m % pref == 0 else m


def _pad_rank(w1, w2):
    r = w1.shape[1]
    rp = -(-r // LANES) * LANES
    return jnp.pad(w1, ((0, 0), (0, rp - r))), jnp.pad(w2, ((0, rp - r), (0, 0)))


def _trunk(x, xattn, conv_prev, shift_prev, wkv_prev, paged, p, bias, far):
    b, t, d = x.shape
    m = b * t
    tm = _tile(m, 512)
    tseq = _tile(t, 512)
    h = x.reshape(m, d)
    depth = p['norm_g'].shape[0]
    ks, vs, convs, shifts, wkvs = [], [], [], [], []
    for li in range(depth):
        g = p['norm_g'][li]
        if li % 2 == 0:
            ei = li // 2
            lam_init = 0.8 - 0.6 * math.exp(-0.3 * li)
            q_dtype = BF16 if paged is None else F32
            glu, q, k, v = even_in(h, g[0], p['w_in_even'], ei, tm=tm, q_dtype=q_dtype)
            conv_out, conv_state = conv_branch(
                glu.reshape(b, t, C_CONV), conv_prev[ei], p['conv_w'][ei], p['conv_b'][ei], p['conv_ln_g'][ei],
                p['conv_ln_b'][ei], bb=1 if t >= CONV_PAD else 8, tt=_tile(t, 128),
                out_dtype=BF16 if t % BF16_SUBLANES == 0 else F32)
            lam = diff_lambda(p['lambda_q1'][ei], p['lambda_k1'][ei], p['lambda_q2'][ei], p['lambda_k2'][ei], lam_init)
            q3, k3, v3 = (z.reshape(b, t, DA_WIDTH) for z in (q, k, v))
            if paged is None:
                o = attn_prompt(q3, k3, v3, bias, far, lam, p['subln_g'][ei], tq=tseq, lam_init=lam_init)
            else:
                k_pool, v_pool, page_table, n_pool = paged
                o = attn_sample(q3, k3, v3, k_pool, v_pool, page_table + ei * n_pool, bias, far, lam,
                                p['subln_g'][ei], lam_init=lam_init)
            h = out_resnorm([conv_out.reshape(m, C_CONV), o.reshape(m, DA_WIDTH)], p['w_out_even'], ei, h, g[1], tm=tm)
            ks.append(k.reshape(b, t, DA_HEADS, 2 * DA_HEAD_DIM))
            vs.append(v.reshape(b, t, DA_HEADS, 2 * DA_HEAD_DIM))
            convs.append(conv_state)
        else:
            oi = li // 2
            prompt = t >= 64
            loras = [p['rw_w12'][oi] + (p['rw_w0'][oi],), p['rw_a12'][oi] + (p['rw_a0'][oi],),
                     p['rw_g12'][oi] + (jnp.zeros((d,), F32),)]
            xs, wpre, apre, gate, sh = rwkv_mix(
                h.reshape(b, t, d), g[0], shift_prev[oi], p['rw_mix'][oi], loras, bb=1 if prompt else 8,
                tt=_tile(t, 256), out_dtype=BF16 if prompt else F32)
            xs = xs.reshape(3, m, d)
            r = proj(xs, None, p['rw_wr'], oi, x_lead=0, tm=tm, tn=d)
            k = proj(xs, None, p['rw_wk'], oi, x_lead=1, tm=tm, tn=d)
            v = proj(xs, None, p['rw_wv'], oi, x_lead=2, tm=tm, tn=d)
            seqs = [z.reshape(b, t, d) for z in (r, k, v)] + [wpre, apre, gate]
            wkv_params = (wkv_prev[oi], p['rw_kk'][oi], p['rw_ka'][oi], p['rw_rk'][oi].reshape(d), p['rw_lnx_g'][oi],
                          p['rw_lnx_b'][oi])
            if prompt:
                yact, s_fin = wkv(*seqs, *wkv_params, bb=1, tblk=tseq, ln=64, nper=WKV_PROMPT_CHUNKS, out_dtype=BF16)
            else:
                yact, s_fin = wkv(*seqs, *wkv_params, bb=8, tblk=t, ln=t, nper=WKV_SAMPLE_SEQS, out_dtype=F32)
            h = out_resnorm([yact.reshape(m, d)], p['rw_wo'], oi, h, g[1], tm=tm)
            shifts.append(sh.reshape(b, d))
            wkvs.append(s_fin)
        qx = proj(h, g[2], p['x_wq'], li, tm=tm, tn=X_WIDTH)
        ox = xattn(li, qx.reshape(b, t, X_WIDTH))
        h = out_resnorm([ox.reshape(m, X_WIDTH)], p['x_wo'], li, h, g[3], tm=tm)
        h = mlp(h, g[4], p['ff_w1'], p['ff_w2'], li, g[5], tm=tm, tf=512)
    return (h.reshape(b, t, d), jnp.stack(ks), jnp.stack(vs), jnp.stack(convs), jnp.stack(shifts), jnp.stack(wkvs))


def kernel(x_prompt, x_sample, cache_attn_k, cache_attn_v, cache_mem_k, cache_mem_v, state_conv, state_shift,
           state_wkv, page_table, mem_prompt, norm_g, mem_norm_g, rel_table, w_in_even, conv_w, conv_b,
           conv_ln_g, conv_ln_b, lambda_q1, lambda_k1, lambda_q2, lambda_k2, subln_g, w_out_even, rw_mix,
           rw_w0, rw_w1, rw_w2, rw_a0, rw_a1, rw_a2, rw_g1, rw_g2, rw_kk, rw_ka, rw_rk, rw_lnx_g, rw_lnx_b,
           rw_wr, rw_wk, rw_wv, rw_wo, x_wq, x_wk, x_wv, x_wo, ff_w1, ff_w2):
    bf = lambda w: w.astype(BF16)
    n_odd = rw_w1.shape[0]
    lora_pairs = lambda w1, w2: [tuple(bf(z) for z in _pad_rank(w1[o], w2[o])) for o in range(n_odd)]
    p = dict(norm_g=norm_g, w_in_even=bf(w_in_even), conv_w=conv_w, conv_b=conv_b, conv_ln_g=conv_ln_g,
             conv_ln_b=conv_ln_b, lambda_q1=lambda_q1, lambda_k1=lambda_k1, lambda_q2=lambda_q2,
             lambda_k2=lambda_k2, subln_g=subln_g, w_out_even=bf(w_out_even), rw_mix=rw_mix, rw_w0=rw_w0,
             rw_w12=lora_pairs(rw_w1, rw_w2), rw_a0=rw_a0, rw_a12=lora_pairs(rw_a1, rw_a2),
             rw_g12=lora_pairs(rw_g1, rw_g2), rw_kk=rw_kk, rw_ka=rw_ka, rw_rk=rw_rk, rw_lnx_g=rw_lnx_g,
             rw_lnx_b=rw_lnx_b, rw_wr=bf(rw_wr), rw_wk=bf(rw_wk), rw_wv=bf(rw_wv), rw_wo=bf(rw_wo), x_wq=bf(x_wq),
             x_wo=bf(x_wo), ff_w1=bf(ff_w1), ff_w2=bf(ff_w2))
    depth = norm_g.shape[0]
    bp, n_mem, d = mem_prompt.shape
    tp = x_prompt.shape[1]
    bd, td, _ = x_sample.shape
    n_even = w_in_even.shape[0]

    mem2d = mem_prompt.reshape(bp * n_mem, d)
    tmem = _tile(bp * n_mem, 512)
    wk_bf, wv_bf = bf(x_wk), bf(x_wv)
    mem_k_layers = [proj(mem2d, mem_norm_g[l], wk_bf, l, tm=tmem, tn=X_WIDTH).reshape(bp, n_mem, X_WIDTH)
                    for l in range(depth)]
    mem_v_layers = [proj(mem2d, mem_norm_g[l], wv_bf, l, tm=tmem, tn=X_WIDTH).reshape(bp, n_mem, X_WIDTH)
                    for l in range(depth)]

    def xattn_prompt(li, qx):
        return xattn_core(qx, mem_k_layers[li], mem_v_layers[li], tt=_tile(tp, 512))

    mem_rows = (depth * bd, n_mem * X_HEADS, X_HEAD_DIM)
    mk_rows, mv_rows = cache_mem_k.reshape(mem_rows), cache_mem_v.reshape(mem_rows)

    def xattn_sample(li, qx):
        return xattn_rows(qx, mk_rows, mv_rows, li)

    far = rel_table[REL_BUCKETS - 1]
    tq = _tile(tp, 512)
    bias_p = bias_tiles(rel_table, tq=tq, tk=tq, nd=2, dstep=tq, keys_first=True)
    bias_s = bias_tiles(rel_table, tq=td, tk=PAGE_SIZE, nd=2, dstep=PAGE_SIZE)

    zc = jnp.zeros((n_even, bp, CONV_W - 1, C_CONV), F32)
    zs = jnp.zeros((n_odd, bp, d), F32)
    zw = jnp.zeros((n_odd, bp, RW_HEADS, RW_HEAD, RW_HEAD), F32)
    y_p, ak_p, av_p, cv_p, sh_p, wk_p = _trunk(x_prompt, xattn_prompt, zc, zs, zw, None, p, bias_p, far)

    n_pool = cache_attn_k.shape[1]
    pool_shape = (n_even * n_pool, PAGE_SIZE, DA_HEADS, 2 * DA_HEAD_DIM)
    pools = (_page_rows(cache_attn_k.reshape(pool_shape)), _page_rows(cache_attn_v.reshape(pool_shape)),
             page_table, n_pool)
    y_s, ak_s, av_s, cv_s, sh_s, wk_s = _trunk(x_sample, xattn_sample, state_conv, state_shift, state_wkv, pools, p,
                                               bias_s, far)

    shape_m = (depth, bp, n_mem, X_HEADS, X_HEAD_DIM)
    return (y_p, y_s, ak_p, av_p, ak_s, av_s, cv_p, cv_s, sh_p, sh_s, wk_p, wk_s,
            jnp.stack(mem_k_layers).reshape(shape_m), jnp.stack(mem_v_layers).reshape(shape_m))
```

```python
import functools
import math

import jax
import jax.numpy as jnp
from jax import lax
from jax.experimental import pallas as pl
from jax.experimental.pallas import tpu as pltpu

F32 = jnp.float32
BF16 = jnp.bfloat16

D_MODEL = 2048
PAGE_SIZE = 128
C_CONV = D_MODEL // 2
CONV_W = 31
DA_HEADS = 4
DA_HEAD_DIM = 128
DA_WIDTH = DA_HEADS * 2 * DA_HEAD_DIM
REL_BUCKETS = 32
REL_MAX_EXACT = REL_BUCKETS // 2
REL_MAX_DIST = 128
RW_HEAD = 64
RW_HEADS = D_MODEL // RW_HEAD
RW_GN_EPS = 64e-5
N_MEM = 256
X_HEADS = 4
X_HEAD_DIM = 128
X_WIDTH = X_HEADS * X_HEAD_DIM
D_FF = 4 * D_MODEL
NORM_EPS = 1e-6
LN_EPS = 1e-5

V7X_VMEM_BYTES = 64 * 1024 * 1024
V7X_VMEM_RESERVE = 6 * 1024 * 1024
LANES = 128
SUBLANES = 8
BF16_SUBLANES = 16
MASK_VALUE = -1e30


def _rel_bucket(n):
    if n < REL_MAX_EXACT:
        return n
    return min(REL_MAX_EXACT + int(math.log(n / REL_MAX_EXACT) / math.log(REL_MAX_DIST / REL_MAX_EXACT)
                                   * (REL_BUCKETS - REL_MAX_EXACT)), REL_BUCKETS - 1)


REL_THRESH = tuple(min(n for n in range(4 * REL_MAX_DIST) if _rel_bucket(n) >= m) for m in range(REL_BUCKETS))


def _cparams(n_grid, vmem_estimate):
    limit = min(max(int(vmem_estimate * 1.2), 32 * 1024 * 1024), V7X_VMEM_BYTES - V7X_VMEM_RESERVE)
    return pltpu.CompilerParams(dimension_semantics=("arbitrary",) * n_grid, vmem_limit_bytes=limit)


def _rms(x, g, eps=NORM_EPS):
    return x * lax.rsqrt(jnp.mean(x * x, axis=-1, keepdims=True) + eps) * g


def _mxu(x):
    return x.astype(BF16) if x.shape[0] % BF16_SUBLANES == 0 else x.astype(F32)


def _dot(a, b):
    return jnp.dot(a, b, preferred_element_type=F32)


def _dot_nt(a, b):
    return lax.dot_general(a, b, (((1,), (1,)), ((), ())), preferred_element_type=F32)


def _dot_tn(a, b):
    return lax.dot_general(a, b, (((0,), (0,)), ((), ())), preferred_element_type=F32)


def _even_in_kernel(x_ref, g_ref, w_ref, glu_ref, q_ref, k_ref, v_ref, u_ref, ga_ref):
    j = pl.program_id(1)

    def z():
        return _dot(u_ref[...], w_ref[...])

    @pl.when(j == 0)
    def _():
        u_ref[...] = _rms(x_ref[...], g_ref[...]).astype(BF16)
        ga_ref[...] = z()

    @pl.when(j == 1)
    def _():
        glu_ref[...] = ga_ref[...] * jax.nn.sigmoid(z())

    @pl.when(j == 2)
    def _():
        q_ref[...] = z().astype(q_ref.dtype)

    @pl.when(j == 3)
    def _():
        k_ref[...] = z()

    @pl.when(j == 4)
    def _():
        v_ref[...] = z()


def _stacked(block, index_map, lead):
    return pl.BlockSpec((None,) + tuple(block), lambda *ids: (lead,) + tuple(index_map(*ids)))


def even_in(x, g, w_in, ei, *, tm, q_dtype):
    m, d = x.shape
    c = C_CONV
    assert w_in.shape[1:] == (d, 5 * c) and m % tm == 0
    est = 2 * tm * d * 4 + 2 * d * c * 2 + 2 * tm * c * (4 + 4 + 4 + 4) + tm * d * 2 + tm * c * 4 + 3 * tm * c * 4
    row = lambda i, j: (i, 0)
    return pl.pallas_call(
        _even_in_kernel,
        grid=(m // tm, 5),
        in_specs=[pl.BlockSpec((tm, d), row), pl.BlockSpec((1, d), lambda i, j: (0, 0)),
                  _stacked((d, c), lambda i, j: (0, j), ei)],
        out_specs=[pl.BlockSpec((tm, c), row)] * 4,
        out_shape=[jax.ShapeDtypeStruct((m, c), F32), jax.ShapeDtypeStruct((m, c), q_dtype),
                   jax.ShapeDtypeStruct((m, c), F32), jax.ShapeDtypeStruct((m, c), F32)],
        scratch_shapes=[pltpu.VMEM((tm, d), BF16), pltpu.VMEM((tm, c), F32)],
        compiler_params=_cparams(2, est),
        name="even_in",
    )(x, g.reshape(1, d), w_in)


def _proj_kernel(x_ref, g_ref, w_ref, o_ref, u_ref, *, norm):
    @pl.when(pl.program_id(1) == 0)
    def _():
        x = x_ref[...].astype(F32)
        if norm:
            x = _rms(x, g_ref[...])
        u_ref[...] = x.astype(u_ref.dtype)

    o_ref[...] = _dot(u_ref[...], w_ref[...]).astype(o_ref.dtype)


def proj(x, g, w, w_lead, *, tm, tn, x_lead=None, out_dtype=F32):
    m, kd = x.shape[-2:]
    n = w.shape[-1]
    assert m % tm == 0 and n % tn == 0 and w.shape[1] == kd
    norm = g is not None
    gg = (g if norm else jnp.ones((kd,), F32)).reshape(1, kd)
    est = 2 * tm * kd * x.dtype.itemsize + 2 * kd * tn * 2 + 2 * tm * tn * 4 + tm * kd * 2 + 2 * tm * tn * 4
    x_map = lambda i, j: (i, 0)
    x_spec = pl.BlockSpec((tm, kd), x_map) if x_lead is None else _stacked((tm, kd), x_map, x_lead)
    return pl.pallas_call(
        functools.partial(_proj_kernel, norm=norm),
        grid=(m // tm, n // tn),
        in_specs=[x_spec, pl.BlockSpec((1, kd), lambda i, j: (0, 0)),
                  _stacked((kd, tn), lambda i, j: (0, j), w_lead)],
        out_specs=pl.BlockSpec((tm, tn), lambda i, j: (i, j)),
        out_shape=jax.ShapeDtypeStruct((m, n), out_dtype),
        scratch_shapes=[pltpu.VMEM((tm, kd), BF16)],
        compiler_params=_cparams(2, est),
        name="proj",
    )(x, gg, w)


def _out_resnorm_kernel(*refs, n_in):
    xs = refs[:n_in]
    ws = refs[n_in:2 * n_in]
    h_ref, g_ref, o_ref = refs[2 * n_in:]
    y = _dot(xs[0][...].astype(BF16), ws[0][...])
    for x_ref, w_ref in zip(xs[1:], ws[1:]):
        y = y + _dot(x_ref[...].astype(BF16), w_ref[...])
    o_ref[...] = h_ref[...] + _rms(y, g_ref[...])


def out_resnorm(xs, w, w_lead, h, g, *, tm):
    m, d = h.shape
    n_in = len(xs)
    kd = xs[0].shape[1]
    assert m % tm == 0 and all(x.shape[1] == kd for x in xs) and w.shape[1:] == (n_in * kd, d)
    est = 4 * tm * d * 4 + tm * d * 4
    for x in xs:
        est += 2 * tm * kd * x.dtype.itemsize + 2 * kd * d * 2
    ws = [w] * n_in
    in_specs = [pl.BlockSpec((tm, kd), lambda i: (i, 0)) for x in xs]
    in_specs += [_stacked((kd, d), functools.partial(lambda blk, i: (blk, 0), blk), w_lead) for blk in range(n_in)]
    in_specs += [pl.BlockSpec((tm, d), lambda i: (i, 0)), pl.BlockSpec((1, d), lambda i: (0, 0))]
    return pl.pallas_call(
        functools.partial(_out_resnorm_kernel, n_in=n_in),
        grid=(m // tm,),
        in_specs=in_specs,
        out_specs=pl.BlockSpec((tm, d), lambda i: (i, 0)),
        out_shape=jax.ShapeDtypeStruct((m, d), F32),
        compiler_params=_cparams(1, est),
        name="out_resnorm",
    )(*xs, *ws, h, g.reshape(1, d))


def _mlp_kernel(h_ref, gpre_ref, w1_ref, w2_ref, gpost_ref, o_ref, *rest):
    w_out, (u_ref, acc_ref) = rest[:-2], rest[-2:]
    f = pl.program_id(1)

    @pl.when(f == 0)
    def _():
        u_ref[...] = _rms(h_ref[...], gpre_ref[...]).astype(BF16)
        acc_ref[...] = jnp.zeros_like(acc_ref)

    w1 = w1_ref[...].astype(BF16)
    w2 = w2_ref[...].astype(BF16)
    if w_out:
        w_out[0][...] = w1
        w_out[1][...] = w2
    t = jnp.maximum(_dot(u_ref[...], w1), 0.0)
    acc_ref[...] += _dot((t * t).astype(BF16), w2)

    @pl.when(f == pl.num_programs(1) - 1)
    def _():
        o_ref[...] = h_ref[...] + _rms(acc_ref[...], gpost_ref[...])


def mlp(h, gpre, w1, w2, li, gpost, *, tm, tf):
    m, d = h.shape
    ff = w1.shape[-1]
    assert m % tm == 0 and ff % tf == 0
    emit = w1.dtype != BF16
    wb = w1.dtype.itemsize
    row_map = lambda i, f: (i, 0)
    if emit:
        assert tm == m
        rows = pl.BlockSpec((tm, d), row_map, pipeline_mode=pl.Buffered(1))
        est = 2 * tm * d * 4 + 4 * d * tf * (wb + 2) + tm * d * (2 + 4) + 3 * tm * tf * 4 + tm * d * 4
        out_specs = [rows, pl.BlockSpec((d, tf), lambda i, f: (0, f)), pl.BlockSpec((tf, d), lambda i, f: (f, 0))]
        out_shape = [jax.ShapeDtypeStruct((m, d), F32), jax.ShapeDtypeStruct((d, ff), BF16),
                     jax.ShapeDtypeStruct((ff, d), BF16)]
    else:
        rows = pl.BlockSpec((tm, d), row_map)
        est = 4 * tm * d * 4 + 4 * d * tf * wb + tm * d * (2 + 4) + 3 * tm * tf * 4 + tm * d * 4
        out_specs = [rows]
        out_shape = [jax.ShapeDtypeStruct((m, d), F32)]
    out = pl.pallas_call(
        _mlp_kernel,
        grid=(m // tm, ff // tf),
        in_specs=[rows, pl.BlockSpec((1, d), lambda i, f: (0, 0)),
                  _stacked((d, tf), lambda i, f: (0, f), li), _stacked((tf, d), lambda i, f: (f, 0), li),
                  pl.BlockSpec((1, d), lambda i, f: (0, 0))],
        out_specs=out_specs,
        out_shape=out_shape,
        scratch_shapes=[pltpu.VMEM((tm, d), BF16), pltpu.VMEM((tm, d), F32)],
        compiler_params=_cparams(2, est),
        name="mlp",
    )(h, gpre.reshape(1, d), w1, w2, gpost.reshape(1, d))
    return out if emit else out[0]


CONV_PAD = 32
CONV_ROWS = 32


def _conv_kernel(glu_ref, prev_ref, w_ref, b_ref, lg_ref, lb_ref, out_ref, st_ref, buf_ref):
    i = pl.program_id(1)
    lead = CONV_PAD - (CONV_W - 1)
    bb, tt, c = glu_ref.shape
    rows = min(CONV_ROWS, tt)

    for bi in range(bb):
        @pl.when(i == 0)
        def _():
            buf_ref[bi, 0:lead, :] = jnp.zeros((lead, c), F32)
            buf_ref[bi, lead:CONV_PAD, :] = prev_ref[bi]

        @pl.when(i > 0)
        def _():
            buf_ref[bi, 0:CONV_PAD, :] = buf_ref[bi, tt:tt + CONV_PAD, :]

        buf_ref[bi, CONV_PAD:CONV_PAD + tt, :] = glu_ref[bi]

        for r0 in range(0, tt, rows):
            win = buf_ref[bi, r0:r0 + rows + CONV_PAD, :]
            acc = jnp.zeros((rows, c), F32)
            for phase in range(SUBLANES):
                shifted = win if phase == 0 else pltpu.roll(win, win.shape[0] - phase, axis=0)
                for tap in range(CONV_W):
                    if (lead + tap) % SUBLANES == phase:
                        lo = lead + tap - phase
                        acc = acc + shifted[lo:lo + rows] * w_ref[tap:tap + 1, :]
            y = acc + b_ref[...]
            mu = jnp.mean(y, axis=-1, keepdims=True)
            yc = y - mu
            var = jnp.mean(yc * yc, axis=-1, keepdims=True)
            yn = yc * lax.rsqrt(var + LN_EPS) * lg_ref[...] + lb_ref[...]
            out_ref[bi, r0:r0 + rows, :] = (yn * jax.nn.sigmoid(yn)).astype(out_ref.dtype)

        @pl.when(i == pl.num_programs(1) - 1)
        def _():
            st_ref[bi] = buf_ref[bi, tt + lead:tt + CONV_PAD, :]


def conv_branch(glu, conv_prev, conv_w, conv_b, ln_g, ln_b, *, bb, tt, out_dtype):
    b, t, c = glu.shape
    assert b % bb == 0 and t % tt == 0 and (tt >= CONV_PAD or t == tt)
    est = bb * (4 * tt * c * 4 + 4 * (CONV_W - 1) * c * 4 + (tt + CONV_PAD) * c * 4) + 2 * CONV_W * c * 4
    vec = pl.BlockSpec((1, c), lambda bi, i: (0, 0))
    return pl.pallas_call(
        _conv_kernel,
        grid=(b // bb, t // tt),
        in_specs=[pl.BlockSpec((bb, tt, c), lambda bi, i: (bi, i, 0)),
                  pl.BlockSpec((bb, CONV_W - 1, c), lambda bi, i: (bi, 0, 0)),
                  pl.BlockSpec((CONV_W, c), lambda bi, i: (0, 0)), vec, vec, vec],
        out_specs=[pl.BlockSpec((bb, tt, c), lambda bi, i: (bi, i, 0)),
                   pl.BlockSpec((bb, CONV_W - 1, c), lambda bi, i: (bi, 0, 0))],
        out_shape=[jax.ShapeDtypeStruct((b, t, c), out_dtype), jax.ShapeDtypeStruct((b, CONV_W - 1, c), F32)],
        scratch_shapes=[pltpu.VMEM((bb, tt + CONV_PAD, c), F32)],
        compiler_params=_cparams(2, est),
        name="conv_branch",
    )(glu, conv_prev, conv_w, conv_b.reshape(1, c), ln_g.reshape(1, c), ln_b.reshape(1, c))


def _lambda_kernel(q1_ref, k1_ref, q2_ref, k2_ref, o_ref, *, lam_init):
    s1 = jnp.sum(q1_ref[...] * k1_ref[...], axis=-1, keepdims=True)
    s2 = jnp.sum(q2_ref[...] * k2_ref[...], axis=-1, keepdims=True)
    o_ref[...] = jnp.broadcast_to(jnp.exp(s1) - jnp.exp(s2) + lam_init, o_ref.shape)


def diff_lambda(lq1, lk1, lq2, lk2, lam_init):
    d = lq1.shape[0]
    out = pl.pallas_call(
        functools.partial(_lambda_kernel, lam_init=lam_init),
        out_shape=jax.ShapeDtypeStruct((1, LANES), F32),
        name="diff_lambda",
    )(lq1.reshape(1, d), lk1.reshape(1, d), lq2.reshape(1, d), lk2.reshape(1, d))
    return out[0, :1]


def _bias_kernel(tab_ref, o_ref, *, dstep, q_axis):
    di = pl.program_id(0)
    h = pl.program_id(1)
    shape = o_ref.shape[2:]
    n = (lax.broadcasted_iota(jnp.int32, shape, q_axis) - lax.broadcasted_iota(jnp.int32, shape, 1 - q_axis)
         + di * dstep)
    bias = jnp.full(shape, tab_ref[(REL_BUCKETS - 1) * DA_HEADS + h], F32)
    for m in range(REL_BUCKETS - 2, -1, -1):
        bias = jnp.where(n < REL_THRESH[m + 1], tab_ref[m * DA_HEADS + h], bias)
    o_ref[0, 0] = jnp.where(n < 0, MASK_VALUE, bias)


def bias_tiles(rel_table, *, tq, tk, nd, dstep, keys_first=False):
    shape = (tk, tq) if keys_first else (tq, tk)
    return pl.pallas_call(
        functools.partial(_bias_kernel, dstep=dstep, q_axis=1 if keys_first else 0),
        grid=(nd, DA_HEADS),
        in_specs=[pl.BlockSpec(memory_space=pltpu.SMEM)],
        out_specs=pl.BlockSpec((1, 1) + shape, lambda di, h: (di, h, 0, 0)),
        out_shape=jax.ShapeDtypeStruct((nd, DA_HEADS) + shape, F32),
        compiler_params=_cparams(2, 8 * tq * tk * 4),
        name="bias_tiles",
    )(rel_table.reshape(-1))


def _attn_prompt_kernel(lam_ref, far_ref, q_ref, k_ref, v_ref, bias_ref, sg_ref, o_ref, kb_ref, vt_ref, m_ref, l_ref,
                        acc_ref, *, lam_init):
    h = pl.program_id(1)
    i = pl.program_id(2)
    dh = DA_HEAD_DIM
    nk, tk, _ = kb_ref.shape

    @pl.when(i == 0)
    def _():
        for j in range(nk):
            kb_ref[j] = k_ref[0, j * tk:(j + 1) * tk, :].astype(BF16)
            vt_ref[j] = v_ref[0, j * tk:(j + 1) * tk, :].T.astype(BF16)

    m_ref[...] = jnp.full(m_ref.shape, MASK_VALUE, F32)
    l_ref[...] = jnp.zeros_like(l_ref)
    acc_ref[...] = jnp.zeros_like(acc_ref)
    q = q_ref[0]

    def tile(j, bias):
        kb = kb_ref[j]
        vt = vt_ref[j]
        for c in range(2):
            s = _dot_nt(kb[:, c * dh:(c + 1) * dh], q[:, c * dh:(c + 1) * dh]) * (dh ** -0.5) + bias
            m_prev = m_ref[c]
            m_new = jnp.maximum(m_prev, jnp.max(s, axis=0, keepdims=True))
            alpha = jnp.exp(m_prev - m_new)
            p = jnp.exp(s - m_new)
            l_ref[c] = alpha * l_ref[c] + jnp.sum(p, axis=0, keepdims=True)
            acc_ref[c] = alpha * acc_ref[c] + _dot(vt, p.astype(BF16))
            m_ref[c] = m_new

    def far_tile(j, carry):
        tile(j, far_ref[h])
        return carry

    lax.fori_loop(0, jnp.maximum(i - 1, 0), far_tile, 0)

    @pl.when(i >= 1)
    def _():
        tile(i - 1, bias_ref[1, 0])

    tile(i, bias_ref[0, 0])
    ot = acc_ref[0] / l_ref[0] - lam_ref[0] * (acc_ref[1] / l_ref[1])
    o_ref[0] = (_rms(ot.T, sg_ref[...]) * (1.0 - lam_init)).astype(o_ref.dtype)


def attn_prompt(q, k, v, bias, far, lam, subln_g, *, tq, lam_init):
    b, t, _ = q.shape
    hw = 2 * DA_HEAD_DIM
    nq = t // tq
    est = (4 * tq * hw * 2 + 4 * t * hw * 4 + 2 * t * hw * 2 + 4 * tq * tq * 4 + 2 * tq * hw * 4 + 10 * tq * tq * 4)
    smem = pl.BlockSpec(memory_space=pltpu.SMEM)
    seq = pl.BlockSpec((1, t, hw), lambda bi, h, i: (bi, 0, h))
    return pl.pallas_call(
        functools.partial(_attn_prompt_kernel, lam_init=lam_init),
        grid=(b, DA_HEADS, nq),
        in_specs=[smem, smem, pl.BlockSpec((1, tq, hw), lambda bi, h, i: (bi, i, h)), seq, seq,
                  pl.BlockSpec((2, 1, tq, tq), lambda bi, h, i: (0, h, 0, 0)),
                  pl.BlockSpec((1, hw), lambda bi, h, i: (0, 0))],
        out_specs=pl.BlockSpec((1, tq, hw), lambda bi, h, i: (bi, i, h)),
        out_shape=jax.ShapeDtypeStruct((b, t, DA_WIDTH), BF16),
        scratch_shapes=[pltpu.VMEM((nq, tq, hw), BF16), pltpu.VMEM((nq, hw, tq), BF16),
                        pltpu.VMEM((2, 1, tq), F32), pltpu.VMEM((2, 1, tq), F32), pltpu.VMEM((2, hw, tq), F32)],
        compiler_params=_cparams(3, est),
        name="attn_prompt",
    )(lam, far, q, k, v, bias, subln_g.reshape(1, hw))


PAGES_PER_STEP = 8


PAGE_ROWS = PAGE_SIZE * 2 * DA_HEADS


def _page_rows(pool):
    pg = pool.shape[0]
    x = pool.reshape(pg, PAGE_SIZE, DA_HEADS, 2, DA_HEAD_DIM)
    return jnp.transpose(x, (0, 1, 3, 2, 4)).reshape(pg, PAGE_ROWS, DA_HEAD_DIM)


def _stack_heads(x, half=None):
    dh = DA_HEAD_DIM
    parts = []
    for c in range(2):
        for h in range(DA_HEADS):
            lo = h * 2 * dh + (c if half is None else half) * dh
            parts.append(x[:, lo:lo + dh])
    return jnp.concatenate(parts, axis=0)


def _attn_sample_kernel(pt_ref, lam_ref, q_ref, kn_ref, vn_ref, ptile_ref, ntile_ref, sg_ref, *rest, lam_init):
    pages = rest[:2 * PAGES_PER_STEP]
    o_ref, m_ref, l_ref, acc_ref = rest[2 * PAGES_PER_STEP:]
    s_idx = pl.program_id(1)
    last = pl.num_programs(1) - 1
    dh = DA_HEAD_DIM
    t = q_ref.shape[1]
    nq = DA_HEADS * t
    scale = dh ** -0.5

    @pl.when(s_idx == 0)
    def _():
        m_ref[...] = jnp.full(m_ref.shape, MASK_VALUE, F32)
        l_ref[...] = jnp.zeros_like(l_ref)
        acc_ref[...] = jnp.zeros_like(acc_ref)

    qm = _stack_heads(q_ref[0])

    def update(s, weighted_values):
        m_prev = m_ref[...]
        m_new = jnp.maximum(m_prev, jnp.max(s, axis=-1, keepdims=True))
        alpha = jnp.exp(m_prev - m_new)
        p = jnp.exp(s - m_new)
        l_ref[...] = alpha * l_ref[...] + jnp.sum(p, axis=-1, keepdims=True)
        lo, hi = weighted_values(p)
        acc_ref[:, :dh] = alpha * acc_ref[:, :dh] + lo
        acc_ref[:, dh:] = alpha * acc_ref[:, dh:] + hi
        m_ref[...] = m_new

    scores = []
    for pp in range(PAGES_PER_STEP):
        tile = ptile_ref[0]
        if pp == PAGES_PER_STEP - 1:
            tile = jnp.where(s_idx == last, ptile_ref[1], tile)
        scores.append(_dot_nt(qm, pages[2 * pp][0]) * scale + tile)

    def page_values(p):
        width = p.shape[1]
        top, bot = p[:nq], p[nq:]
        lhs = jnp.concatenate([top, pltpu.roll(bot, width - DA_HEADS, axis=1),
                               pltpu.roll(top, DA_HEADS, axis=1), bot], axis=0)
        res = _dot(lhs[:, :PAGE_ROWS], pages[1][0])
        for pp in range(1, PAGES_PER_STEP):
            res = res + _dot(lhs[:, pp * PAGE_ROWS:(pp + 1) * PAGE_ROWS], pages[2 * pp + 1][0])
        return res[:2 * nq], res[2 * nq:]

    update(jnp.concatenate(scores, axis=1), page_values)

    @pl.when(s_idx == last)
    def _():
        vn = vn_ref[0]
        update(_dot_nt(qm, _stack_heads(kn_ref[0])) * scale + ntile_ref[...],
               lambda p: (_dot(p, _stack_heads(vn, 0)), _dot(p, _stack_heads(vn, 1))))
        o = acc_ref[...] / l_ref[...]
        o = _rms(o[:nq] - lam_ref[0] * o[nq:], sg_ref[...]) * (1.0 - lam_init)
        for h in range(DA_HEADS):
            o_ref[0, :, h * 2 * dh:(h + 1) * 2 * dh] = o[h * t:(h + 1) * t].astype(o_ref.dtype)


def _sample_tiles(bias, far):
    _, nh, t, tk = bias.shape
    c_row = jnp.arange(2).reshape(2, 1, 1, 1, 1, 1)
    h_row = jnp.arange(nh).reshape(1, nh, 1, 1, 1, 1)
    own_page = (c_row == jnp.arange(2).reshape(1, 1, 1, 1, 2, 1)) & (h_row == jnp.arange(nh).reshape(1, 1, 1, 1, 1, nh))
    base = jnp.stack([jnp.broadcast_to(far.reshape(nh, 1, 1), (nh, t, tk)), bias[1]])
    ptile = jnp.where(own_page[None], base[:, None, :, :, :, None, None], MASK_VALUE).reshape(2, 2 * nh * t, tk * 2 * nh)
    own_new = (c_row == jnp.arange(2).reshape(1, 1, 1, 2, 1, 1)) & (h_row == jnp.arange(nh).reshape(1, 1, 1, 1, nh, 1))
    ntile = jnp.where(own_new, bias[0][None, :, :, None, None, :t], MASK_VALUE).reshape(2 * nh * t, 2 * nh * t)
    return ptile, ntile


def attn_sample(q, k_new, v_new, k_rows, v_rows, page_table, bias, far, lam, subln_g, *, lam_init):
    b, t, w = q.shape
    n_pages = page_table.shape[1]
    assert n_pages % PAGES_PER_STEP == 0 and k_rows.shape[1:] == (PAGE_ROWS, DA_HEAD_DIM)
    hw = 2 * DA_HEAD_DIM
    ptile, ntile = _sample_tiles(bias, far)
    smem = pl.BlockSpec(memory_space=pltpu.SMEM)
    row = pl.BlockSpec((1, t, w), lambda bi, s, pt: (bi, 0, 0))

    def page_spec(pp):
        return pl.BlockSpec((1, PAGE_ROWS, DA_HEAD_DIM), lambda bi, s, pt: (pt[bi, s * PAGES_PER_STEP + pp], 0, 0))

    in_specs = [smem, row, row, row,
                pl.BlockSpec(ptile.shape, lambda bi, s, pt: (0, 0, 0)),
                pl.BlockSpec(ntile.shape, lambda bi, s, pt: (0, 0)),
                pl.BlockSpec((1, hw), lambda bi, s, pt: (0, 0))]
    pools = []
    for pp in range(PAGES_PER_STEP):
        in_specs += [page_spec(pp), page_spec(pp)]
        pools += [k_rows, v_rows]
    nr = 2 * DA_HEADS * t
    est = (4 * PAGES_PER_STEP * PAGE_ROWS * DA_HEAD_DIM * 4 + 16 * t * w * 4 + 4 * ptile.size * 4
           + 12 * nr * PAGES_PER_STEP * PAGE_ROWS * 4)
    return pl.pallas_call(
        functools.partial(_attn_sample_kernel, lam_init=lam_init),
        grid_spec=pltpu.PrefetchScalarGridSpec(
            num_scalar_prefetch=1,
            grid=(b, n_pages // PAGES_PER_STEP),
            in_specs=in_specs,
            out_specs=pl.BlockSpec((1, t, w), lambda bi, s, pt: (bi, 0, 0)),
            scratch_shapes=[pltpu.VMEM((nr, 1), F32), pltpu.VMEM((nr, 1), F32), pltpu.VMEM((nr, hw), F32)]),
        out_shape=jax.ShapeDtypeStruct((b, t, w), F32),
        compiler_params=_cparams(2, est),
        name="attn_sample",
    )(page_table, lam, q, k_new, v_new, ptile, ntile, subln_g.reshape(1, hw), *pools)


def _xattn_kernel(q_ref, mk_ref, mv_ref, o_ref):
    q = _mxu(q_ref[0])
    mk = mk_ref[0].astype(q.dtype)
    mv = mv_ref[0].astype(q.dtype)
    dh = X_HEAD_DIM
    for h in range(X_HEADS):
        sl = slice(h * dh, (h + 1) * dh)
        s = _dot_nt(q[:, sl], mk[:, sl]) * (dh ** -0.5)
        p = jnp.exp(s - jnp.max(s, axis=-1, keepdims=True))
        p = p / jnp.sum(p, axis=-1, keepdims=True)
        o_ref[0, :, sl] = _dot(p.astype(q.dtype), mv[:, sl]).astype(o_ref.dtype)


def xattn_core(q, mk, mv, *, tt):
    b, t, w = q.shape
    n_mem = mk.shape[1]
    est = 4 * tt * w * 4 + 4 * n_mem * w * 4 + 6 * tt * n_mem * 4
    return pl.pallas_call(
        _xattn_kernel,
        grid=(b, t // tt),
        in_specs=[pl.BlockSpec((1, tt, w), lambda bi, i: (bi, i, 0)),
                  pl.BlockSpec((1, n_mem, w), lambda bi, i: (bi, 0, 0)),
                  pl.BlockSpec((1, n_mem, w), lambda bi, i: (bi, 0, 0))],
        out_specs=pl.BlockSpec((1, tt, w), lambda bi, i: (bi, i, 0)),
        out_shape=jax.ShapeDtypeStruct((b, t, w), F32),
        compiler_params=_cparams(2, est),
        name="xattn_core",
    )(q, mk, mv)


XATTN_SEQS = 8


def _xattn_rows_kernel(q_ref, mk_ref, mv_ref, mask_ref, o_ref):
    dh = X_HEAD_DIM
    t = q_ref.shape[1]
    for bi in range(q_ref.shape[0]):
        q = q_ref[bi]
        qm = jnp.concatenate([q[:, h * dh:(h + 1) * dh] for h in range(X_HEADS)], axis=0)
        s = _dot_nt(qm, mk_ref[bi]) * (dh ** -0.5) + mask_ref[...]
        p = jnp.exp(s - jnp.max(s, axis=-1, keepdims=True))
        p = p / jnp.sum(p, axis=-1, keepdims=True)
        res = _dot(p, mv_ref[bi])
        for h in range(X_HEADS):
            o_ref[bi, :, h * dh:(h + 1) * dh] = res[h * t:(h + 1) * t]


def xattn_rows(q, mk_rows, mv_rows, li):
    b, t, w = q.shape
    nr = mk_rows.shape[1]
    assert b % XATTN_SEQS == 0 and mk_rows.shape[2] == X_HEAD_DIM
    row_head = jnp.arange(X_HEADS * t).reshape(-1, 1) // t
    col_head = jnp.arange(nr).reshape(1, -1) % X_HEADS
    mask = jnp.where(row_head == col_head, 0.0, MASK_VALUE).astype(F32)
    nb = b // XATTN_SEQS
    mem = pl.BlockSpec((XATTN_SEQS, nr, X_HEAD_DIM), lambda i: (li * nb + i, 0, 0))
    est = 4 * XATTN_SEQS * (nr * X_HEAD_DIM + 2 * t * w) * 4 + 2 * mask.size * 4 + 8 * X_HEADS * t * nr * 4
    return pl.pallas_call(
        _xattn_rows_kernel,
        grid=(nb,),
        in_specs=[pl.BlockSpec((XATTN_SEQS, t, w), lambda i: (i, 0, 0)), mem, mem,
                  pl.BlockSpec(mask.shape, lambda i: (0, 0))],
        out_specs=pl.BlockSpec((XATTN_SEQS, t, w), lambda i: (i, 0, 0)),
        out_shape=jax.ShapeDtypeStruct((b, t, w), F32),
        compiler_params=_cparams(1, est),
        name="xattn_rows",
    )(q, mk_rows, mv_rows, mask)


RW_MIX_PROJ = (0, 2, 3)
RW_MIX_LORA = ((1, "tanh"), (4, None), (5, "sigmoid"))


def _rwkv_mix_kernel(h_ref, g_ref, sp_ref, mix_ref, *rest):
    loras = [rest[3 * n:3 * n + 3] for n in range(3)]
    xs_ref, lo_refs, sh_ref, carry_ref = rest[9], rest[10:13], rest[13], rest[14]
    i = pl.program_id(1)
    bb, tt, _ = h_ref.shape
    assert (bb * tt) % BF16_SUBLANES == 0

    @pl.when(i == 0)
    def _():
        carry_ref[...] = sp_ref[...]

    lora_in = [[], [], []]
    for bi in range(bb):
        u = _rms(h_ref[bi], g_ref[...])
        row = lax.broadcasted_iota(jnp.int32, u.shape, 0)
        prev = jnp.where(row == 0, carry_ref[bi], pltpu.roll(u, 1, axis=0))
        xx = prev - u
        for slot, n in enumerate(RW_MIX_PROJ):
            xs_ref[slot, bi] = (u + xx * mix_ref[n:n + 1, :]).astype(xs_ref.dtype)
        for slot, (n, _) in enumerate(RW_MIX_LORA):
            lora_in[slot].append((u + xx * mix_ref[n:n + 1, :]).astype(BF16))
        carry_ref[bi] = u[tt - 1:tt, :]
        sh_ref[bi] = u[tt - 1:tt, :]

    for (w1_ref, w2_ref, b_ref), o_ref, xs, (_, act) in zip(loras, lo_refs, lora_in, RW_MIX_LORA):
        mid = _dot(jnp.concatenate(xs, axis=0), w1_ref[...])
        if act == "tanh":
            mid = jnp.tanh(mid)
        elif act == "sigmoid":
            mid = jax.nn.sigmoid(mid)
        out = _dot(mid.astype(BF16), w2_ref[...]) + b_ref[...]
        for bi in range(bb):
            o_ref[bi] = out[bi * tt:(bi + 1) * tt]


def rwkv_mix(h, g, shift_prev, mix, loras, *, bb, tt, out_dtype):
    b, t, d = h.shape
    assert b % bb == 0 and t % tt == 0
    est = bb * (2 * tt * d * 4 + 6 * tt * d * 4 + 6 * tt * d * 4 + 12 * tt * d * 4)
    const = lambda bi, i: (0, 0)
    seq = pl.BlockSpec((bb, tt, d), lambda bi, i: (bi, i, 0))
    lora_args, lora_specs = [], []
    for w1, w2, bias in loras:
        lora_args += [w1, w2, bias.reshape(1, d)]
        lora_specs += [pl.BlockSpec(w1.shape, const), pl.BlockSpec(w2.shape, const), pl.BlockSpec((1, d), const)]
        est += 4 * (w1.size + w2.size) * 2
    return pl.pallas_call(
        _rwkv_mix_kernel,
        grid=(b // bb, t // tt),
        in_specs=[seq, pl.BlockSpec((1, d), const), pl.BlockSpec((bb, 1, d), lambda bi, i: (bi, 0, 0)),
                  pl.BlockSpec((6, d), const)] + lora_specs,
        out_specs=[pl.BlockSpec((3, bb, tt, d), lambda bi, i: (0, bi, i, 0)), seq, seq, seq,
                   pl.BlockSpec((bb, 1, d), lambda bi, i: (bi, 0, 0))],
        out_shape=[jax.ShapeDtypeStruct((3, b, t, d), out_dtype)] + [jax.ShapeDtypeStruct((b, t, d), F32)] * 3
        + [jax.ShapeDtypeStruct((b, 1, d), F32)],
        scratch_shapes=[pltpu.VMEM((bb, 1, d), F32)],
        compiler_params=_cparams(2, est),
        name="rwkv_mix",
    )(h, g.reshape(1, d), shift_prev.reshape(b, 1, d), mix, *lora_args)


WKV_GROUP = 4
WKV_PROMPT_CHUNKS = 8
WKV_SAMPLE_SEQS = 8


def _cumsum_rows(x):
    n = x.shape[0]
    row = lax.broadcasted_iota(jnp.int32, x.shape, 0)
    step = 1
    while step < n:
        x = x + jnp.where(row >= step, pltpu.roll(x, step, axis=0), 0.0)
        step *= 2
    return x


def _head_ones(width):
    shift = RW_HEAD.bit_length() - 1
    assert RW_HEAD == 1 << shift
    gi = lax.shift_right_logical(lax.broadcasted_iota(jnp.int32, (width, width), 0), shift)
    gj = lax.shift_right_logical(lax.broadcasted_iota(jnp.int32, (width, width), 1), shift)
    return jnp.where(gi == gj, 1.0, 0.0).astype(F32)


def _head_sums(x, head_ones):
    hi = x.astype(BF16).astype(F32)
    return _dot(hi, head_ones) + _dot(x - hi, head_ones)


def _wkv_solve_chunks(chunks, kk_p, ka_p, rk_p, head_ones, states=None):
    ln = chunks[0][0].shape[0]
    n = RW_HEAD
    ri = lax.broadcasted_iota(jnp.int32, (ln, 2 * ln), 0)
    ci = lax.broadcasted_iota(jnp.int32, (ln, 2 * ln), 1)
    ci = jnp.where(ci < ln, ci, ci - ln)
    mask_a = ci < ri
    mask_r = ci <= ri
    nlev = max(1, (ln - 1).bit_length())
    zeros = jnp.zeros((ln, n), F32)

    pairs = []
    w_ends = []
    bonuses = []
    for r, kr, v, wp, ap in chunks:
        a = jax.nn.sigmoid(ap)
        lw = -jnp.exp(-jax.nn.softplus(-wp) - 0.5)
        cum = _cumsum_rows(lw)
        cl = cum[ln - 1:ln, :]
        w_in = jnp.exp(cum)
        w_ex = jnp.exp(cum - lw)
        w_inv = jnp.exp(-cum)
        w_rem = jnp.exp(cl - cum)
        w_ends.append(jnp.exp(cl))
        kkv = kr * kk_p
        k = kr * (1.0 + (a - 1.0) * ka_p)
        kkn = kkv / jnp.maximum(jnp.sqrt(_head_sums(kkv * kkv, head_ones)), 1e-12)
        bt = kkn * a
        bonuses.append(_head_sums(r * k * rk_p, head_ones) * v)
        wide = dict(a_til=-kkn * w_ex, r_til=r * w_in, b_til=bt * w_inv, k_til=k * w_inv, bp=bt * w_rem,
                    kp=k * w_rem, v=v)
        for h in range(WKV_GROUP):
            pairs.append({name: z[:, h * n:(h + 1) * n] for name, z in wide.items()})

    for p in pairs:
        p['amat'] = _dot_nt(jnp.concatenate([p['a_til'], p['r_til']], axis=0),
                            jnp.concatenate([p['b_til'], p['k_til']], axis=0))
    for p in pairs:
        p['a_rows'] = jnp.where(mask_a, p['amat'][:ln], 0.0)
        p['r_rows'] = jnp.where(mask_r, p['amat'][ln:], 0.0)
        p['pw'] = p['a_rows'][:, :ln]
        p['akv'] = _dot(p['a_rows'], jnp.concatenate([zeros, p['v']], axis=0))
    eye = jnp.where(lax.broadcasted_iota(jnp.int32, (ln, ln), 0) == lax.broadcasted_iota(jnp.int32, (ln, ln), 1),
                    1.0, 0.0)
    for p in pairs:
        p['inv'] = eye + p['pw']
    for _ in range(1, nlev):
        for p in pairs:
            p['pw'] = _dot(p['pw'], p['pw'])
        for p in pairs:
            p['inv'] = p['inv'] + _dot(p['pw'], p['inv'])
    out = []
    if states is None:
        for p in pairs:
            p['x'] = _dot(p['inv'], jnp.concatenate([p['a_til'], p['akv']], axis=1))
        for p in pairs:
            tc = _dot_tn(p['x'], p['bp'])
            vk = _dot_tn(p['v'], p['kp'])
            ry = _dot(p['r_rows'], jnp.concatenate([p['x'], jnp.concatenate([zeros, p['v']], axis=1)], axis=0))
            out.append((tc[:n], tc[n:] + vk, p['r_til'] + ry[:, :n], ry[:, n:]))
    else:
        for p, s0 in zip(pairs, states):
            ars = _dot_nt(jnp.concatenate([p['a_til'], p['r_til']], axis=0), s0)
            p['as'], p['rs'] = ars[:ln], ars[ln:]
        for p in pairs:
            p['u'] = _dot(p['inv'], p['as'] + p['akv'])
        for p in pairs:
            p['uv'] = jnp.concatenate([p['u'], p['v']], axis=0)
            p['ry'] = _dot(p['r_rows'], p['uv'])
        for p in pairs:
            p['ds'] = _dot_tn(p['uv'], jnp.concatenate([p['bp'], p['kp']], axis=0))
        for q, (p, s0) in enumerate(zip(pairs, states)):
            sl = slice((q % WKV_GROUP) * n, (q % WKV_GROUP + 1) * n)
            out.append((p['rs'] + p['ry'], s0 * w_ends[q // WKV_GROUP][:, sl] + p['ds']))
    return out, w_ends, bonuses


def _wkv_kernel(r_ref, k_ref, v_ref, w_ref, a_ref, g_ref, s0_ref, kk_ref, ka_ref, rk_ref, lg_ref, lb_ref,
                y_ref, sT_ref, s_scr, tm_scr, cm_scr, rq_scr, yc_scr, we_scr, bon_scr, y_scr, *, ln, nper):
    tb = pl.program_id(2)
    bb, tblk, _ = r_ref.shape
    nck = tblk // ln
    nslot = bb * nck
    n = RW_HEAD

    @pl.when(tb == 0)
    def _():
        s_scr[...] = s0_ref[...]

    kk_p, ka_p, rk_p = kk_ref[...], ka_ref[...], rk_ref[...]
    head_ones = _head_ones(r_ref.shape[2])

    def slot_index(slot):
        if nck == 1:
            return slot, 0
        return 0, slot

    def rows_of(c):
        return pl.ds(c * ln if isinstance(c, int) else pl.multiple_of(c * ln, ln), ln)

    def solve(it, carry):
        chunks = []
        for q in range(nper):
            bi, c = slot_index(it * nper + q)
            rows = rows_of(c)
            chunks.append(tuple(ref[bi, rows, :] for ref in (r_ref, k_ref, v_ref, w_ref, a_ref)))
        mats, w_ends, bonuses = _wkv_solve_chunks(chunks, kk_p, ka_p, rk_p, head_ones)
        for q in range(nper):
            slot = it * nper + q
            we_scr[slot] = w_ends[q]
            bon_scr[slot] = bonuses[q]
            for h in range(WKV_GROUP):
                tmat, cmat, rq, yc = mats[q * WKV_GROUP + h]
                tm_scr[slot, h] = tmat
                cm_scr[slot, h] = cmat
                rq_scr[slot, h] = rq
                yc_scr[slot, h] = yc
        return carry

    def solve_from_state(it, carry):
        seqs = [it * nper + q for q in range(nper)]
        states = [s_scr[bi, h] for bi in seqs for h in range(WKV_GROUP)]
        chunks = [tuple(ref[bi] for ref in (r_ref, k_ref, v_ref, w_ref, a_ref)) for bi in seqs]
        res, _, bonuses = _wkv_solve_chunks(chunks, kk_p, ka_p, rk_p, head_ones, states)
        for q, bi in enumerate(seqs):
            bon_scr[bi] = bonuses[q]
            y_scr[bi] = jnp.concatenate([res[q * WKV_GROUP + h][0] for h in range(WKV_GROUP)], axis=-1)
            for h in range(WKV_GROUP):
                s_scr[bi, h] = res[q * WKV_GROUP + h][1]
        return carry

    lg, lb = lg_ref[...], lb_ref[...]

    def advance(c, states):
        new_states = []
        for bi in range(bb):
            slot = bi * nck + c
            w_end = we_scr[slot]
            ys = []
            for h in range(WKV_GROUP):
                s = states[bi * WKV_GROUP + h]
                sl = slice(h * n, (h + 1) * n)
                ys.append(_dot_nt(rq_scr[slot, h], s) + yc_scr[slot, h])
                new_states.append(s * w_end[:, sl] + _dot(s, tm_scr[slot, h]) + cm_scr[slot, h])
            y_scr[slot] = jnp.concatenate(ys, axis=-1)
        return tuple(new_states)

    if nck == 1:
        lax.fori_loop(0, nslot // nper, solve_from_state, 0)
    else:
        lax.fori_loop(0, nslot // nper, solve, 0)
        states = tuple(s_scr[bi, h] for bi in range(bb) for h in range(WKV_GROUP))
        states = lax.fori_loop(0, nck, advance, states)
        for bi in range(bb):
            for h in range(WKV_GROUP):
                s_scr[bi, h] = states[bi * WKV_GROUP + h]

    def finish(slots):
        ys = [y_scr[slot] for slot in slots]
        ycs = [y - _head_sums(y, head_ones) * (1.0 / n) for y in ys]
        vrs = [_head_sums(yc * yc, head_ones) * (1.0 / n) for yc in ycs]
        for slot, yc, var in zip(slots, ycs, vrs):
            bi, c = slot_index(slot)
            rows = rows_of(c)
            yn = yc * lax.rsqrt(var + RW_GN_EPS) * lg + lb + bon_scr[slot]
            y_ref[bi, rows, :] = (yn * g_ref[bi, rows, :]).astype(y_ref.dtype)

    finish(list(range(nslot)))

    @pl.when(tb == pl.num_programs(2) - 1)
    def _():
        sT_ref[...] = s_scr[...]


def wkv(r, k, v, wpre, apre, gate, s0, kk, ka, rk, lnx_g, lnx_b, *, bb, tblk, ln, nper, out_dtype):
    b, t, d = r.shape
    gl = WKV_GROUP * RW_HEAD
    nslot = bb * (tblk // ln)
    assert b % bb == 0 and t % tblk == 0 and tblk % ln == 0 and d % gl == 0 and nslot % nper == 0
    assert bb == 1 or tblk == ln
    seq = pl.BlockSpec((bb, tblk, gl), lambda bi, g, i: (bi, i, g))
    st = pl.BlockSpec((bb, WKV_GROUP, RW_HEAD, RW_HEAD), lambda bi, g, i: (bi, g, 0, 0))
    vec = pl.BlockSpec((1, gl), lambda bi, g, i: (0, g))
    sq = pltpu.VMEM((nslot, WKV_GROUP, RW_HEAD, RW_HEAD), F32)
    tall = pltpu.VMEM((nslot, WKV_GROUP, ln, RW_HEAD), F32)
    est = (14 * bb * tblk * gl * 4 + 5 * bb * gl * RW_HEAD * 4 + 2 * nslot * gl * (RW_HEAD + ln) * 4
           + nslot * (2 * ln + 8) * gl * 4 + 96 * nper * ln * gl * 4)
    return pl.pallas_call(
        functools.partial(_wkv_kernel, ln=ln, nper=nper),
        grid=(b // bb, d // gl, t // tblk),
        in_specs=[seq] * 6 + [st] + [vec] * 5,
        out_specs=[seq, st],
        out_shape=[jax.ShapeDtypeStruct((b, t, d), out_dtype), jax.ShapeDtypeStruct(s0.shape, F32)],
        scratch_shapes=[pltpu.VMEM((bb, WKV_GROUP, RW_HEAD, RW_HEAD), F32), sq, sq, tall, tall,
                        pltpu.VMEM((nslot, 1, gl), F32), pltpu.VMEM((nslot, ln, gl), F32),
                        pltpu.VMEM((nslot, ln, gl), F32)],
        compiler_params=_cparams(3, est),
        name="wkv",
    )(r, k, v, wpre, apre, gate, s0, kk.reshape(1, d), ka.reshape(1, d), rk.reshape(1, d),
      lnx_g.reshape(1, d), lnx_b.reshape(1, d))


def _tile(m, pref):
    return pref if m % pref == 0 else m


def _pad_rank(w1, w2):
    r = w1.shape[1]
    rp = -(-r // LANES) * LANES
    return jnp.pad(w1, ((0, 0), (0, rp - r))), jnp.pad(w2, ((0, rp - r), (0, 0)))


def _trunk(x, xattn, conv_prev, shift_prev, wkv_prev, paged, p, bias, far, ff_rounded=None):
    ff_out = []
    b, t, d = x.shape
    m = b * t
    tm = _tile(m, 512)
    tseq = _tile(t, 512)
    h = x.reshape(m, d)
    depth = p['norm_g'].shape[0]
    ks, vs, convs, shifts, wkvs = [], [], [], [], []
    for li in range(depth):
        g = p['norm_g'][li]
        if li % 2 == 0:
            ei = li // 2
            lam_init = 0.8 - 0.6 * math.exp(-0.3 * li)
            q_dtype = BF16 if paged is None else F32
            glu, q, k, v = even_in(h, g[0], p['w_in_even'], ei, tm=tm, q_dtype=q_dtype)
            conv_out, conv_state = conv_branch(
                glu.reshape(b, t, C_CONV), conv_prev[ei], p['conv_w'][ei], p['conv_b'][ei], p['conv_ln_g'][ei],
                p['conv_ln_b'][ei], bb=1 if t >= CONV_PAD else 8, tt=_tile(t, 128),
                out_dtype=BF16 if t % BF16_SUBLANES == 0 else F32)
            lam = diff_lambda(p['lambda_q1'][ei], p['lambda_k1'][ei], p['lambda_q2'][ei], p['lambda_k2'][ei], lam_init)
            q3, k3, v3 = (z.reshape(b, t, DA_WIDTH) for z in (q, k, v))
            if paged is None:
                o = attn_prompt(q3, k3, v3, bias, far, lam, p['subln_g'][ei], tq=tseq, lam_init=lam_init)
            else:
                k_pool, v_pool, page_table, n_pool = paged
                o = attn_sample(q3, k3, v3, k_pool, v_pool, page_table + ei * n_pool, bias, far, lam,
                                p['subln_g'][ei], lam_init=lam_init)
            h = out_resnorm([conv_out.reshape(m, C_CONV), o.reshape(m, DA_WIDTH)], p['w_out_even'], ei, h, g[1], tm=tm)
            ks.append(k.reshape(b, t, DA_HEADS, 2 * DA_HEAD_DIM))
            vs.append(v.reshape(b, t, DA_HEADS, 2 * DA_HEAD_DIM))
            convs.append(conv_state)
        else:
            oi = li // 2
            prompt = t >= 64
            loras = [p['rw_w12'][oi] + (p['rw_w0'][oi],), p['rw_a12'][oi] + (p['rw_a0'][oi],),
                     p['rw_g12'][oi] + (jnp.zeros((d,), F32),)]
            xs, wpre, apre, gate, sh = rwkv_mix(
                h.reshape(b, t, d), g[0], shift_prev[oi], p['rw_mix'][oi], loras, bb=1 if prompt else 8,
                tt=_tile(t, 256), out_dtype=BF16 if prompt else F32)
            xs = xs.reshape(3, m, d)
            r = proj(xs, None, p['rw_wr'], oi, x_lead=0, tm=tm, tn=d)
            k = proj(xs, None, p['rw_wk'], oi, x_lead=1, tm=tm, tn=d)
            v = proj(xs, None, p['rw_wv'], oi, x_lead=2, tm=tm, tn=d)
            seqs = [z.reshape(b, t, d) for z in (r, k, v)] + [wpre, apre, gate]
            wkv_params = (wkv_prev[oi], p['rw_kk'][oi], p['rw_ka'][oi], p['rw_rk'][oi].reshape(d), p['rw_lnx_g'][oi],
                          p['rw_lnx_b'][oi])
            if prompt:
                yact, s_fin = wkv(*seqs, *wkv_params, bb=1, tblk=tseq, ln=64, nper=WKV_PROMPT_CHUNKS, out_dtype=BF16)
            else:
                yact, s_fin = wkv(*seqs, *wkv_params, bb=8, tblk=t, ln=t, nper=WKV_SAMPLE_SEQS, out_dtype=F32)
            h = out_resnorm([yact.reshape(m, d)], p['rw_wo'], oi, h, g[1], tm=tm)
            shifts.append(sh.reshape(b, d))
            wkvs.append(s_fin)
        qx = proj(h, g[2], p['x_wq'], li, tm=tm, tn=X_WIDTH)
        ox = xattn(li, qx.reshape(b, t, X_WIDTH))
        h = out_resnorm([ox.reshape(m, X_WIDTH)], p['x_wo'], li, h, g[3], tm=tm)
        if ff_rounded is None:
            h, w1b, w2b = mlp(h, g[4], p['ff_w1'], p['ff_w2'], li, g[5], tm=m, tf=256)
            ff_out.append((w1b[None], w2b[None]))
        else:
            h = mlp(h, g[4], *ff_rounded[li], 0, g[5], tm=tm, tf=1024)
    return (h.reshape(b, t, d), jnp.stack(ks), jnp.stack(vs), jnp.stack(convs), jnp.stack(shifts), jnp.stack(wkvs),
            ff_out)


def kernel(x_prompt, x_sample, cache_attn_k, cache_attn_v, cache_mem_k, cache_mem_v, state_conv, state_shift,
           state_wkv, page_table, mem_prompt, norm_g, mem_norm_g, rel_table, w_in_even, conv_w, conv_b,
           conv_ln_g, conv_ln_b, lambda_q1, lambda_k1, lambda_q2, lambda_k2, subln_g, w_out_even, rw_mix,
           rw_w0, rw_w1, rw_w2, rw_a0, rw_a1, rw_a2, rw_g1, rw_g2, rw_kk, rw_ka, rw_rk, rw_lnx_g, rw_lnx_b,
           rw_wr, rw_wk, rw_wv, rw_wo, x_wq, x_wk, x_wv, x_wo, ff_w1, ff_w2):
    bf = lambda w: w.astype(BF16)
    n_odd = rw_w1.shape[0]
    lora_pairs = lambda w1, w2: [tuple(bf(z) for z in _pad_rank(w1[o], w2[o])) for o in range(n_odd)]
    p = dict(norm_g=norm_g, w_in_even=bf(w_in_even), conv_w=conv_w, conv_b=conv_b, conv_ln_g=conv_ln_g,
             conv_ln_b=conv_ln_b, lambda_q1=lambda_q1, lambda_k1=lambda_k1, lambda_q2=lambda_q2,
             lambda_k2=lambda_k2, subln_g=subln_g, w_out_even=bf(w_out_even), rw_mix=rw_mix, rw_w0=rw_w0,
             rw_w12=lora_pairs(rw_w1, rw_w2), rw_a0=rw_a0, rw_a12=lora_pairs(rw_a1, rw_a2),
             rw_g12=lora_pairs(rw_g1, rw_g2), rw_kk=rw_kk, rw_ka=rw_ka, rw_rk=rw_rk, rw_lnx_g=rw_lnx_g,
             rw_lnx_b=rw_lnx_b, rw_wr=bf(rw_wr), rw_wk=bf(rw_wk), rw_wv=bf(rw_wv), rw_wo=bf(rw_wo), x_wq=bf(x_wq),
             x_wo=bf(x_wo), ff_w1=ff_w1, ff_w2=ff_w2)
    depth = norm_g.shape[0]
    bp, n_mem, d = mem_prompt.shape
    tp = x_prompt.shape[1]
    bd, td, _ = x_sample.shape
    n_even = w_in_even.shape[0]

    mem2d = mem_prompt.reshape(bp * n_mem, d)
    tmem = _tile(bp * n_mem, 512)
    wk_bf, wv_bf = bf(x_wk), bf(x_wv)
    mem_k_layers = [proj(mem2d, mem_norm_g[l], wk_bf, l, tm=tmem, tn=X_WIDTH).reshape(bp, n_mem, X_WIDTH)
                    for l in range(depth)]
    mem_v_layers = [proj(mem2d, mem_norm_g[l], wv_bf, l, tm=tmem, tn=X_WIDTH).reshape(bp, n_mem, X_WIDTH)
                    for l in range(depth)]

    def xattn_prompt(li, qx):
        return xattn_core(qx, mem_k_layers[li], mem_v_layers[li], tt=_tile(tp, 512))

    mem_rows = (depth * bd, n_mem * X_HEADS, X_HEAD_DIM)
    mk_rows, mv_rows = cache_mem_k.reshape(mem_rows), cache_mem_v.reshape(mem_rows)

    def xattn_sample(li, qx):
        return xattn_rows(qx, mk_rows, mv_rows, li)

    far = rel_table[REL_BUCKETS - 1]
    tq = _tile(tp, 512)
    bias_p = bias_tiles(rel_table, tq=tq, tk=tq, nd=2, dstep=tq, keys_first=True)
    bias_s = bias_tiles(rel_table, tq=td, tk=PAGE_SIZE, nd=2, dstep=PAGE_SIZE)

    zc = jnp.zeros((n_even, bp, CONV_W - 1, C_CONV), F32)
    zs = jnp.zeros((n_odd, bp, d), F32)
    zw = jnp.zeros((n_odd, bp, RW_HEADS, RW_HEAD, RW_HEAD), F32)
    n_pool = cache_attn_k.shape[1]
    pool_shape = (n_even * n_pool, PAGE_SIZE, DA_HEADS, 2 * DA_HEAD_DIM)
    pools = (_page_rows(cache_attn_k.reshape(pool_shape)), _page_rows(cache_attn_v.reshape(pool_shape)),
             page_table, n_pool)
    y_s, ak_s, av_s, cv_s, sh_s, wk_s, ff_rounded = _trunk(
        x_sample, xattn_sample, state_conv, state_shift, state_wkv, pools, p, bias_s, far)
    y_p, ak_p, av_p, cv_p, sh_p, wk_p, _ = _trunk(x_prompt, xattn_prompt, zc, zs, zw, None, p, bias_p, far,
                                                  ff_rounded)

    shape_m = (depth, bp, n_mem, X_HEADS, X_HEAD_DIM)
    return (y_p, y_s, ak_p, av_p, ak_s, av_s, cv_p, cv_s, sh_p, sh_s, wk_p, wk_s,
            jnp.stack(mem_k_layers).reshape(shape_m), jnp.stack(mem_v_layers).reshape(shape_m))
```

```python
import functools
import math

import jax
import jax.numpy as jnp
from jax import lax
from jax.experimental import pallas as pl
from jax.experimental.pallas import tpu as pltpu

F32 = jnp.float32
BF16 = jnp.bfloat16

D_MODEL = 2048
PAGE_SIZE = 128
C_CONV = D_MODEL // 2
CONV_W = 31
DA_HEADS = 4
DA_HEAD_DIM = 128
DA_WIDTH = DA_HEADS * 2 * DA_HEAD_DIM
REL_BUCKETS = 32
REL_MAX_EXACT = REL_BUCKETS // 2
REL_MAX_DIST = 128
RW_HEAD = 64
RW_HEADS = D_MODEL // RW_HEAD
RW_GN_EPS = 64e-5
N_MEM = 256
X_HEADS = 4
X_HEAD_DIM = 128
X_WIDTH = X_HEADS * X_HEAD_DIM
D_FF = 4 * D_MODEL
NORM_EPS = 1e-6
LN_EPS = 1e-5

V7X_VMEM_BYTES = 64 * 1024 * 1024
V7X_VMEM_RESERVE = 6 * 1024 * 1024
LANES = 128
SUBLANES = 8
BF16_SUBLANES = 16
MASK_VALUE = -1e30


def _rel_bucket(n):
    if n < REL_MAX_EXACT:
        return n
    return min(REL_MAX_EXACT + int(math.log(n / REL_MAX_EXACT) / math.log(REL_MAX_DIST / REL_MAX_EXACT)
                                   * (REL_BUCKETS - REL_MAX_EXACT)), REL_BUCKETS - 1)


REL_THRESH = tuple(min(n for n in range(4 * REL_MAX_DIST) if _rel_bucket(n) >= m) for m in range(REL_BUCKETS))


def _cparams(n_grid, vmem_estimate):
    limit = min(max(int(vmem_estimate * 1.2), 32 * 1024 * 1024), V7X_VMEM_BYTES - V7X_VMEM_RESERVE)
    return pltpu.CompilerParams(dimension_semantics=("arbitrary",) * n_grid, vmem_limit_bytes=limit)


def _rms(x, g, eps=NORM_EPS):
    return x * lax.rsqrt(jnp.mean(x * x, axis=-1, keepdims=True) + eps) * g


def _mxu(x):
    return x.astype(BF16) if x.shape[0] % BF16_SUBLANES == 0 else x.astype(F32)


def _dot(a, b):
    return jnp.dot(a, b, preferred_element_type=F32)


def _dot_nt(a, b):
    return lax.dot_general(a, b, (((1,), (1,)), ((), ())), preferred_element_type=F32)


def _dot_tn(a, b):
    return lax.dot_general(a, b, (((0,), (0,)), ((), ())), preferred_element_type=F32)


def _even_in_kernel(x_ref, g_ref, w_ref, glu_ref, q_ref, k_ref, v_ref, u_ref, ga_ref):
    j = pl.program_id(1)

    def z():
        return _dot(u_ref[...], w_ref[...])

    @pl.when(j == 0)
    def _():
        u_ref[...] = _rms(x_ref[...], g_ref[...]).astype(BF16)
        ga_ref[...] = z()

    @pl.when(j == 1)
    def _():
        glu_ref[...] = ga_ref[...] * jax.nn.sigmoid(z())

    @pl.when(j == 2)
    def _():
        q_ref[...] = z().astype(q_ref.dtype)

    @pl.when(j == 3)
    def _():
        k_ref[...] = z()

    @pl.when(j == 4)
    def _():
        v_ref[...] = z()


def _stacked(block, index_map, lead):
    return pl.BlockSpec((None,) + tuple(block), lambda *ids: (lead,) + tuple(index_map(*ids)))


def even_in(x, g, w_in, ei, *, tm, q_dtype):
    m, d = x.shape
    c = C_CONV
    assert w_in.shape[1:] == (d, 5 * c) and m % tm == 0
    est = 2 * tm * d * 4 + 2 * d * c * 2 + 2 * tm * c * (4 + 4 + 4 + 4) + tm * d * 2 + tm * c * 4 + 3 * tm * c * 4
    row = lambda i, j: (i, 0)
    return pl.pallas_call(
        _even_in_kernel,
        grid=(m // tm, 5),
        in_specs=[pl.BlockSpec((tm, d), row), pl.BlockSpec((1, d), lambda i, j: (0, 0)),
                  _stacked((d, c), lambda i, j: (0, j), ei)],
        out_specs=[pl.BlockSpec((tm, c), row)] * 4,
        out_shape=[jax.ShapeDtypeStruct((m, c), F32), jax.ShapeDtypeStruct((m, c), q_dtype),
                   jax.ShapeDtypeStruct((m, c), F32), jax.ShapeDtypeStruct((m, c), F32)],
        scratch_shapes=[pltpu.VMEM((tm, d), BF16), pltpu.VMEM((tm, c), F32)],
        compiler_params=_cparams(2, est),
        name="even_in",
    )(x, g.reshape(1, d), w_in)


def _proj_kernel(x_ref, g_ref, w_ref, o_ref, u_ref, *, norm):
    @pl.when(pl.program_id(1) == 0)
    def _():
        x = x_ref[...].astype(F32)
        if norm:
            x = _rms(x, g_ref[...])
        u_ref[...] = x.astype(u_ref.dtype)

    o_ref[...] = _dot(u_ref[...], w_ref[...]).astype(o_ref.dtype)


def proj(x, g, w, w_lead, *, tm, tn, x_lead=None, out_dtype=F32):
    m, kd = x.shape[-2:]
    n = w.shape[-1]
    assert m % tm == 0 and n % tn == 0 and w.shape[1] == kd
    norm = g is not None
    gg = (g if norm else jnp.ones((kd,), F32)).reshape(1, kd)
    est = 2 * tm * kd * x.dtype.itemsize + 2 * kd * tn * 2 + 2 * tm * tn * 4 + tm * kd * 2 + 2 * tm * tn * 4
    x_map = lambda i, j: (i, 0)
    x_spec = pl.BlockSpec((tm, kd), x_map) if x_lead is None else _stacked((tm, kd), x_map, x_lead)
    return pl.pallas_call(
        functools.partial(_proj_kernel, norm=norm),
        grid=(m // tm, n // tn),
        in_specs=[x_spec, pl.BlockSpec((1, kd), lambda i, j: (0, 0)),
                  _stacked((kd, tn), lambda i, j: (0, j), w_lead)],
        out_specs=pl.BlockSpec((tm, tn), lambda i, j: (i, j)),
        out_shape=jax.ShapeDtypeStruct((m, n), out_dtype),
        scratch_shapes=[pltpu.VMEM((tm, kd), BF16)],
        compiler_params=_cparams(2, est),
        name="proj",
    )(x, gg, w)


def _out_resnorm_kernel(*refs, n_in):
    xs = refs[:n_in]
    ws = refs[n_in:2 * n_in]
    h_ref, g_ref, o_ref = refs[2 * n_in:]
    y = _dot(xs[0][...].astype(BF16), ws[0][...])
    for x_ref, w_ref in zip(xs[1:], ws[1:]):
        y = y + _dot(x_ref[...].astype(BF16), w_ref[...])
    o_ref[...] = h_ref[...] + _rms(y, g_ref[...])


def out_resnorm(xs, w, w_lead, h, g, *, tm):
    m, d = h.shape
    n_in = len(xs)
    kd = xs[0].shape[1]
    assert m % tm == 0 and all(x.shape[1] == kd for x in xs) and w.shape[1:] == (n_in * kd, d)
    est = 4 * tm * d * 4 + tm * d * 4
    for x in xs:
        est += 2 * tm * kd * x.dtype.itemsize + 2 * kd * d * 2
    ws = [w] * n_in
    in_specs = [pl.BlockSpec((tm, kd), lambda i: (i, 0)) for x in xs]
    in_specs += [_stacked((kd, d), functools.partial(lambda blk, i: (blk, 0), blk), w_lead) for blk in range(n_in)]
    in_specs += [pl.BlockSpec((tm, d), lambda i: (i, 0)), pl.BlockSpec((1, d), lambda i: (0, 0))]
    return pl.pallas_call(
        functools.partial(_out_resnorm_kernel, n_in=n_in),
        grid=(m // tm,),
        in_specs=in_specs,
        out_specs=pl.BlockSpec((tm, d), lambda i: (i, 0)),
        out_shape=jax.ShapeDtypeStruct((m, d), F32),
        compiler_params=_cparams(1, est),
        name="out_resnorm",
    )(*xs, *ws, h, g.reshape(1, d))


def _mlp_kernel(h_ref, gpre_ref, w1_ref, w2_ref, gpost_ref, o_ref, *rest):
    w_out, (u_ref, acc_ref) = rest[:-2], rest[-2:]
    f = pl.program_id(1)

    @pl.when(f == 0)
    def _():
        u_ref[...] = _rms(h_ref[...], gpre_ref[...]).astype(BF16)
        acc_ref[...] = jnp.zeros_like(acc_ref)

    w1 = w1_ref[...].astype(BF16)
    w2 = w2_ref[...].astype(BF16)
    if w_out:
        w_out[0][...] = w1
        w_out[1][...] = w2
    t = jnp.maximum(_dot(u_ref[...], w1), 0.0)
    acc_ref[...] += _dot((t * t).astype(BF16), w2)

    @pl.when(f == pl.num_programs(1) - 1)
    def _():
        o_ref[...] = h_ref[...] + _rms(acc_ref[...], gpost_ref[...])


def mlp(h, gpre, w1, w2, li, gpost, *, tm, tf):
    m, d = h.shape
    ff = w1.shape[-1]
    assert m % tm == 0 and ff % tf == 0
    emit = w1.dtype != BF16
    wb = w1.dtype.itemsize
    row_map = lambda i, f: (i, 0)
    if emit:
        assert tm == m
        rows = pl.BlockSpec((tm, d), row_map, pipeline_mode=pl.Buffered(1))
        est = 2 * tm * d * 4 + 4 * d * tf * (wb + 2) + tm * d * (2 + 4) + 3 * tm * tf * 4 + tm * d * 4
        out_specs = [rows, pl.BlockSpec((d, tf), lambda i, f: (0, f)), pl.BlockSpec((tf, d), lambda i, f: (f, 0))]
        out_shape = [jax.ShapeDtypeStruct((m, d), F32), jax.ShapeDtypeStruct((d, ff), BF16),
                     jax.ShapeDtypeStruct((ff, d), BF16)]
    else:
        rows = pl.BlockSpec((tm, d), row_map)
        est = 4 * tm * d * 4 + 4 * d * tf * wb + tm * d * (2 + 4) + 3 * tm * tf * 4 + tm * d * 4
        out_specs = [rows]
        out_shape = [jax.ShapeDtypeStruct((m, d), F32)]
    out = pl.pallas_call(
        _mlp_kernel,
        grid=(m // tm, ff // tf),
        in_specs=[rows, pl.BlockSpec((1, d), lambda i, f: (0, 0)),
                  _stacked((d, tf), lambda i, f: (0, f), li), _stacked((tf, d), lambda i, f: (f, 0), li),
                  pl.BlockSpec((1, d), lambda i, f: (0, 0))],
        out_specs=out_specs,
        out_shape=out_shape,
        scratch_shapes=[pltpu.VMEM((tm, d), BF16), pltpu.VMEM((tm, d), F32)],
        compiler_params=_cparams(2, est),
        name="mlp",
    )(h, gpre.reshape(1, d), w1, w2, gpost.reshape(1, d))
    return out if emit else out[0]


CONV_PAD = 32
CONV_ROWS = 32


def _conv_kernel(glu_ref, prev_ref, w_ref, b_ref, lg_ref, lb_ref, out_ref, st_ref, buf_ref):
    i = pl.program_id(1)
    lead = CONV_PAD - (CONV_W - 1)
    bb, tt, c = glu_ref.shape
    rows = min(CONV_ROWS, tt)

    for bi in range(bb):
        @pl.when(i == 0)
        def _():
            buf_ref[bi, 0:lead, :] = jnp.zeros((lead, c), F32)
            buf_ref[bi, lead:CONV_PAD, :] = prev_ref[bi]

        @pl.when(i > 0)
        def _():
            buf_ref[bi, 0:CONV_PAD, :] = buf_ref[bi, tt:tt + CONV_PAD, :]

        buf_ref[bi, CONV_PAD:CONV_PAD + tt, :] = glu_ref[bi]

        for r0 in range(0, tt, rows):
            win = buf_ref[bi, r0:r0 + rows + CONV_PAD, :]
            acc = jnp.zeros((rows, c), F32)
            for phase in range(SUBLANES):
                shifted = win if phase == 0 else pltpu.roll(win, win.shape[0] - phase, axis=0)
                for tap in range(CONV_W):
                    if (lead + tap) % SUBLANES == phase:
                        lo = lead + tap - phase
                        acc = acc + shifted[lo:lo + rows] * w_ref[tap:tap + 1, :]
            y = acc + b_ref[...]
            mu = jnp.mean(y, axis=-1, keepdims=True)
            yc = y - mu
            var = jnp.mean(yc * yc, axis=-1, keepdims=True)
            yn = yc * lax.rsqrt(var + LN_EPS) * lg_ref[...] + lb_ref[...]
            out_ref[bi, r0:r0 + rows, :] = (yn * jax.nn.sigmoid(yn)).astype(out_ref.dtype)

        @pl.when(i == pl.num_programs(1) - 1)
        def _():
            st_ref[bi] = buf_ref[bi, tt + lead:tt + CONV_PAD, :]


def conv_branch(glu, conv_prev, conv_w, conv_b, ln_g, ln_b, *, bb, tt, out_dtype):
    b, t, c = glu.shape
    assert b % bb == 0 and t % tt == 0 and (tt >= CONV_PAD or t == tt)
    est = bb * (4 * tt * c * 4 + 4 * (CONV_W - 1) * c * 4 + (tt + CONV_PAD) * c * 4) + 2 * CONV_W * c * 4
    vec = pl.BlockSpec((1, c), lambda bi, i: (0, 0))
    return pl.pallas_call(
        _conv_kernel,
        grid=(b // bb, t // tt),
        in_specs=[pl.BlockSpec((bb, tt, c), lambda bi, i: (bi, i, 0)),
                  pl.BlockSpec((bb, CONV_W - 1, c), lambda bi, i: (bi, 0, 0)),
                  pl.BlockSpec((CONV_W, c), lambda bi, i: (0, 0)), vec, vec, vec],
        out_specs=[pl.BlockSpec((bb, tt, c), lambda bi, i: (bi, i, 0)),
                   pl.BlockSpec((bb, CONV_W - 1, c), lambda bi, i: (bi, 0, 0))],
        out_shape=[jax.ShapeDtypeStruct((b, t, c), out_dtype), jax.ShapeDtypeStruct((b, CONV_W - 1, c), F32)],
        scratch_shapes=[pltpu.VMEM((bb, tt + CONV_PAD, c), F32)],
        compiler_params=_cparams(2, est),
        name="conv_branch",
    )(glu, conv_prev, conv_w, conv_b.reshape(1, c), ln_g.reshape(1, c), ln_b.reshape(1, c))


def _lambda_kernel(q1_ref, k1_ref, q2_ref, k2_ref, o_ref, *, lam_init):
    s1 = jnp.sum(q1_ref[...] * k1_ref[...], axis=-1, keepdims=True)
    s2 = jnp.sum(q2_ref[...] * k2_ref[...], axis=-1, keepdims=True)
    o_ref[...] = jnp.broadcast_to(jnp.exp(s1) - jnp.exp(s2) + lam_init, o_ref.shape)


def diff_lambda(lq1, lk1, lq2, lk2, lam_init):
    d = lq1.shape[0]
    out = pl.pallas_call(
        functools.partial(_lambda_kernel, lam_init=lam_init),
        out_shape=jax.ShapeDtypeStruct((1, LANES), F32),
        name="diff_lambda",
    )(lq1.reshape(1, d), lk1.reshape(1, d), lq2.reshape(1, d), lk2.reshape(1, d))
    return out[0, :1]


def _bias_kernel(tab_ref, o_ref, *, dstep, q_axis):
    di = pl.program_id(0)
    h = pl.program_id(1)
    shape = o_ref.shape[2:]
    n = (lax.broadcasted_iota(jnp.int32, shape, q_axis) - lax.broadcasted_iota(jnp.int32, shape, 1 - q_axis)
         + di * dstep)
    bias = jnp.full(shape, tab_ref[(REL_BUCKETS - 1) * DA_HEADS + h], F32)
    for m in range(REL_BUCKETS - 2, -1, -1):
        bias = jnp.where(n < REL_THRESH[m + 1], tab_ref[m * DA_HEADS + h], bias)
    o_ref[0, 0] = jnp.where(n < 0, MASK_VALUE, bias)


def bias_tiles(rel_table, *, tq, tk, nd, dstep, keys_first=False):
    shape = (tk, tq) if keys_first else (tq, tk)
    return pl.pallas_call(
        functools.partial(_bias_kernel, dstep=dstep, q_axis=1 if keys_first else 0),
        grid=(nd, DA_HEADS),
        in_specs=[pl.BlockSpec(memory_space=pltpu.SMEM)],
        out_specs=pl.BlockSpec((1, 1) + shape, lambda di, h: (di, h, 0, 0)),
        out_shape=jax.ShapeDtypeStruct((nd, DA_HEADS) + shape, F32),
        compiler_params=_cparams(2, 8 * tq * tk * 4),
        name="bias_tiles",
    )(rel_table.reshape(-1))


def _attn_prompt_kernel(lam_ref, far_ref, q_ref, k_ref, v_ref, bias_ref, sg_ref, o_ref, kb_ref, vt_ref, m_ref, l_ref,
                        acc_ref, *, lam_init):
    h = pl.program_id(1)
    i = pl.program_id(2)
    dh = DA_HEAD_DIM
    nk, tk, _ = kb_ref.shape

    @pl.when(i == 0)
    def _():
        for j in range(nk):
            kb_ref[j] = k_ref[0, j * tk:(j + 1) * tk, :].astype(BF16)
            vt_ref[j] = v_ref[0, j * tk:(j + 1) * tk, :].T.astype(BF16)

    m_ref[...] = jnp.full(m_ref.shape, MASK_VALUE, F32)
    l_ref[...] = jnp.zeros_like(l_ref)
    acc_ref[...] = jnp.zeros_like(acc_ref)
    q = q_ref[0]

    def tile(j, bias):
        kb = kb_ref[j]
        vt = vt_ref[j]
        for c in range(2):
            s = _dot_nt(kb[:, c * dh:(c + 1) * dh], q[:, c * dh:(c + 1) * dh]) * (dh ** -0.5) + bias
            m_prev = m_ref[c]
            m_new = jnp.maximum(m_prev, jnp.max(s, axis=0, keepdims=True))
            alpha = jnp.exp(m_prev - m_new)
            p = jnp.exp(s - m_new)
            l_ref[c] = alpha * l_ref[c] + jnp.sum(p, axis=0, keepdims=True)
            acc_ref[c] = alpha * acc_ref[c] + _dot(vt, p.astype(BF16))
            m_ref[c] = m_new

    def far_tile(j, carry):
        tile(j, far_ref[h])
        return carry

    lax.fori_loop(0, jnp.maximum(i - 1, 0), far_tile, 0)

    @pl.when(i >= 1)
    def _():
        tile(i - 1, bias_ref[1, 0])

    tile(i, bias_ref[0, 0])
    ot = acc_ref[0] / l_ref[0] - lam_ref[0] * (acc_ref[1] / l_ref[1])
    o_ref[0] = (_rms(ot.T, sg_ref[...]) * (1.0 - lam_init)).astype(o_ref.dtype)


def attn_prompt(q, k, v, bias, far, lam, subln_g, *, tq, lam_init):
    b, t, _ = q.shape
    hw = 2 * DA_HEAD_DIM
    nq = t // tq
    est = (4 * tq * hw * 2 + 4 * t * hw * 4 + 2 * t * hw * 2 + 4 * tq * tq * 4 + 2 * tq * hw * 4 + 10 * tq * tq * 4)
    smem = pl.BlockSpec(memory_space=pltpu.SMEM)
    seq = pl.BlockSpec((1, t, hw), lambda bi, h, i: (bi, 0, h))
    return pl.pallas_call(
        functools.partial(_attn_prompt_kernel, lam_init=lam_init),
        grid=(b, DA_HEADS, nq),
        in_specs=[smem, smem, pl.BlockSpec((1, tq, hw), lambda bi, h, i: (bi, i, h)), seq, seq,
                  pl.BlockSpec((2, 1, tq, tq), lambda bi, h, i: (0, h, 0, 0)),
                  pl.BlockSpec((1, hw), lambda bi, h, i: (0, 0))],
        out_specs=pl.BlockSpec((1, tq, hw), lambda bi, h, i: (bi, i, h)),
        out_shape=jax.ShapeDtypeStruct((b, t, DA_WIDTH), BF16),
        scratch_shapes=[pltpu.VMEM((nq, tq, hw), BF16), pltpu.VMEM((nq, hw, tq), BF16),
                        pltpu.VMEM((2, 1, tq), F32), pltpu.VMEM((2, 1, tq), F32), pltpu.VMEM((2, hw, tq), F32)],
        compiler_params=_cparams(3, est),
        name="attn_prompt",
    )(lam, far, q, k, v, bias, subln_g.reshape(1, hw))


PAGES_PER_STEP = 16


PAGE_ROWS = PAGE_SIZE * 2 * DA_HEADS


def _page_rows(pool):
    pg = pool.shape[0]
    x = pool.reshape(pg, PAGE_SIZE, DA_HEADS, 2, DA_HEAD_DIM)
    return jnp.transpose(x, (0, 1, 3, 2, 4)).reshape(pg, PAGE_ROWS, DA_HEAD_DIM)


def _stack_heads(x, half=None):
    dh = DA_HEAD_DIM
    parts = []
    for c in range(2):
        for h in range(DA_HEADS):
            lo = h * 2 * dh + (c if half is None else half) * dh
            parts.append(x[:, lo:lo + dh])
    return jnp.concatenate(parts, axis=0)


def _attn_sample_kernel(pt_ref, lam_ref, q_ref, kn_ref, vn_ref, ptile_ref, ntile_ref, sg_ref, *rest, lam_init):
    pages = rest[:2 * PAGES_PER_STEP]
    o_ref, m_ref, l_ref, acc_ref = rest[2 * PAGES_PER_STEP:]
    s_idx = pl.program_id(1)
    last = pl.num_programs(1) - 1
    dh = DA_HEAD_DIM
    t = q_ref.shape[1]
    nq = DA_HEADS * t
    scale = dh ** -0.5

    @pl.when(s_idx == 0)
    def _():
        m_ref[...] = jnp.full(m_ref.shape, MASK_VALUE, F32)
        l_ref[...] = jnp.zeros_like(l_ref)
        acc_ref[...] = jnp.zeros_like(acc_ref)

    qm = _stack_heads(q_ref[0])

    def update(s, weighted_values):
        m_prev = m_ref[...]
        m_new = jnp.maximum(m_prev, jnp.max(s, axis=-1, keepdims=True))
        alpha = jnp.exp(m_prev - m_new)
        p = jnp.exp(s - m_new)
        l_ref[...] = alpha * l_ref[...] + jnp.sum(p, axis=-1, keepdims=True)
        lo, hi = weighted_values(p)
        acc_ref[:, :dh] = alpha * acc_ref[:, :dh] + lo
        acc_ref[:, dh:] = alpha * acc_ref[:, dh:] + hi
        m_ref[...] = m_new

    scores = []
    for pp in range(PAGES_PER_STEP):
        tile = ptile_ref[0]
        if pp == PAGES_PER_STEP - 1:
            tile = jnp.where(s_idx == last, ptile_ref[1], tile)
        scores.append(_dot_nt(qm, pages[2 * pp][0]) * scale + tile)

    def page_values(p):
        width = p.shape[1]
        top, bot = p[:nq], p[nq:]
        lhs = jnp.concatenate([top, pltpu.roll(bot, width - DA_HEADS, axis=1),
                               pltpu.roll(top, DA_HEADS, axis=1), bot], axis=0)
        res = _dot(lhs[:, :PAGE_ROWS], pages[1][0])
        for pp in range(1, PAGES_PER_STEP):
            res = res + _dot(lhs[:, pp * PAGE_ROWS:(pp + 1) * PAGE_ROWS], pages[2 * pp + 1][0])
        return res[:2 * nq], res[2 * nq:]

    update(jnp.concatenate(scores, axis=1), page_values)

    @pl.when(s_idx == last)
    def _():
        vn = vn_ref[0]
        update(_dot_nt(qm, _stack_heads(kn_ref[0])) * scale + ntile_ref[...],
               lambda p: (_dot(p, _stack_heads(vn, 0)), _dot(p, _stack_heads(vn, 1))))
        o = acc_ref[...] / l_ref[...]
        o = _rms(o[:nq] - lam_ref[0] * o[nq:], sg_ref[...]) * (1.0 - lam_init)
        for h in range(DA_HEADS):
            o_ref[0, :, h * 2 * dh:(h + 1) * 2 * dh] = o[h * t:(h + 1) * t].astype(o_ref.dtype)


def _sample_tiles(bias, far):
    _, nh, t, tk = bias.shape
    c_row = jnp.arange(2).reshape(2, 1, 1, 1, 1, 1)
    h_row = jnp.arange(nh).reshape(1, nh, 1, 1, 1, 1)
    own_page = (c_row == jnp.arange(2).reshape(1, 1, 1, 1, 2, 1)) & (h_row == jnp.arange(nh).reshape(1, 1, 1, 1, 1, nh))
    base = jnp.stack([jnp.broadcast_to(far.reshape(nh, 1, 1), (nh, t, tk)), bias[1]])
    ptile = jnp.where(own_page[None], base[:, None, :, :, :, None, None], MASK_VALUE).reshape(2, 2 * nh * t, tk * 2 * nh)
    own_new = (c_row == jnp.arange(2).reshape(1, 1, 1, 2, 1, 1)) & (h_row == jnp.arange(nh).reshape(1, 1, 1, 1, nh, 1))
    ntile = jnp.where(own_new, bias[0][None, :, :, None, None, :t], MASK_VALUE).reshape(2 * nh * t, 2 * nh * t)
    return ptile, ntile


def attn_sample(q, k_new, v_new, k_rows, v_rows, page_table, bias, far, lam, subln_g, *, lam_init):
    b, t, w = q.shape
    n_pages = page_table.shape[1]
    assert n_pages % PAGES_PER_STEP == 0 and k_rows.shape[1:] == (PAGE_ROWS, DA_HEAD_DIM)
    hw = 2 * DA_HEAD_DIM
    ptile, ntile = _sample_tiles(bias, far)
    smem = pl.BlockSpec(memory_space=pltpu.SMEM)
    row = pl.BlockSpec((1, t, w), lambda bi, s, pt: (bi, 0, 0))

    def page_spec(pp):
        return pl.BlockSpec((1, PAGE_ROWS, DA_HEAD_DIM), lambda bi, s, pt: (pt[bi, s * PAGES_PER_STEP + pp], 0, 0))

    in_specs = [smem, row, row, row,
                pl.BlockSpec(ptile.shape, lambda bi, s, pt: (0, 0, 0)),
                pl.BlockSpec(ntile.shape, lambda bi, s, pt: (0, 0)),
                pl.BlockSpec((1, hw), lambda bi, s, pt: (0, 0))]
    pools = []
    for pp in range(PAGES_PER_STEP):
        in_specs += [page_spec(pp), page_spec(pp)]
        pools += [k_rows, v_rows]
    nr = 2 * DA_HEADS * t
    est = (4 * PAGES_PER_STEP * PAGE_ROWS * DA_HEAD_DIM * 4 + 16 * t * w * 4 + 4 * ptile.size * 4
           + 12 * nr * PAGES_PER_STEP * PAGE_ROWS * 4)
    return pl.pallas_call(
        functools.partial(_attn_sample_kernel, lam_init=lam_init),
        grid_spec=pltpu.PrefetchScalarGridSpec(
            num_scalar_prefetch=1,
            grid=(b, n_pages // PAGES_PER_STEP),
            in_specs=in_specs,
            out_specs=pl.BlockSpec((1, t, w), lambda bi, s, pt: (bi, 0, 0)),
            scratch_shapes=[pltpu.VMEM((nr, 1), F32), pltpu.VMEM((nr, 1), F32), pltpu.VMEM((nr, hw), F32)]),
        out_shape=jax.ShapeDtypeStruct((b, t, w), F32),
        compiler_params=_cparams(2, est),
        name="attn_sample",
    )(page_table, lam, q, k_new, v_new, ptile, ntile, subln_g.reshape(1, hw), *pools)


def _xattn_kernel(q_ref, mk_ref, mv_ref, o_ref):
    q = _mxu(q_ref[0])
    mk = mk_ref[0].astype(q.dtype)
    mv = mv_ref[0].astype(q.dtype)
    dh = X_HEAD_DIM
    for h in range(X_HEADS):
        sl = slice(h * dh, (h + 1) * dh)
        s = _dot_nt(q[:, sl], mk[:, sl]) * (dh ** -0.5)
        p = jnp.exp(s - jnp.max(s, axis=-1, keepdims=True))
        p = p / jnp.sum(p, axis=-1, keepdims=True)
        o_ref[0, :, sl] = _dot(p.astype(q.dtype), mv[:, sl]).astype(o_ref.dtype)


def xattn_core(q, mk, mv, *, tt):
    b, t, w = q.shape
    n_mem = mk.shape[1]
    est = 4 * tt * w * 4 + 4 * n_mem * w * 4 + 6 * tt * n_mem * 4
    return pl.pallas_call(
        _xattn_kernel,
        grid=(b, t // tt),
        in_specs=[pl.BlockSpec((1, tt, w), lambda bi, i: (bi, i, 0)),
                  pl.BlockSpec((1, n_mem, w), lambda bi, i: (bi, 0, 0)),
                  pl.BlockSpec((1, n_mem, w), lambda bi, i: (bi, 0, 0))],
        out_specs=pl.BlockSpec((1, tt, w), lambda bi, i: (bi, i, 0)),
        out_shape=jax.ShapeDtypeStruct((b, t, w), F32),
        compiler_params=_cparams(2, est),
        name="xattn_core",
    )(q, mk, mv)


XATTN_SEQS = 8


def _xattn_rows_kernel(q_ref, mk_ref, mv_ref, mask_ref, o_ref):
    dh = X_HEAD_DIM
    t = q_ref.shape[1]
    for bi in range(q_ref.shape[0]):
        q = q_ref[bi]
        qm = jnp.concatenate([q[:, h * dh:(h + 1) * dh] for h in range(X_HEADS)], axis=0)
        s = _dot_nt(qm, mk_ref[bi]) * (dh ** -0.5) + mask_ref[...]
        p = jnp.exp(s - jnp.max(s, axis=-1, keepdims=True))
        p = p / jnp.sum(p, axis=-1, keepdims=True)
        res = _dot(p, mv_ref[bi])
        for h in range(X_HEADS):
            o_ref[bi, :, h * dh:(h + 1) * dh] = res[h * t:(h + 1) * t]


def xattn_rows(q, mk_rows, mv_rows, li):
    b, t, w = q.shape
    nr = mk_rows.shape[1]
    assert b % XATTN_SEQS == 0 and mk_rows.shape[2] == X_HEAD_DIM
    row_head = jnp.arange(X_HEADS * t).reshape(-1, 1) // t
    col_head = jnp.arange(nr).reshape(1, -1) % X_HEADS
    mask = jnp.where(row_head == col_head, 0.0, MASK_VALUE).astype(F32)
    nb = b // XATTN_SEQS
    mem = pl.BlockSpec((XATTN_SEQS, nr, X_HEAD_DIM), lambda i: (li * nb + i, 0, 0))
    est = 4 * XATTN_SEQS * (nr * X_HEAD_DIM + 2 * t * w) * 4 + 2 * mask.size * 4 + 8 * X_HEADS * t * nr * 4
    return pl.pallas_call(
        _xattn_rows_kernel,
        grid=(nb,),
        in_specs=[pl.BlockSpec((XATTN_SEQS, t, w), lambda i: (i, 0, 0)), mem, mem,
                  pl.BlockSpec(mask.shape, lambda i: (0, 0))],
        out_specs=pl.BlockSpec((XATTN_SEQS, t, w), lambda i: (i, 0, 0)),
        out_shape=jax.ShapeDtypeStruct((b, t, w), F32),
        compiler_params=_cparams(1, est),
        name="xattn_rows",
    )(q, mk_rows, mv_rows, mask)


RW_MIX_PROJ = (0, 2, 3)
RW_MIX_LORA = ((1, "tanh"), (4, None), (5, "sigmoid"))


def _rwkv_mix_kernel(h_ref, g_ref, sp_ref, mix_ref, *rest):
    loras = [rest[3 * n:3 * n + 3] for n in range(3)]
    xs_ref, lo_refs, sh_ref, carry_ref = rest[9], rest[10:13], rest[13], rest[14]
    i = pl.program_id(1)
    bb, tt, _ = h_ref.shape
    assert (bb * tt) % BF16_SUBLANES == 0

    @pl.when(i == 0)
    def _():
        carry_ref[...] = sp_ref[...]

    lora_in = [[], [], []]
    for bi in range(bb):
        u = _rms(h_ref[bi], g_ref[...])
        row = lax.broadcasted_iota(jnp.int32, u.shape, 0)
        prev = jnp.where(row == 0, carry_ref[bi], pltpu.roll(u, 1, axis=0))
        xx = prev - u
        for slot, n in enumerate(RW_MIX_PROJ):
            xs_ref[slot, bi] = (u + xx * mix_ref[n:n + 1, :]).astype(xs_ref.dtype)
        for slot, (n, _) in enumerate(RW_MIX_LORA):
            lora_in[slot].append((u + xx * mix_ref[n:n + 1, :]).astype(BF16))
        carry_ref[bi] = u[tt - 1:tt, :]
        sh_ref[bi] = u[tt - 1:tt, :]

    for (w1_ref, w2_ref, b_ref), o_ref, xs, (_, act) in zip(loras, lo_refs, lora_in, RW_MIX_LORA):
        mid = _dot(jnp.concatenate(xs, axis=0), w1_ref[...])
        if act == "tanh":
            mid = jnp.tanh(mid)
        elif act == "sigmoid":
            mid = jax.nn.sigmoid(mid)
        out = _dot(mid.astype(BF16), w2_ref[...]) + b_ref[...]
        for bi in range(bb):
            o_ref[bi] = out[bi * tt:(bi + 1) * tt]


def rwkv_mix(h, g, shift_prev, mix, loras, *, bb, tt, out_dtype):
    b, t, d = h.shape
    assert b % bb == 0 and t % tt == 0
    est = bb * (2 * tt * d * 4 + 6 * tt * d * 4 + 6 * tt * d * 4 + 12 * tt * d * 4)
    const = lambda bi, i: (0, 0)
    seq = pl.BlockSpec((bb, tt, d), lambda bi, i: (bi, i, 0))
    lora_args, lora_specs = [], []
    for w1, w2, bias in loras:
        lora_args += [w1, w2, bias.reshape(1, d)]
        lora_specs += [pl.BlockSpec(w1.shape, const), pl.BlockSpec(w2.shape, const), pl.BlockSpec((1, d), const)]
        est += 4 * (w1.size + w2.size) * 2
    return pl.pallas_call(
        _rwkv_mix_kernel,
        grid=(b // bb, t // tt),
        in_specs=[seq, pl.BlockSpec((1, d), const), pl.BlockSpec((bb, 1, d), lambda bi, i: (bi, 0, 0)),
                  pl.BlockSpec((6, d), const)] + lora_specs,
        out_specs=[pl.BlockSpec((3, bb, tt, d), lambda bi, i: (0, bi, i, 0)), seq, seq, seq,
                   pl.BlockSpec((bb, 1, d), lambda bi, i: (bi, 0, 0))],
        out_shape=[jax.ShapeDtypeStruct((3, b, t, d), out_dtype)] + [jax.ShapeDtypeStruct((b, t, d), F32)] * 3
        + [jax.ShapeDtypeStruct((b, 1, d), F32)],
        scratch_shapes=[pltpu.VMEM((bb, 1, d), F32)],
        compiler_params=_cparams(2, est),
        name="rwkv_mix",
    )(h, g.reshape(1, d), shift_prev.reshape(b, 1, d), mix, *lora_args)


WKV_GROUP = 4
WKV_PROMPT_CHUNKS = 8
WKV_SAMPLE_SEQS = 8


def _cumsum_rows(x):
    n = x.shape[0]
    row = lax.broadcasted_iota(jnp.int32, x.shape, 0)
    step = 1
    while step < n:
        x = x + jnp.where(row >= step, pltpu.roll(x, step, axis=0), 0.0)
        step *= 2
    return x


def _head_ones(width):
    shift = RW_HEAD.bit_length() - 1
    assert RW_HEAD == 1 << shift
    gi = lax.shift_right_logical(lax.broadcasted_iota(jnp.int32, (width, width), 0), shift)
    gj = lax.shift_right_logical(lax.broadcasted_iota(jnp.int32, (width, width), 1), shift)
    return jnp.where(gi == gj, 1.0, 0.0).astype(F32)


def _head_sums(x, head_ones):
    hi = x.astype(BF16).astype(F32)
    return _dot(hi, head_ones) + _dot(x - hi, head_ones)


def _wkv_solve_chunks(chunks, kk_p, ka_p, rk_p, head_ones, states=None):
    ln = chunks[0][0].shape[0]
    n = RW_HEAD
    ri = lax.broadcasted_iota(jnp.int32, (ln, 2 * ln), 0)
    ci = lax.broadcasted_iota(jnp.int32, (ln, 2 * ln), 1)
    ci = jnp.where(ci < ln, ci, ci - ln)
    mask_a = ci < ri
    mask_r = ci <= ri
    nlev = max(1, (ln - 1).bit_length())
    zeros = jnp.zeros((ln, n), F32)

    pairs = []
    w_ends = []
    bonuses = []
    for r, kr, v, wp, ap in chunks:
        a = jax.nn.sigmoid(ap)
        lw = -jnp.exp(-jax.nn.softplus(-wp) - 0.5)
        cum = _cumsum_rows(lw)
        cl = cum[ln - 1:ln, :]
        w_in = jnp.exp(cum)
        w_ex = jnp.exp(cum - lw)
        w_inv = jnp.exp(-cum)
        w_rem = jnp.exp(cl - cum)
        w_ends.append(jnp.exp(cl))
        kkv = kr * kk_p
        k = kr * (1.0 + (a - 1.0) * ka_p)
        kkn = kkv / jnp.maximum(jnp.sqrt(_head_sums(kkv * kkv, head_ones)), 1e-12)
        bt = kkn * a
        bonuses.append(_head_sums(r * k * rk_p, head_ones) * v)
        wide = dict(a_til=-kkn * w_ex, r_til=r * w_in, b_til=bt * w_inv, k_til=k * w_inv, bp=bt * w_rem,
                    kp=k * w_rem, v=v)
        for h in range(WKV_GROUP):
            pairs.append({name: z[:, h * n:(h + 1) * n] for name, z in wide.items()})

    for p in pairs:
        p['amat'] = _dot_nt(jnp.concatenate([p['a_til'], p['r_til']], axis=0),
                            jnp.concatenate([p['b_til'], p['k_til']], axis=0))
    for p in pairs:
        p['a_rows'] = jnp.where(mask_a, p['amat'][:ln], 0.0)
        p['r_rows'] = jnp.where(mask_r, p['amat'][ln:], 0.0)
        p['pw'] = p['a_rows'][:, :ln]
        p['akv'] = _dot(p['a_rows'], jnp.concatenate([zeros, p['v']], axis=0))
    eye = jnp.where(lax.broadcasted_iota(jnp.int32, (ln, ln), 0) == lax.broadcasted_iota(jnp.int32, (ln, ln), 1),
                    1.0, 0.0)
    for p in pairs:
        p['inv'] = eye + p['pw']
    for _ in range(1, nlev):
        for p in pairs:
            p['pw'] = _dot(p['pw'], p['pw'])
        for p in pairs:
            p['inv'] = p['inv'] + _dot(p['pw'], p['inv'])
    out = []
    if states is None:
        for p in pairs:
            p['x'] = _dot(p['inv'], jnp.concatenate([p['a_til'], p['akv']], axis=1))
        for p in pairs:
            tc = _dot_tn(p['x'], p['bp'])
            vk = _dot_tn(p['v'], p['kp'])
            ry = _dot(p['r_rows'], jnp.concatenate([p['x'], jnp.concatenate([zeros, p['v']], axis=1)], axis=0))
            out.append((tc[:n], tc[n:] + vk, p['r_til'] + ry[:, :n], ry[:, n:]))
    else:
        for p, s0 in zip(pairs, states):
            ars = _dot_nt(jnp.concatenate([p['a_til'], p['r_til']], axis=0), s0)
            p['as'], p['rs'] = ars[:ln], ars[ln:]
        for p in pairs:
            p['u'] = _dot(p['inv'], p['as'] + p['akv'])
        for p in pairs:
            p['uv'] = jnp.concatenate([p['u'], p['v']], axis=0)
            p['ry'] = _dot(p['r_rows'], p['uv'])
        for p in pairs:
            p['ds'] = _dot_tn(p['uv'], jnp.concatenate([p['bp'], p['kp']], axis=0))
        for q, (p, s0) in enumerate(zip(pairs, states)):
            sl = slice((q % WKV_GROUP) * n, (q % WKV_GROUP + 1) * n)
            out.append((p['rs'] + p['ry'], s0 * w_ends[q // WKV_GROUP][:, sl] + p['ds']))
    return out, w_ends, bonuses


def _wkv_kernel(r_ref, k_ref, v_ref, w_ref, a_ref, g_ref, s0_ref, kk_ref, ka_ref, rk_ref, lg_ref, lb_ref,
                y_ref, sT_ref, s_scr, tm_scr, cm_scr, rq_scr, yc_scr, we_scr, bon_scr, y_scr, *, ln, nper):
    tb = pl.program_id(2)
    bb, tblk, _ = r_ref.shape
    nck = tblk // ln
    nslot = bb * nck
    n = RW_HEAD

    @pl.when(tb == 0)
    def _():
        s_scr[...] = s0_ref[...]

    kk_p, ka_p, rk_p = kk_ref[...], ka_ref[...], rk_ref[...]
    head_ones = _head_ones(r_ref.shape[2])

    def slot_index(slot):
        if nck == 1:
            return slot, 0
        return 0, slot

    def rows_of(c):
        return pl.ds(c * ln if isinstance(c, int) else pl.multiple_of(c * ln, ln), ln)

    def solve(it, carry):
        chunks = []
        for q in range(nper):
            bi, c = slot_index(it * nper + q)
            rows = rows_of(c)
            chunks.append(tuple(ref[bi, rows, :] for ref in (r_ref, k_ref, v_ref, w_ref, a_ref)))
        mats, w_ends, bonuses = _wkv_solve_chunks(chunks, kk_p, ka_p, rk_p, head_ones)
        for q in range(nper):
            slot = it * nper + q
            we_scr[slot] = w_ends[q]
            bon_scr[slot] = bonuses[q]
            for h in range(WKV_GROUP):
                tmat, cmat, rq, yc = mats[q * WKV_GROUP + h]
                tm_scr[slot, h] = tmat
                cm_scr[slot, h] = cmat
                rq_scr[slot, h] = rq
                yc_scr[slot, h] = yc
        return carry

    def solve_from_state(it, carry):
        seqs = [it * nper + q for q in range(nper)]
        states = [s_scr[bi, h] for bi in seqs for h in range(WKV_GROUP)]
        chunks = [tuple(ref[bi] for ref in (r_ref, k_ref, v_ref, w_ref, a_ref)) for bi in seqs]
        res, _, bonuses = _wkv_solve_chunks(chunks, kk_p, ka_p, rk_p, head_ones, states)
        for q, bi in enumerate(seqs):
            bon_scr[bi] = bonuses[q]
            y_scr[bi] = jnp.concatenate([res[q * WKV_GROUP + h][0] for h in range(WKV_GROUP)], axis=-1)
            for h in range(WKV_GROUP):
                s_scr[bi, h] = res[q * WKV_GROUP + h][1]
        return carry

    lg, lb = lg_ref[...], lb_ref[...]

    def advance(c, states):
        new_states = []
        for bi in range(bb):
            slot = bi * nck + c
            w_end = we_scr[slot]
            ys = []
            for h in range(WKV_GROUP):
                s = states[bi * WKV_GROUP + h]
                sl = slice(h * n, (h + 1) * n)
                ys.append(_dot_nt(rq_scr[slot, h], s) + yc_scr[slot, h])
                new_states.append(s * w_end[:, sl] + _dot(s, tm_scr[slot, h]) + cm_scr[slot, h])
            y_scr[slot] = jnp.concatenate(ys, axis=-1)
        return tuple(new_states)

    if nck == 1:
        lax.fori_loop(0, nslot // nper, solve_from_state, 0)
    else:
        lax.fori_loop(0, nslot // nper, solve, 0)
        states = tuple(s_scr[bi, h] for bi in range(bb) for h in range(WKV_GROUP))
        states = lax.fori_loop(0, nck, advance, states)
        for bi in range(bb):
            for h in range(WKV_GROUP):
                s_scr[bi, h] = states[bi * WKV_GROUP + h]

    def finish(slots):
        ys = [y_scr[slot] for slot in slots]
        ycs = [y - _head_sums(y, head_ones) * (1.0 / n) for y in ys]
        vrs = [_head_sums(yc * yc, head_ones) * (1.0 / n) for yc in ycs]
        for slot, yc, var in zip(slots, ycs, vrs):
            bi, c = slot_index(slot)
            rows = rows_of(c)
            yn = yc * lax.rsqrt(var + RW_GN_EPS) * lg + lb + bon_scr[slot]
            y_ref[bi, rows, :] = (yn * g_ref[bi, rows, :]).astype(y_ref.dtype)

    finish(list(range(nslot)))

    @pl.when(tb == pl.num_programs(2) - 1)
    def _():
        sT_ref[...] = s_scr[...]


def wkv(r, k, v, wpre, apre, gate, s0, kk, ka, rk, lnx_g, lnx_b, *, bb, tblk, ln, nper, out_dtype):
    b, t, d = r.shape
    gl = WKV_GROUP * RW_HEAD
    nslot = bb * (tblk // ln)
    assert b % bb == 0 and t % tblk == 0 and tblk % ln == 0 and d % gl == 0 and nslot % nper == 0
    assert bb == 1 or tblk == ln
    seq = pl.BlockSpec((bb, tblk, gl), lambda bi, g, i: (bi, i, g))
    st = pl.BlockSpec((bb, WKV_GROUP, RW_HEAD, RW_HEAD), lambda bi, g, i: (bi, g, 0, 0))
    vec = pl.BlockSpec((1, gl), lambda bi, g, i: (0, g))
    sq = pltpu.VMEM((nslot, WKV_GROUP, RW_HEAD, RW_HEAD), F32)
    tall = pltpu.VMEM((nslot, WKV_GROUP, ln, RW_HEAD), F32)
    est = (14 * bb * tblk * gl * 4 + 5 * bb * gl * RW_HEAD * 4 + 2 * nslot * gl * (RW_HEAD + ln) * 4
           + nslot * (2 * ln + 8) * gl * 4 + 96 * nper * ln * gl * 4)
    return pl.pallas_call(
        functools.partial(_wkv_kernel, ln=ln, nper=nper),
        grid=(b // bb, d // gl, t // tblk),
        in_specs=[seq] * 6 + [st] + [vec] * 5,
        out_specs=[seq, st],
        out_shape=[jax.ShapeDtypeStruct((b, t, d), out_dtype), jax.ShapeDtypeStruct(s0.shape, F32)],
        scratch_shapes=[pltpu.VMEM((bb, WKV_GROUP, RW_HEAD, RW_HEAD), F32), sq, sq, tall, tall,
                        pltpu.VMEM((nslot, 1, gl), F32), pltpu.VMEM((nslot, ln, gl), F32),
                        pltpu.VMEM((nslot, ln, gl), F32)],
        compiler_params=_cparams(3, est),
        name="wkv",
    )(r, k, v, wpre, apre, gate, s0, kk.reshape(1, d), ka.reshape(1, d), rk.reshape(1, d),
      lnx_g.reshape(1, d), lnx_b.reshape(1, d))


def _tile(m, pref):
    return pref if m % pref == 0 else m


def _pad_rank(w1, w2):
    r = w1.shape[1]
    rp = -(-r // LANES) * LANES
    return jnp.pad(w1, ((0, 0), (0, rp - r))), jnp.pad(w2, ((0, rp - r), (0, 0)))


def _trunk(x, xattn, conv_prev, shift_prev, wkv_prev, paged, p, bias, far, ff_rounded=None):
    ff_out = []
    b, t, d = x.shape
    m = b * t
    tm = _tile(m, 512)
    tseq = _tile(t, 512)
    h = x.reshape(m, d)
    depth = p['norm_g'].shape[0]
    ks, vs, convs, shifts, wkvs = [], [], [], [], []
    for li in range(depth):
        g = p['norm_g'][li]
        if li % 2 == 0:
            ei = li // 2
            lam_init = 0.8 - 0.6 * math.exp(-0.3 * li)
            q_dtype = BF16 if paged is None else F32
            glu, q, k, v = even_in(h, g[0], p['w_in_even'], ei, tm=tm, q_dtype=q_dtype)
            conv_out, conv_state = conv_branch(
                glu.reshape(b, t, C_CONV), conv_prev[ei], p['conv_w'][ei], p['conv_b'][ei], p['conv_ln_g'][ei],
                p['conv_ln_b'][ei], bb=1 if t >= CONV_PAD else 8, tt=_tile(t, 128),
                out_dtype=BF16 if t % BF16_SUBLANES == 0 else F32)
            lam = diff_lambda(p['lambda_q1'][ei], p['lambda_k1'][ei], p['lambda_q2'][ei], p['lambda_k2'][ei], lam_init)
            q3, k3, v3 = (z.reshape(b, t, DA_WIDTH) for z in (q, k, v))
            if paged is None:
                o = attn_prompt(q3, k3, v3, bias, far, lam, p['subln_g'][ei], tq=tseq, lam_init=lam_init)
            else:
                k_pool, v_pool, page_table, n_pool = paged
                o = attn_sample(q3, k3, v3, k_pool, v_pool, page_table + ei * n_pool, bias, far, lam,
                                p['subln_g'][ei], lam_init=lam_init)
            h = out_resnorm([conv_out.reshape(m, C_CONV), o.reshape(m, DA_WIDTH)], p['w_out_even'], ei, h, g[1], tm=tm)
            ks.append(k.reshape(b, t, DA_HEADS, 2 * DA_HEAD_DIM))
            vs.append(v.reshape(b, t, DA_HEADS, 2 * DA_HEAD_DIM))
            convs.append(conv_state)
        else:
            oi = li // 2
            prompt = t >= 64
            loras = [p['rw_w12'][oi] + (p['rw_w0'][oi],), p['rw_a12'][oi] + (p['rw_a0'][oi],),
                     p['rw_g12'][oi] + (jnp.zeros((d,), F32),)]
            xs, wpre, apre, gate, sh = rwkv_mix(
                h.reshape(b, t, d), g[0], shift_prev[oi], p['rw_mix'][oi], loras, bb=1 if prompt else 8,
                tt=_tile(t, 256), out_dtype=BF16 if prompt else F32)
            xs = xs.reshape(3, m, d)
            r = proj(xs, None, p['rw_wr'], oi, x_lead=0, tm=tm, tn=d)
            k = proj(xs, None, p['rw_wk'], oi, x_lead=1, tm=tm, tn=d)
            v = proj(xs, None, p['rw_wv'], oi, x_lead=2, tm=tm, tn=d)
            seqs = [z.reshape(b, t, d) for z in (r, k, v)] + [wpre, apre, gate]
            wkv_params = (wkv_prev[oi], p['rw_kk'][oi], p['rw_ka'][oi], p['rw_rk'][oi].reshape(d), p['rw_lnx_g'][oi],
                          p['rw_lnx_b'][oi])
            if prompt:
                yact, s_fin = wkv(*seqs, *wkv_params, bb=1, tblk=tseq, ln=64, nper=WKV_PROMPT_CHUNKS, out_dtype=BF16)
            else:
                yact, s_fin = wkv(*seqs, *wkv_params, bb=8, tblk=t, ln=t, nper=WKV_SAMPLE_SEQS, out_dtype=F32)
            h = out_resnorm([yact.reshape(m, d)], p['rw_wo'], oi, h, g[1], tm=tm)
            shifts.append(sh.reshape(b, d))
            wkvs.append(s_fin)
        qx = proj(h, g[2], p['x_wq'], li, tm=tm, tn=X_WIDTH)
        ox = xattn(li, qx.reshape(b, t, X_WIDTH))
        h = out_resnorm([ox.reshape(m, X_WIDTH)], p['x_wo'], li, h, g[3], tm=tm)
        if ff_rounded is None:
            h, w1b, w2b = mlp(h, g[4], p['ff_w1'], p['ff_w2'], li, g[5], tm=m, tf=256)
            ff_out.append((w1b[None], w2b[None]))
        else:
            h = mlp(h, g[4], *ff_rounded[li], 0, g[5], tm=tm, tf=1024)
    return (h.reshape(b, t, d), jnp.stack(ks), jnp.stack(vs), jnp.stack(convs), jnp.stack(shifts), jnp.stack(wkvs),
            ff_out)


def kernel(x_prompt, x_sample, cache_attn_k, cache_attn_v, cache_mem_k, cache_mem_v, state_conv, state_shift,
           state_wkv, page_table, mem_prompt, norm_g, mem_norm_g, rel_table, w_in_even, conv_w, conv_b,
           conv_ln_g, conv_ln_b, lambda_q1, lambda_k1, lambda_q2, lambda_k2, subln_g, w_out_even, rw_mix,
           rw_w0, rw_w1, rw_w2, rw_a0, rw_a1, rw_a2, rw_g1, rw_g2, rw_kk, rw_ka, rw_rk, rw_lnx_g, rw_lnx_b,
           rw_wr, rw_wk, rw_wv, rw_wo, x_wq, x_wk, x_wv, x_wo, ff_w1, ff_w2):
    bf = lambda w: w.astype(BF16)
    n_odd = rw_w1.shape[0]
    lora_pairs = lambda w1, w2: [tuple(bf(z) for z in _pad_rank(w1[o], w2[o])) for o in range(n_odd)]
    p = dict(norm_g=norm_g, w_in_even=bf(w_in_even), conv_w=conv_w, conv_b=conv_b, conv_ln_g=conv_ln_g,
             conv_ln_b=conv_ln_b, lambda_q1=lambda_q1, lambda_k1=lambda_k1, lambda_q2=lambda_q2,
             lambda_k2=lambda_k2, subln_g=subln_g, w_out_even=bf(w_out_even), rw_mix=rw_mix, rw_w0=rw_w0,
             rw_w12=lora_pairs(rw_w1, rw_w2), rw_a0=rw_a0, rw_a12=lora_pairs(rw_a1, rw_a2),
             rw_g12=lora_pairs(rw_g1, rw_g2), rw_kk=rw_kk, rw_ka=rw_ka, rw_rk=rw_rk, rw_lnx_g=rw_lnx_g,
             rw_lnx_b=rw_lnx_b, rw_wr=bf(rw_wr), rw_wk=bf(rw_wk), rw_wv=bf(rw_wv), rw_wo=bf(rw_wo), x_wq=bf(x_wq),
             x_wo=bf(x_wo), ff_w1=ff_w1, ff_w2=ff_w2)
    depth = norm_g.shape[0]
    bp, n_mem, d = mem_prompt.shape
    tp = x_prompt.shape[1]
    bd, td, _ = x_sample.shape
    n_even = w_in_even.shape[0]

    mem2d = mem_prompt.reshape(bp * n_mem, d)
    tmem = _tile(bp * n_mem, 512)
    wk_bf, wv_bf = bf(x_wk), bf(x_wv)
    mem_k_layers = [proj(mem2d, mem_norm_g[l], wk_bf, l, tm=tmem, tn=X_WIDTH).reshape(bp, n_mem, X_WIDTH)
                    for l in range(depth)]
    mem_v_layers = [proj(mem2d, mem_norm_g[l], wv_bf, l, tm=tmem, tn=X_WIDTH).reshape(bp, n_mem, X_WIDTH)
                    for l in range(depth)]

    def xattn_prompt(li, qx):
        return xattn_core(qx, mem_k_layers[li], mem_v_layers[li], tt=_tile(tp, 512))

    mem_rows = (depth * bd, n_mem * X_HEADS, X_HEAD_DIM)
    mk_rows, mv_rows = cache_mem_k.reshape(mem_rows), cache_mem_v.reshape(mem_rows)

    def xattn_sample(li, qx):
        return xattn_rows(qx, mk_rows, mv_rows, li)

    far = rel_table[REL_BUCKETS - 1]
    tq = _tile(tp, 512)
    bias_p = bias_tiles(rel_table, tq=tq, tk=tq, nd=2, dstep=tq, keys_first=True)
    bias_s = bias_tiles(rel_table, tq=td, tk=PAGE_SIZE, nd=2, dstep=PAGE_SIZE)

    zc = jnp.zeros((n_even, bp, CONV_W - 1, C_CONV), F32)
    zs = jnp.zeros((n_odd, bp, d), F32)
    zw = jnp.zeros((n_odd, bp, RW_HEADS, RW_HEAD, RW_HEAD), F32)
    n_pool = cache_attn_k.shape[1]
    pool_shape = (n_even * n_pool, PAGE_SIZE, DA_HEADS, 2 * DA_HEAD_DIM)
    pools = (_page_rows(cache_attn_k.reshape(pool_shape)), _page_rows(cache_attn_v.reshape(pool_shape)),
             page_table, n_pool)
    y_s, ak_s, av_s, cv_s, sh_s, wk_s, ff_rounded = _trunk(
        x_sample, xattn_sample, state_conv, state_shift, state_wkv, pools, p, bias_s, far)
    y_p, ak_p, av_p, cv_p, sh_p, wk_p, _ = _trunk(x_prompt, xattn_prompt, zc, zs, zw, None, p, bias_p, far,
                                                  ff_rounded)

    shape_m = (depth, bp, n_mem, X_HEADS, X_HEAD_DIM)
    return (y_p, y_s, ak_p, av_p, ak_s, av_s, cv_p, cv_s, sh_p, sh_s, wk_p, wk_s,
            jnp.stack(mem_k_layers).reshape(shape_m), jnp.stack(mem_v_layers).reshape(shape_m))
```
